```python
import math
import jax, jax.numpy as jnp
from jax import lax
import numpy as np

D_MODEL = 1024
BATCH = 8
SEQ = 2048
DEPTH = 2
DEC_BATCH = 128
DEC_SEQ = 1
PAST_LEN = 16384
PAGE_SIZE = 128

N_A_LAYERS = (DEPTH + 1) // 2
N_C_LAYERS = DEPTH // 2
MIX_WIDTH = D_MODEL
RET_WIDTH = MIX_WIDTH // 2
RET_HEADS = 4
RET_DK = RET_WIDTH // RET_HEADS
RET_DV = RET_WIDTH // RET_HEADS
RET_CHUNK = 128
ROPE_BASE = 10000.0
LRU_WIDTH = MIX_WIDTH - RET_WIDTH
LRU_BLOCKS = 8
LRU_BW = LRU_WIDTH // LRU_BLOCKS
LRU_C = 8.0
CONV_W = 4
IN_A = 4 * RET_WIDTH + 2 * LRU_WIDTH
SG_CHUNK = 128
SG_HALF = D_MODEL
SG_GROUPS = 8
SG_GW = SG_HALF // SG_GROUPS
D_FF = ((8 * D_MODEL + 3 * 256 - 1) // (3 * 256)) * 256
EPS = 1e-6

kernel_name = "hybrid_retention_rglru_sgu_decoder_step"


def rmsnorm(x, g):
    xf = x.astype(jnp.float32)
    y = xf * lax.rsqrt(jnp.mean(xf * xf, axis=-1, keepdims=True) + EPS)
    return (y * g.astype(jnp.float32)).astype(x.dtype)


def layernorm(x, g, b):
    xf = x.astype(jnp.float32)
    xc = xf - jnp.mean(xf, axis=-1, keepdims=True)
    y = xc * lax.rsqrt(jnp.mean(xc * xc, axis=-1, keepdims=True) + EPS)
    return (y * g.astype(jnp.float32) + b.astype(jnp.float32)).astype(x.dtype)


def group_norm_heads(o, g):
    B, L, H, DV = o.shape
    of = o.astype(jnp.float32)
    xc = of - jnp.mean(of, axis=-1, keepdims=True)
    y = xc * lax.rsqrt(jnp.mean(xc * xc, axis=-1, keepdims=True) + EPS)
    return (y.reshape(B, L, H * DV) * g.astype(jnp.float32)).astype(o.dtype)


def rotary(x, pos):
    half = x.shape[-1] // 2
    inv = ROPE_BASE ** (-jnp.arange(half, dtype=jnp.float32) / half)
    ang = pos[:, None] * inv[None, :]
    cos = jnp.cos(ang)[None, :, None, :]
    sin = jnp.sin(ang)[None, :, None, :]
    xf = x.astype(jnp.float32)
    x1, x2 = xf[..., :half], xf[..., half:]
    return jnp.concatenate([x1 * cos - x2 * sin, x2 * cos + x1 * sin], axis=-1).astype(x.dtype)


def retention(q, k, v, s0):
    B, L, H, DK = q.shape
    DV = v.shape[-1]
    dt = q.dtype
    C = RET_CHUNK if L % RET_CHUNK == 0 else L
    n = L // C
    lg = jnp.log1p(-jnp.exp2(-5.0 - jnp.arange(H, dtype=jnp.float32)))
    q = q.reshape(B, n, C, H, DK) * (DK ** -0.5)
    k = k.reshape(B, n, C, H, DK)
    v = v.reshape(B, n, C, H, DV)
    idx = jnp.arange(C, dtype=jnp.float32)
    diff = idx[:, None] - idx[None, :]
    causal = diff >= 0
    dmat = jnp.where(causal[None], jnp.exp(jnp.where(causal, diff, 0.0)[None] * lg[:, None, None]), 0.0)
    scores = jnp.einsum('bnihd,bnjhd->bnhij', q, k) * dmat.astype(dt)
    inner = jnp.einsum('bnhij,bnjhe->bnihe', scores, v)
    zeta = jnp.exp((C - 1.0 - idx)[None, :] * lg[:, None]).astype(dt)
    xi = jnp.exp((idx + 1.0)[None, :] * lg[:, None]).astype(dt)
    u = jnp.einsum('bnjhd,bnjhe,hj->nbhde', k, v, zeta)
    g_chunk = jnp.exp(C * lg).astype(s0.dtype)[:, None, None]

    def step(s, u_c):
        return (g_chunk * s + u_c).astype(s.dtype), s

    s_fin, s_start = lax.scan(step, s0, u)
    cross = jnp.einsum('bnihd,nbhde,hi->bnihe', q, s_start.astype(dt), xi)
    return (inner + cross).reshape(B, L, H, DV), s_fin


def causal_conv(x, buf, w, b):
    L = x.shape[1]
    xp = jnp.concatenate([buf.astype(x.dtype), x], axis=1)
    y = b
    for j in range(CONV_W):
        y = y + xp[:, j:j + L] * w[j]
    return y, xp[:, -(CONV_W - 1):]


def rg_lru(x, h0, wa, ba, wx, bx, lam):
    B, L, W = x.shape
    xb = x.reshape(B, L, LRU_BLOCKS, LRU_BW)
    r = jax.nn.sigmoid(jnp.einsum('blnc,ncd->blnd', xb, wa).reshape(B, L, W) + ba).astype(jnp.float32)
    i = jax.nn.sigmoid(jnp.einsum('blnc,ncd->blnd', xb, wx).reshape(B, L, W) + bx).astype(jnp.float32)
    log_a = -LRU_C * r * jax.nn.softplus(-lam.astype(jnp.float32))
    a = jnp.exp(log_a)
    mult = jnp.sqrt(-jnp.expm1(2.0 * log_a))
    b_in = x.astype(jnp.float32) * i * mult
    b_in = b_in.at[:, 0].add(a[:, 0] * h0.astype(jnp.float32))

    def combine(left, right):
        a1, b1 = left
        a2, b2 = right
        return a1 * a2, a2 * b1 + b2

    _, h = lax.associative_scan(combine, (a, b_in), axis=1)
    h = h.astype(x.dtype)
    return h, h[:, -1]


def mixer_ab(h, pos, s_ret, h_lru, conv_buf, w_in, gn_g, conv_w, conv_b, wa, ba, wx, bx, lam, w_out):
    B, L, _ = h.shape
    z = h @ w_in
    q, k, v, g, xb, gb = jnp.split(
        z, [RET_WIDTH, 2 * RET_WIDTH, 3 * RET_WIDTH, 4 * RET_WIDTH, 4 * RET_WIDTH + LRU_WIDTH], axis=-1)
    q = rotary(q.reshape(B, L, RET_HEADS, RET_DK), pos)
    k = rotary(k.reshape(B, L, RET_HEADS, RET_DK), pos)
    o, s_new = retention(q, k, v.reshape(B, L, RET_HEADS, RET_DV), s_ret)
    y_a = group_norm_heads(o, gn_g) * jax.nn.silu(g)
    xc, conv_new = causal_conv(xb, conv_buf, conv_w, conv_b)
    hl, h_new = rg_lru(xc, h_lru, wa, ba, wx, bx, lam)
    y_b = hl * jax.nn.gelu(gb)
    out = jnp.concatenate([y_a, y_b], axis=-1) @ w_out
    return out, s_new, h_new, conv_new


def mixer_c(h, w_in, vg, vb, ws, bs, w_out):
    B, L, _ = h.shape
    z = jax.nn.gelu(h @ w_in)
    u, v = jnp.split(z, [SG_HALF], axis=-1)
    v = layernorm(v, vg, vb)
    C = SG_CHUNK if L >= SG_CHUNK else L
    n = L // C
    mask = jnp.tril(jnp.ones((C, C), dtype=ws.dtype))
    wm = ws[:, :C, :C] * mask
    vr = v.reshape(B, n, C, SG_GROUPS, SG_GW)
    sv = jnp.einsum('gij,bnjgd->bnigd', wm, vr) + bs[:, :C].T[:, :, None]
    out = (u * sv.reshape(B, L, SG_HALF)) @ w_out
    return out, v


def swiglu(h, wg, wu, wd):
    return (jax.nn.silu(h @ wg) * (h @ wu)) @ wd


def trunk(x, pos, s_ret, h_lru, conv_buf, norm1, norm2, norm_f, w_in_a, ret_gn, conv_w, conv_b,
          lru_wa, lru_ba, lru_wx, lru_bx, lru_lambda, w_out_a, w_in_c, sg_norm_g, sg_norm_b,
          sg_ws, sg_bs, w_out_c, ffn_wg, ffn_wu, ffn_wd):
    s_out, h_out, c_out, v_out = [], [], [], []
    for layer in range(DEPTH):
        hn = rmsnorm(x, norm1[layer])
        if layer % 2 == 0:
            a = layer // 2
            mix, s_n, h_n, c_n = mixer_ab(hn, pos, s_ret[a], h_lru[a], conv_buf[a], w_in_a[a], ret_gn[a],
                                          conv_w[a], conv_b[a], lru_wa[a], lru_ba[a], lru_wx[a],
                                          lru_bx[a], lru_lambda[a], w_out_a[a])
            s_out.append(s_n)
            h_out.append(h_n)
            c_out.append(c_n)
        else:
            c = layer // 2
            mix, v_n = mixer_c(hn, w_in_c[c], sg_norm_g[c], sg_norm_b[c], sg_ws[c], sg_bs[c], w_out_c[c])
            v_out.append(v_n)
        x = x + mix
        x = x + swiglu(rmsnorm(x, norm2[layer]), ffn_wg[layer], ffn_wu[layer], ffn_wd[layer])
    return rmsnorm(x, norm_f), jnp.stack(s_out), jnp.stack(h_out), jnp.stack(c_out), v_out


def setup_inputs(seed: int = 0) -> dict:
    key = jax.random.key(seed)
    ks = jax.random.split(key, 32)
    f32 = jnp.float32
    nrm = lambda k, shape, scale: jax.random.normal(k, shape, f32) * scale
    a0 = jax.random.uniform(ks[0], (N_A_LAYERS, LRU_WIDTH), f32, 0.9, 0.999)
    s = a0 ** (1.0 / LRU_C)
    lru_lambda = jnp.log(s) - jnp.log1p(-s)
    return {
        "x_prompt": nrm(ks[1], (BATCH, SEQ, D_MODEL), 1.0),
        "x_sample": nrm(ks[2], (DEC_BATCH, DEC_SEQ, D_MODEL), 1.0),
        "state_ret": nrm(ks[3], (N_A_LAYERS, DEC_BATCH, RET_HEADS, RET_DK, RET_DV), 1.0),
        "state_lru": nrm(ks[4], (N_A_LAYERS, DEC_BATCH, LRU_WIDTH), 0.5),
        "state_conv": nrm(ks[5], (N_A_LAYERS, DEC_BATCH, CONV_W - 1, LRU_WIDTH), 1.0),
        "norm1": 1.0 + nrm(ks[6], (DEPTH, D_MODEL), 0.02),
        "norm2": 1.0 + nrm(ks[7], (DEPTH, D_MODEL), 0.02),
        "norm_f": 1.0 + nrm(ks[8], (D_MODEL,), 0.02),
        "w_in_a": nrm(ks[9], (N_A_LAYERS, D_MODEL, IN_A), D_MODEL ** -0.5),
        "ret_gn": 1.0 + nrm(ks[10], (N_A_LAYERS, RET_WIDTH), 0.02),
        "conv_w": nrm(ks[11], (N_A_LAYERS, CONV_W, LRU_WIDTH), CONV_W ** -0.5),
        "conv_b": nrm(ks[12], (N_A_LAYERS, LRU_WIDTH), 0.02),
        "lru_wa": nrm(ks[13], (N_A_LAYERS, LRU_BLOCKS, LRU_BW, LRU_BW), LRU_BW ** -0.5),
        "lru_ba": nrm(ks[14], (N_A_LAYERS, LRU_WIDTH), 0.02),
        "lru_wx": nrm(ks[15], (N_A_LAYERS, LRU_BLOCKS, LRU_BW, LRU_BW), LRU_BW ** -0.5),
        "lru_bx": nrm(ks[16], (N_A_LAYERS, LRU_WIDTH), 0.02),
        "lru_lambda": lru_lambda,
        "w_out_a": nrm(ks[17], (N_A_LAYERS, MIX_WIDTH, D_MODEL), MIX_WIDTH ** -0.5),
        "w_in_c": nrm(ks[18], (N_C_LAYERS, D_MODEL, 2 * SG_HALF), D_MODEL ** -0.5),
        "sg_norm_g": 1.0 + nrm(ks[19], (N_C_LAYERS, SG_HALF), 0.02),
        "sg_norm_b": nrm(ks[20], (N_C_LAYERS, SG_HALF), 0.02),
        "sg_ws": nrm(ks[21], (N_C_LAYERS, SG_GROUPS, SG_CHUNK, SG_CHUNK), SG_CHUNK ** -0.5),
        "sg_bs": 1.0 + nrm(ks[22], (N_C_LAYERS, SG_GROUPS, SG_CHUNK), 0.01),
        "w_out_c": nrm(ks[23], (N_C_LAYERS, SG_HALF, D_MODEL), SG_HALF ** -0.5),
        "ffn_wg": nrm(ks[24], (DEPTH, D_MODEL, D_FF), D_MODEL ** -0.5),
        "ffn_wu": nrm(ks[25], (DEPTH, D_MODEL, D_FF), D_MODEL ** -0.5),
        "ffn_wd": nrm(ks[26], (DEPTH, D_FF, D_MODEL), D_FF ** -0.5),
    }


def reference(x_prompt, x_sample, state_ret, state_lru, state_conv, norm1, norm2, norm_f, w_in_a, ret_gn,
              conv_w, conv_b, lru_wa, lru_ba, lru_wx, lru_bx, lru_lambda, w_out_a, w_in_c, sg_norm_g,
              sg_norm_b, sg_ws, sg_bs, w_out_c, ffn_wg, ffn_wu, ffn_wd):
    bp, lp, _ = x_prompt.shape
    ls = x_sample.shape[1]
    dt = x_prompt.dtype
    pos_p = jnp.arange(lp, dtype=jnp.float32)
    pos_s = PAST_LEN + jnp.arange(ls, dtype=jnp.float32)
    ret0 = jnp.zeros((N_A_LAYERS, bp, RET_HEADS, RET_DK, RET_DV), dt)
    lru0 = jnp.zeros((N_A_LAYERS, bp, LRU_WIDTH), dt)
    conv0 = jnp.zeros((N_A_LAYERS, bp, CONV_W - 1, LRU_WIDTH), dt)
    weights = (norm1, norm2, norm_f, w_in_a, ret_gn, conv_w, conv_b, lru_wa, lru_ba, lru_wx, lru_bx,
               lru_lambda, w_out_a, w_in_c, sg_norm_g, sg_norm_b, sg_ws, sg_bs, w_out_c, ffn_wg, ffn_wu, ffn_wd)
    y_prompt, ret_prompt, lru_prompt, conv_prompt, _ = trunk(x_prompt, pos_p, ret0, lru0, conv0, *weights)
    y_sample, ret_sample, lru_sample, conv_sample, v_rows = trunk(
        x_sample, pos_s, state_ret, state_lru, state_conv, *weights)
    sgu_v_sample = jnp.stack(v_rows)
    return (y_prompt, y_sample, ret_prompt, ret_sample, lru_prompt, lru_sample, conv_prompt, conv_sample,
            sgu_v_sample)
```

```python
import functools
import math

import jax
import jax.numpy as jnp
from jax import lax
from jax.experimental import pallas as pl
from jax.experimental.pallas import tpu as pltpu

F32 = jnp.float32
BF16 = jnp.bfloat16

D_MODEL = 1024
RET_WIDTH = 512
RET_HEADS = 4
RET_DK = 128
RET_CHUNK = 128
ROPE_BASE = 10000.0
LRU_WIDTH = 512
LRU_BLOCKS = 8
LRU_BW = 64
LRU_C = 8.0
CONV_W = 4
SG_CHUNK = 128
SG_HALF = 1024
SG_GROUPS = 8
SG_GW = 128
D_FF = 2816
EPS = 1e-6
PAST_LEN = 16384

V7X_VMEM_BYTES = 64 * 1024 * 1024
SUBLANES = 8
LANES = 128

Q_SCALE = RET_DK ** -0.5
GELU_C = math.sqrt(2.0 / math.pi)


def _vmem_limit(estimate_bytes):
    return int(min(estimate_bytes * 3 // 2 + (8 << 20), V7X_VMEM_BYTES - (6 << 20)))


def _resident(shape):
    nd = len(shape)
    return pl.BlockSpec(shape, lambda *_: (0,) * nd, pipeline_mode=pl.Buffered(1))


def _dot(a, b):
    return jnp.dot(a, b, preferred_element_type=F32)


def _rms(x, g):
    ms = jnp.mean(x * x, axis=-1, keepdims=True)
    return x * lax.rsqrt(ms + EPS) * g


def _gelu(x):
    return x * (0.5 * (1.0 + jnp.tanh(GELU_C * (x + 0.044715 * (x * x * x)))))


def _silu(x):
    return x * jax.nn.sigmoid(x)


def _softplus(x):
    return jnp.maximum(x, 0.0) + jnp.log1p(jnp.exp(-jnp.abs(x)))


def _group_norm(o):
    mu = jnp.mean(o, axis=-1, keepdims=True)
    oc = o - mu
    var = jnp.mean(oc * oc, axis=-1, keepdims=True)
    return oc * lax.rsqrt(var + EPS)


def _rotary(x, cosf, sinf):
    return x * cosf + pltpu.roll(x, RET_DK // 2, 1) * sinf


def _lru_coeffs(xc, wabd_ref, wxbd_ref, ba, bx, lam):
    xcb = xc.astype(BF16)
    ra, rx = [], []
    for g in range(LRU_WIDTH // LANES):
        cols = slice(g * LANES, (g + 1) * LANES)
        ra.append(_dot(xcb[:, cols], wabd_ref[g]))
        rx.append(_dot(xcb[:, cols], wxbd_ref[g]))
    r = jax.nn.sigmoid(jnp.concatenate(ra, axis=1) + ba)
    i = jax.nn.sigmoid(jnp.concatenate(rx, axis=1) + bx)
    log_a = (-LRU_C * r) * _softplus(-lam)
    a = jnp.exp(log_a)
    mult = jnp.sqrt(-jnp.tanh(log_a) * (a * a + 1.0))
    return a, xc * i * mult


def _ffn_kernel(x_ref, n2_ref, wg_ref, wu_ref, wd_ref, nf_ref, o_ref, *, final_norm):
    x = x_ref[...]
    h = _rms(x, n2_ref[...]).astype(BF16)
    act = (_silu(_dot(h, wg_ref[...])) * _dot(h, wu_ref[...])).astype(BF16)
    y = x + _dot(act, wd_ref[...])
    if final_norm:
        y = _rms(y, nf_ref[...])
    o_ref[...] = y


def _ffn(x, n2, wg, wu, wd, nf, *, final_norm, tm):
    t = x.shape[0]
    est = 3 * D_MODEL * D_FF * 2 + 4 * tm * D_MODEL * 4 + 3 * tm * D_FF * 4
    return pl.pallas_call(
        functools.partial(_ffn_kernel, final_norm=final_norm),
        grid=(t // tm,),
        in_specs=[
            pl.BlockSpec((tm, D_MODEL), lambda i: (i, 0)),
            _resident((1, D_MODEL)),
            _resident((D_MODEL, D_FF)),
            _resident((D_MODEL, D_FF)),
            _resident((D_FF, D_MODEL)),
            _resident((1, D_MODEL)),
        ],
        out_specs=pl.BlockSpec((tm, D_MODEL), lambda i: (i, 0)),
        out_shape=jax.ShapeDtypeStruct((t, D_MODEL), F32),
        compiler_params=pltpu.CompilerParams(
            dimension_semantics=("arbitrary",), vmem_limit_bytes=_vmem_limit(est)),
        name="ffn_final" if final_norm else "ffn",
    )(x, n2, wg, wu, wd, nf)


def _sgu_in(x, n1, win_ref, vg, vb):
    hn = _rms(x, n1).astype(BF16)
    u = _gelu(_dot(hn, win_ref[:, :SG_HALF]))
    zv = _gelu(_dot(hn, win_ref[:, SG_HALF:]))
    mu = jnp.mean(zv, axis=-1, keepdims=True)
    vc = zv - mu
    var = jnp.mean(vc * vc, axis=-1, keepdims=True)
    return u, vc * lax.rsqrt(var + EPS) * vg + vb


def _mixc_prompt_kernel(x_ref, n1_ref, win_ref, vg_ref, vb_ref, ws_ref, bsb_ref, wout_ref,
                        o_ref, u_s, v_s, gated_s, *, tm):
    x = x_ref[...]
    u, vn = _sgu_in(x, n1_ref[...], win_ref, vg_ref[...], vb_ref[...])
    u_s[...] = u
    v_s[...] = vn.astype(BF16)
    ri = lax.broadcasted_iota(jnp.int32, (SG_CHUNK, SG_CHUNK), 0)
    ci = lax.broadcasted_iota(jnp.int32, (SG_CHUNK, SG_CHUNK), 1)
    for g in range(SG_GROUPS):
        cols = slice(g * SG_GW, (g + 1) * SG_GW)
        wm = jnp.where(ri >= ci, ws_ref[g], 0.0).astype(BF16)
        for c in range(tm // SG_CHUNK):
            rows = slice(c * SG_CHUNK, (c + 1) * SG_CHUNK)
            sv = _dot(wm, v_s[rows, cols]) + bsb_ref[g]
            gated_s[rows, cols] = (u_s[rows, cols] * sv).astype(BF16)
    o_ref[...] = x + _dot(gated_s[...], wout_ref[...])


def _mixc_prompt(x, n1, win, vg, vb, ws, bsb, wout, *, tm):
    t = x.shape[0]
    est = 3 * D_MODEL * SG_HALF * 2 + 4 * tm * D_MODEL * 4 + 4 * tm * 2 * SG_HALF * 4
    return pl.pallas_call(
        functools.partial(_mixc_prompt_kernel, tm=tm),
        grid=(t // tm,),
        in_specs=[
            pl.BlockSpec((tm, D_MODEL), lambda i: (i, 0)),
            _resident((1, D_MODEL)),
            _resident((D_MODEL, 2 * SG_HALF)),
            _resident((1, SG_HALF)),
            _resident((1, SG_HALF)),
            _resident((SG_GROUPS, SG_CHUNK, SG_CHUNK)),
            _resident((SG_GROUPS, SG_CHUNK, SG_GW)),
            _resident((SG_HALF, D_MODEL)),
        ],
        out_specs=pl.BlockSpec((tm, D_MODEL), lambda i: (i, 0)),
        out_shape=jax.ShapeDtypeStruct((t, D_MODEL), F32),
        scratch_shapes=[
            pltpu.VMEM((tm, SG_HALF), F32),
            pltpu.VMEM((tm, SG_HALF), BF16),
            pltpu.VMEM((tm, SG_HALF), BF16),
        ],
        compiler_params=pltpu.CompilerParams(
            dimension_semantics=("arbitrary",), vmem_limit_bytes=_vmem_limit(est)),
        name="mixc_prompt",
    )(x, n1, win, vg, vb, ws, bsb, wout)


def _mixc_sample_kernel(x_ref, n1_ref, win_ref, vg_ref, vb_ref, scale_ref, bias_ref, wout_ref,
                        o_ref, v_ref):
    x = x_ref[...]
    u, vn = _sgu_in(x, n1_ref[...], win_ref, vg_ref[...], vb_ref[...])
    v_ref[...] = vn
    sv = vn * scale_ref[...] + bias_ref[...]
    o_ref[...] = x + _dot((u * sv).astype(BF16), wout_ref[...])


def _mixc_sample(x, n1, win, vg, vb, scale, bias, wout):
    t = x.shape[0]
    est = 3 * D_MODEL * SG_HALF * 2 + 8 * t * 2 * SG_HALF * 4
    return pl.pallas_call(
        _mixc_sample_kernel,
        grid=(1,),
        in_specs=[
            _resident((t, D_MODEL)),
            _resident((1, D_MODEL)),
            _resident((D_MODEL, 2 * SG_HALF)),
            _resident((1, SG_HALF)),
            _resident((1, SG_HALF)),
            _resident((1, SG_HALF)),
            _resident((1, SG_HALF)),
            _resident((SG_HALF, D_MODEL)),
        ],
        out_specs=[
            pl.BlockSpec((t, D_MODEL), lambda i: (0, 0)),
            pl.BlockSpec((t, SG_HALF), lambda i: (0, 0)),
        ],
        out_shape=[
            jax.ShapeDtypeStruct((t, D_MODEL), F32),
            jax.ShapeDtypeStruct((t, SG_HALF), F32),
        ],
        compiler_params=pltpu.CompilerParams(
            dimension_semantics=("arbitrary",), vmem_limit_bytes=_vmem_limit(est)),
        name="mixc_sample",
    )(x, n1, win, vg, vb, scale, bias, wout)


def _lru_scan(a, b, h0, h_s, *, tm):
    w = a.shape[1]
    groups = tm // SUBLANES
    a3 = a.reshape(groups, SUBLANES, w)
    b3 = b.reshape(groups, SUBLANES, w)
    sub = lax.broadcasted_iota(jnp.int32, (groups, SUBLANES, w), 1)
    for s in (1, 2, 4):
        a_sh = pltpu.roll(a3, s, 1)
        b_sh = pltpu.roll(b3, s, 1)
        keep = sub >= s
        b3 = jnp.where(keep, a3 * b_sh + b3, b3)
        a3 = jnp.where(keep, a3 * a_sh, a3)
    hprev = jnp.broadcast_to(h0, (SUBLANES, w))
    for r in range(groups):
        hr = a3[r] * hprev + b3[r]
        h_s[r * SUBLANES:(r + 1) * SUBLANES, :] = hr
        hprev = jnp.broadcast_to(hr[SUBLANES - 1:SUBLANES, :], (SUBLANES, w))
    return hprev[0:1, :]


def _mixa_prompt_kernel(gdec_ref, x_ref, n1_ref, win_ref, cos_ref, sin_ref, dmat_ref, zeta_ref,
                        xi_ref, s0_ref, h0_ref, cb0_ref, gn_ref, cw_ref, cbias_ref, wabd_ref,
                        wxbd_ref, ba_ref, bx_ref, lam_ref, wout_ref,
                        o_ref, s_ref, hT_ref, cT_ref,
                        q_s, k_s, kz_s, v_s, sg_s, gg_s, xp_s, h_s, ymix_s, sb_s, *, tm):
    nchunk = tm // RET_CHUNK

    @pl.when(pl.program_id(1) == 0)
    def _():
        s_ref[...] = s0_ref[...]
        hT_ref[...] = h0_ref[...]
        cT_ref[...] = cb0_ref[...]

    x = x_ref[...]
    hn = _rms(x, n1_ref[...]).astype(BF16)
    cosf = cos_ref[...]
    sinf = sin_ref[...]
    w = RET_WIDTH
    zq = _dot(hn, win_ref[:, 0 * w:1 * w])
    zk = _dot(hn, win_ref[:, 1 * w:2 * w])
    for h in range(RET_HEADS):
        cols = slice(h * RET_DK, (h + 1) * RET_DK)
        q_s[:, cols] = (_rotary(zq[:, cols], cosf, sinf) * Q_SCALE).astype(BF16)
        kr = _rotary(zk[:, cols], cosf, sinf)
        k_s[:, cols] = kr.astype(BF16)
        for c in range(nchunk):
            rows = slice(c * RET_CHUNK, (c + 1) * RET_CHUNK)
            kz_s[rows, cols] = (kr[rows, :] * zeta_ref[h]).astype(BF16)
    v_s[...] = _dot(hn, win_ref[:, 2 * w:3 * w]).astype(BF16)
    sg_s[...] = _silu(_dot(hn, win_ref[:, 3 * w:4 * w]))
    xb = _dot(hn, win_ref[:, 4 * w:5 * w])
    gg_s[...] = _gelu(_dot(hn, win_ref[:, 5 * w:6 * w]))

    pad = SUBLANES
    xp_s[pad - (CONV_W - 1):pad, :] = cT_ref[0]
    xp_s[pad:pad + tm, :] = xb
    cT_ref[0] = xp_s[pad + tm - (CONV_W - 1):pad + tm, :]
    xc = cbias_ref[...]
    for j in range(CONV_W):
        off = pad - (CONV_W - 1) + j
        xc = xc + xp_s[off:off + tm, :] * cw_ref[j:j + 1, :]

    a, b_in = _lru_coeffs(xc, wabd_ref, wxbd_ref, ba_ref[...], bx_ref[...], lam_ref[...])
    hT_ref[0] = _lru_scan(a, b_in, hT_ref[0], h_s, tm=tm)
    ymix_s[:, RET_WIDTH:] = (h_s[...] * gg_s[...]).astype(BF16)

    for h in range(RET_HEADS):
        sb_s[h] = s_ref[0, h].astype(BF16)
    for c in range(nchunk):
        rows = slice(c * RET_CHUNK, (c + 1) * RET_CHUNK)
        for h in range(RET_HEADS):
            cols = slice(h * RET_DK, (h + 1) * RET_DK)
            qs = q_s[rows, cols]
            vb = v_s[rows, cols]
            sc = lax.dot_general(qs, k_s[rows, cols], (((1,), (1,)), ((), ())),
                                 preferred_element_type=F32) * dmat_ref[h]
            o = _dot(sc.astype(BF16), vb) + _dot(qs, sb_s[h]) * xi_ref[h]
            ya = _group_norm(o) * gn_ref[:, cols] * sg_s[rows, cols]
            ymix_s[rows, cols] = ya.astype(BF16)
            u = lax.dot_general(kz_s[rows, cols], vb, (((0,), (0,)), ((), ())),
                                preferred_element_type=F32)
            s_new = gdec_ref[h] * s_ref[0, h] + u
            s_ref[0, h] = s_new
            sb_s[h] = s_new.astype(BF16)

    o_ref[...] = x + _dot(ymix_s[...], wout_ref[...])


def _mixa_prompt(x, gdec, n1, win, cosf, sinf, dmat, zeta, xi, s0, h0, cb0, gn, cw, cbias,
                 wabd, wxbd, ba, bx, lam, wout, *, batch, seq, tm):
    nt = seq // tm
    in_w = 6 * RET_WIDTH
    est = (D_MODEL * in_w * 2 + D_MODEL * D_MODEL * 2 + 4 * tm * D_MODEL * 4
           + 10 * tm * RET_WIDTH * 4 + 3 * tm * in_w * 4)
    row_blk = lambda b, t: (b * nt + t, 0)
    per_b4 = lambda b, t: (b, 0, 0, 0)
    per_b3 = lambda b, t: (b, 0, 0)
    hd = (RET_HEADS, RET_CHUNK, RET_CHUNK)
    return pl.pallas_call(
        functools.partial(_mixa_prompt_kernel, tm=tm),
        grid=(batch, nt),
        in_specs=[
            pl.BlockSpec(memory_space=pltpu.SMEM),
            pl.BlockSpec((tm, D_MODEL), row_blk),
            _resident((1, D_MODEL)),
            _resident((D_MODEL, in_w)),
            pl.BlockSpec((tm, RET_DK), lambda b, t: (t, 0)),
            pl.BlockSpec((tm, RET_DK), lambda b, t: (t, 0)),
            _resident(hd),
            _resident(hd),
            _resident(hd),
            pl.BlockSpec((1, RET_HEADS, RET_DK, RET_DK), per_b4),
            pl.BlockSpec((1, 1, LRU_WIDTH), per_b3),
            pl.BlockSpec((1, CONV_W - 1, LRU_WIDTH), per_b3),
            _resident((1, RET_WIDTH)),
            _resident((CONV_W, LRU_WIDTH)),
            _resident((1, LRU_WIDTH)),
            _resident((LRU_WIDTH // LANES, LANES, LANES)),
            _resident((LRU_WIDTH // LANES, LANES, LANES)),
            _resident((1, LRU_WIDTH)),
            _resident((1, LRU_WIDTH)),
            _resident((1, LRU_WIDTH)),
            _resident((D_MODEL, D_MODEL)),
        ],
        out_specs=[
            pl.BlockSpec((tm, D_MODEL), row_blk),
            pl.BlockSpec((1, RET_HEADS, RET_DK, RET_DK), per_b4),
            pl.BlockSpec((1, 1, LRU_WIDTH), per_b3),
            pl.BlockSpec((1, CONV_W - 1, LRU_WIDTH), per_b3),
        ],
        out_shape=[
            jax.ShapeDtypeStruct((batch * seq, D_MODEL), F32),
            jax.ShapeDtypeStruct((batch, RET_HEADS, RET_DK, RET_DK), F32),
            jax.ShapeDtypeStruct((batch, 1, LRU_WIDTH), F32),
            jax.ShapeDtypeStruct((batch, CONV_W - 1, LRU_WIDTH), F32),
        ],
        scratch_shapes=[
            pltpu.VMEM((tm, RET_WIDTH), BF16),
            pltpu.VMEM((tm, RET_WIDTH), BF16),
            pltpu.VMEM((tm, RET_WIDTH), BF16),
            pltpu.VMEM((tm, RET_WIDTH), BF16),
            pltpu.VMEM((tm, RET_WIDTH), F32),
            pltpu.VMEM((tm, LRU_WIDTH), F32),
            pltpu.VMEM((tm + 2 * SUBLANES, LRU_WIDTH), F32),
            pltpu.VMEM((tm, LRU_WIDTH), F32),
            pltpu.VMEM((tm, D_MODEL), BF16),
            pltpu.VMEM((RET_HEADS, RET_DK, RET_DK), BF16),
        ],
        compiler_params=pltpu.CompilerParams(
            dimension_semantics=("arbitrary", "arbitrary"), vmem_limit_bytes=_vmem_limit(est)),
        name="mixa_prompt",
    )(gdec, x, n1, win, cosf, sinf, dmat, zeta, xi, s0, h0, cb0, gn, cw, cbias, wabd, wxbd,
      ba, bx, lam, wout)


def _mixa_sample_kernel(gdec_ref, x_ref, n1_ref, win_ref, wkt_ref, cos_ref, sin_ref, cost_ref,
                        sint_ref, s0_ref, h0_ref, cb0_ref, gn_ref, cw_ref, cbias_ref, wabd_ref,
                        wxbd_ref, ba_ref, bx_ref, lam_ref, wout_ref,
                        o_ref, s_ref, hT_ref, cT_ref, ymix_s, *, bt):
    x = x_ref[...]
    hn = _rms(x, n1_ref[...]).astype(BF16)
    cosf = cos_ref[...]
    sinf = sin_ref[...]
    w = RET_WIDTH
    zq = _dot(hn, win_ref[:, 0 * w:1 * w])
    zk = _dot(hn, win_ref[:, 1 * w:2 * w])
    zv = _dot(hn, win_ref[:, 2 * w:3 * w])
    sg = _silu(_dot(hn, win_ref[:, 3 * w:4 * w]))
    xb = _dot(hn, win_ref[:, 4 * w:5 * w])
    gg = _gelu(_dot(hn, win_ref[:, 5 * w:6 * w]))
    zkt = lax.dot_general(wkt_ref[...], hn, (((1,), (1,)), ((), ())), preferred_element_type=F32)

    rowi = lax.broadcasted_iota(jnp.int32, (bt, RET_DK), 0)
    half = RET_DK // 2
    for h in range(RET_HEADS):
        cols = slice(h * RET_DK, (h + 1) * RET_DK)
        qs = (_rotary(zq[:, cols], cosf, sinf) * Q_SCALE).astype(BF16)
        kb = _rotary(zk[:, cols], cosf, sinf).astype(BF16)
        vh = zv[:, cols]
        kt = zkt[cols, :]
        kt_rolled = jnp.concatenate([kt[half:, :], kt[:half, :]], axis=0)
        ktb = (kt * cost_ref[...] + kt_rolled * sint_ref[...]).astype(BF16)
        qk = jnp.sum(qs.astype(F32) * kb.astype(F32), axis=-1, keepdims=True)
        cross = jnp.zeros((bt, RET_DK), F32)
        for b in range(bt):
            s_old = s0_ref[b, h]
            cr = _dot(qs, s_old.astype(BF16))
            cross = cross + jnp.where(rowi == b, cr, 0.0)
            u = _dot(ktb, jnp.where(rowi == b, vh, 0.0).astype(BF16))
            s_ref[b, h] = gdec_ref[h] * s_old + u
        o = qk * vh + cross * gdec_ref[h]
        ymix_s[:, cols] = (_group_norm(o) * gn_ref[:, cols] * sg[:, cols]).astype(BF16)

    xc = cbias_ref[...]
    for j in range(CONV_W - 1):
        xc = xc + cb0_ref[j] * cw_ref[j:j + 1, :]
    xc = xc + xb * cw_ref[CONV_W - 1:CONV_W, :]
    for j in range(CONV_W - 2):
        cT_ref[j] = cb0_ref[j + 1]
    cT_ref[CONV_W - 2] = xb
    a, b_in = _lru_coeffs(xc, wabd_ref, wxbd_ref, ba_ref[...], bx_ref[...], lam_ref[...])
    hnew = a * h0_ref[...] + b_in
    hT_ref[...] = hnew
    ymix_s[:, RET_WIDTH:] = (hnew * gg).astype(BF16)
    o_ref[...] = x + _dot(ymix_s[...], wout_ref[...])


def _mixa_sample(x, gdec, n1, win, wkt, cosf, sinf, cost, sint, s0, h0, cb0, gn, cw, cbias,
                 wabd, wxbd, ba, bx, lam, wout, *, bt):
    batch = x.shape[0]
    in_w = 6 * RET_WIDTH
    est = (D_MODEL * (in_w + RET_WIDTH) * 2 + D_MODEL * D_MODEL * 2
           + 4 * bt * RET_HEADS * RET_DK * RET_DK * 4 + 16 * bt * in_w * 4)
    rows = lambda i: (i, 0)
    return pl.pallas_call(
        functools.partial(_mixa_sample_kernel, bt=bt),
        grid=(batch // bt,),
        in_specs=[
            pl.BlockSpec(memory_space=pltpu.SMEM),
            pl.BlockSpec((bt, D_MODEL), rows),
            _resident((1, D_MODEL)),
            _resident((D_MODEL, in_w)),
            _resident((RET_WIDTH, D_MODEL)),
            _resident((1, RET_DK)),
            _resident((1, RET_DK)),
            _resident((RET_DK, bt)),
            _resident((RET_DK, bt)),
            pl.BlockSpec((bt, RET_HEADS, RET_DK, RET_DK), lambda i: (i, 0, 0, 0)),
            pl.BlockSpec((bt, LRU_WIDTH), rows),
            pl.BlockSpec((CONV_W - 1, bt, LRU_WIDTH), lambda i: (0, i, 0)),
            _resident((1, RET_WIDTH)),
            _resident((CONV_W, LRU_WIDTH)),
            _resident((1, LRU_WIDTH)),
            _resident((LRU_WIDTH // LANES, LANES, LANES)),
            _resident((LRU_WIDTH // LANES, LANES, LANES)),
            _resident((1, LRU_WIDTH)),
            _resident((1, LRU_WIDTH)),
            _resident((1, LRU_WIDTH)),
            _resident((D_MODEL, D_MODEL)),
        ],
        out_specs=[
            pl.BlockSpec((bt, D_MODEL), rows),
            pl.BlockSpec((bt, RET_HEADS, RET_DK, RET_DK), lambda i: (i, 0, 0, 0)),
            pl.BlockSpec((bt, LRU_WIDTH), rows),
            pl.BlockSpec((CONV_W - 1, bt, LRU_WIDTH), lambda i: (0, i, 0)),
        ],
        out_shape=[
            jax.ShapeDtypeStruct((batch, D_MODEL), F32),
            jax.ShapeDtypeStruct((batch, RET_HEADS, RET_DK, RET_DK), F32),
            jax.ShapeDtypeStruct((batch, LRU_WIDTH), F32),
            jax.ShapeDtypeStruct((CONV_W - 1, batch, LRU_WIDTH), F32),
        ],
        scratch_shapes=[pltpu.VMEM((bt, D_MODEL), BF16)],
        compiler_params=pltpu.CompilerParams(
            dimension_semantics=("arbitrary",), vmem_limit_bytes=_vmem_limit(est)),
        name="mixa_sample",
    )(gdec, x, n1, win, wkt, cosf, sinf, cost, sint, s0, h0, cb0, gn, cw, cbias, wabd, wxbd,
      ba, bx, lam, wout)


def _rope_tables(pos):
    half = RET_DK // 2
    inv = ROPE_BASE ** (-jnp.arange(half, dtype=F32) / half)
    ang = pos[:, None] * inv[None, :]
    cos, sin = jnp.cos(ang), jnp.sin(ang)
    return jnp.concatenate([cos, cos], axis=1), jnp.concatenate([-sin, sin], axis=1)


def _decay_tables(c):
    lg = jnp.log1p(-jnp.exp2(-5.0 - jnp.arange(RET_HEADS, dtype=F32)))
    idx = jnp.arange(c, dtype=F32)
    diff = idx[:, None] - idx[None, :]
    causal = diff >= 0
    dmat = jnp.where(causal[None], jnp.exp(jnp.where(causal, diff, 0.0)[None] * lg[:, None, None]), 0.0)
    zeta = jnp.exp((c - 1.0 - idx)[None, :] * lg[:, None])
    xi = jnp.exp((idx + 1.0)[None, :] * lg[:, None])
    gdec = jnp.exp(c * lg)
    return lg, dmat, zeta, xi, gdec


def _block_diag_pairs(w):
    z = jnp.zeros((LRU_BW, LRU_BW), w.dtype)
    tiles = [jnp.block([[w[2 * g], z], [z, w[2 * g + 1]]]) for g in range(LRU_BLOCKS // 2)]
    return jnp.stack(tiles).astype(BF16)


def kernel(x_prompt, x_sample, state_ret, state_lru, state_conv, norm1, norm2, norm_f, w_in_a, ret_gn,
           conv_w, conv_b, lru_wa, lru_ba, lru_wx, lru_bx, lru_lambda, w_out_a, w_in_c, sg_norm_g,
           sg_norm_b, sg_ws, sg_bs, w_out_c, ffn_wg, ffn_wu, ffn_wd):
    bp, lp, _ = x_prompt.shape
    bs = x_sample.shape[0]
    row = lambda v: v.reshape(1, -1)

    win_a = w_in_a[0].astype(BF16)
    wkt_a = w_in_a[0][:, RET_WIDTH:2 * RET_WIDTH].T.astype(BF16)
    wout_a = w_out_a[0].astype(BF16)
    wabd = _block_diag_pairs(lru_wa[0])
    wxbd = _block_diag_pairs(lru_wx[0])
    win_c = w_in_c[0].astype(BF16)
    wout_c = w_out_c[0].astype(BF16)
    wg = ffn_wg.astype(BF16)
    wu = ffn_wu.astype(BF16)
    wd = ffn_wd.astype(BF16)
    mixa_w = (row(ret_gn[0]), conv_w[0], row(conv_b[0]), wabd, wxbd, row(lru_ba[0]),
              row(lru_bx[0]), row(lru_lambda[0]), wout_a)

    tm = 512
    cos_p, sin_p = _rope_tables(jnp.arange(lp, dtype=F32))
    _, dmat, zeta, xi, gdec_p = _decay_tables(RET_CHUNK)
    bc = lambda t: jnp.broadcast_to(t[:, :, None], (RET_HEADS, RET_CHUNK, RET_DK))
    zeros = lambda *s: jnp.zeros(s, F32)
    xp = x_prompt.reshape(bp * lp, D_MODEL)
    xp, ret_p, lru_p, conv_p = _mixa_prompt(
        xp, gdec_p, row(norm1[0]), win_a, cos_p, sin_p, dmat, bc(zeta), bc(xi),
        zeros(bp, RET_HEADS, RET_DK, RET_DK), zeros(bp, 1, LRU_WIDTH),
        zeros(bp, CONV_W - 1, LRU_WIDTH), *mixa_w, batch=bp, seq=lp, tm=tm)
    xp = _ffn(xp, row(norm2[0]), wg[0], wu[0], wd[0], row(norm_f), final_norm=False, tm=tm)
    bsb = jnp.broadcast_to(sg_bs[0][:, :, None], (SG_GROUPS, SG_CHUNK, SG_GW))
    xp = _mixc_prompt(xp, row(norm1[1]), win_c, row(sg_norm_g[0]), row(sg_norm_b[0]), sg_ws[0],
                      bsb, wout_c, tm=tm)
    y_prompt = _ffn(xp, row(norm2[1]), wg[1], wu[1], wd[1], row(norm_f), final_norm=True, tm=tm)

    bt = 16
    cos_s, sin_s = _rope_tables(PAST_LEN + jnp.arange(1, dtype=F32))
    _, _, _, _, gdec_s = _decay_tables(1)
    cost = jnp.broadcast_to(cos_s.reshape(RET_DK, 1), (RET_DK, bt))
    sint = jnp.broadcast_to(sin_s.reshape(RET_DK, 1), (RET_DK, bt))
    xs = x_sample.reshape(bs, D_MODEL)
    xs, ret_s, lru_s, conv_s = _mixa_sample(
        xs, gdec_s, row(norm1[0]), win_a, wkt_a, cos_s, sin_s, cost, sint, state_ret[0],
        state_lru[0], jnp.transpose(state_conv[0], (1, 0, 2)), *mixa_w, bt=bt)
    xs = _ffn(xs, row(norm2[0]), wg[0], wu[0], wd[0], row(norm_f), final_norm=False, tm=bs)
    sg_scale = row(jnp.repeat(sg_ws[0][:, 0, 0], SG_GW))
    sg_bias = row(jnp.repeat(sg_bs[0][:, 0], SG_GW))
    xs, v_s = _mixc_sample(xs, row(norm1[1]), win_c, row(sg_norm_g[0]), row(sg_norm_b[0]),
                           sg_scale, sg_bias, wout_c)
    y_sample = _ffn(xs, row(norm2[1]), wg[1], wu[1], wd[1], row(norm_f), final_norm=True, tm=bs)

    return (y_prompt.reshape(bp, lp, D_MODEL),
            y_sample.reshape(bs, 1, D_MODEL),
            ret_p[None],
            ret_s[None],
            lru_p.reshape(1, bp, LRU_WIDTH),
            lru_s[None],
            conv_p[None],
            jnp.transpose(conv_s, (1, 0, 2))[None],
            v_s.reshape(1, bs, 1, SG_HALF))
```

```python
import functools
import math

import jax
import jax.numpy as jnp
from jax import lax
from jax.experimental import pallas as pl
from jax.experimental.pallas import tpu as pltpu

F32 = jnp.float32
BF16 = jnp.bfloat16

D_MODEL = 1024
RET_WIDTH = 512
RET_HEADS = 4
RET_DK = 128
RET_CHUNK = 128
ROPE_BASE = 10000.0
LRU_WIDTH = 512
LRU_BLOCKS = 8
LRU_BW = 64
LRU_C = 8.0
CONV_W = 4
SG_CHUNK = 128
SG_HALF = 1024
SG_GROUPS = 8
SG_GW = 128
D_FF = 2816
EPS = 1e-6
PAST_LEN = 16384

V7X_VMEM_BYTES = 64 * 1024 * 1024
SUBLANES = 8
LANES = 128

Q_SCALE = RET_DK ** -0.5
GELU_C = math.sqrt(2.0 / math.pi)


def _vmem_limit(estimate_bytes):
    return int(min(estimate_bytes * 3 // 2 + (8 << 20), V7X_VMEM_BYTES - (6 << 20)))


def _resident(shape):
    nd = len(shape)
    return pl.BlockSpec(shape, lambda *_: (0,) * nd, pipeline_mode=pl.Buffered(1))


def _dot(a, b):
    return jnp.dot(a, b, preferred_element_type=F32)


def _rms(x, g):
    ms = jnp.mean(x * x, axis=-1, keepdims=True)
    return x * lax.rsqrt(ms + EPS) * g


def _gelu(x):
    return x * (0.5 * (1.0 + jnp.tanh(GELU_C * (x + 0.044715 * (x * x * x)))))


def _silu(x):
    return x * jax.nn.sigmoid(x)


def _softplus(x):
    return jnp.maximum(x, 0.0) + jnp.log1p(jnp.exp(-jnp.abs(x)))


def _group_norm(o):
    mu = jnp.mean(o, axis=-1, keepdims=True)
    oc = o - mu
    var = jnp.mean(oc * oc, axis=-1, keepdims=True)
    return oc * lax.rsqrt(var + EPS)


def _rotary(x, cosf, sinf):
    return x * cosf + pltpu.roll(x, RET_DK // 2, 1) * sinf


def _lru_coeffs(xc, wgate_ref, ba, bx, lam):
    xcb = xc.astype(BF16)
    ra, rx = [], []
    for g in range(LRU_WIDTH // LANES):
        cols = slice(g * LANES, (g + 1) * LANES)
        both = _dot(xcb[:, cols], wgate_ref[g])
        ra.append(both[:, :LANES])
        rx.append(both[:, LANES:])
    r = jax.nn.sigmoid(jnp.concatenate(ra, axis=1) + ba)
    i = jax.nn.sigmoid(jnp.concatenate(rx, axis=1) + bx)
    log_a = (-LRU_C * r) * _softplus(-lam)
    a = jnp.exp(log_a)
    mult = jnp.sqrt(-jnp.tanh(log_a) * (a * a + 1.0))
    return a, xc * i * mult


def _ffn_kernel(x_ref, n2_ref, wg_ref, wu_ref, wd_ref, nf_ref, o_ref, *, final_norm):
    x = x_ref[...]
    h = _rms(x, n2_ref[...]).astype(BF16)
    act = (_silu(_dot(h, wg_ref[...])) * _dot(h, wu_ref[...])).astype(BF16)
    y = x + _dot(act, wd_ref[...])
    if final_norm:
        y = _rms(y, nf_ref[...])
    o_ref[...] = y


def _ffn(x, n2, wg, wu, wd, nf, *, layer, final_norm, tm):
    t = x.shape[0]
    est = 3 * D_MODEL * D_FF * 2 + 4 * tm * D_MODEL * 4 + 3 * tm * D_FF * 4
    layer_blk = lambda shape: pl.BlockSpec((None,) + shape, lambda i: (layer, 0, 0),
                                           pipeline_mode=pl.Buffered(1))
    return pl.pallas_call(
        functools.partial(_ffn_kernel, final_norm=final_norm),
        grid=(t // tm,),
        in_specs=[
            pl.BlockSpec((tm, D_MODEL), lambda i: (i, 0)),
            _resident((1, D_MODEL)),
            layer_blk((D_MODEL, D_FF)),
            layer_blk((D_MODEL, D_FF)),
            layer_blk((D_FF, D_MODEL)),
            _resident((1, D_MODEL)),
        ],
        out_specs=pl.BlockSpec((tm, D_MODEL), lambda i: (i, 0)),
        out_shape=jax.ShapeDtypeStruct((t, D_MODEL), F32),
        compiler_params=pltpu.CompilerParams(
            dimension_semantics=("arbitrary",), vmem_limit_bytes=_vmem_limit(est)),
        name="ffn_final" if final_norm else "ffn",
    )(x, n2, wg, wu, wd, nf)


def _sgu_in(x, n1, win_ref, vg, vb):
    hn = _rms(x, n1).astype(BF16)
    u = _gelu(_dot(hn, win_ref[:, :SG_HALF]))
    zv = _gelu(_dot(hn, win_ref[:, SG_HALF:]))
    mu = jnp.mean(zv, axis=-1, keepdims=True)
    vc = zv - mu
    var = jnp.mean(vc * vc, axis=-1, keepdims=True)
    return u, vc * lax.rsqrt(var + EPS) * vg + vb


def _mixc_prompt_kernel(x_ref, n1_ref, win_ref, vg_ref, vb_ref, ws_ref, bsb_ref, wout_ref,
                        o_ref, u_s, v_s, gated_s, *, tm):
    x = x_ref[...]
    u, vn = _sgu_in(x, n1_ref[...], win_ref, vg_ref[...], vb_ref[...])
    u_s[...] = u
    v_s[...] = vn.astype(BF16)
    ri = lax.broadcasted_iota(jnp.int32, (SG_CHUNK, SG_CHUNK), 0)
    ci = lax.broadcasted_iota(jnp.int32, (SG_CHUNK, SG_CHUNK), 1)
    for g in range(SG_GROUPS):
        cols = slice(g * SG_GW, (g + 1) * SG_GW)
        wm = jnp.where(ri >= ci, ws_ref[g], 0.0).astype(BF16)
        nchunk = tm // SG_CHUNK
        vcat = jnp.concatenate(
            [v_s[c * SG_CHUNK:(c + 1) * SG_CHUNK, cols] for c in range(nchunk)], axis=1)
        sv_all = _dot(wm, vcat)
        for c in range(nchunk):
            rows = slice(c * SG_CHUNK, (c + 1) * SG_CHUNK)
            sv = sv_all[:, c * SG_GW:(c + 1) * SG_GW] + bsb_ref[g]
            gated_s[rows, cols] = (u_s[rows, cols] * sv).astype(BF16)
    o_ref[...] = x + _dot(gated_s[...], wout_ref[...])


def _mixc_prompt(x, n1, win, vg, vb, ws, bsb, wout, *, tm):
    t = x.shape[0]
    est = 3 * D_MODEL * SG_HALF * 2 + 4 * tm * D_MODEL * 4 + 4 * tm * 2 * SG_HALF * 4
    return pl.pallas_call(
        functools.partial(_mixc_prompt_kernel, tm=tm),
        grid=(t // tm,),
        in_specs=[
            pl.BlockSpec((tm, D_MODEL), lambda i: (i, 0)),
            _resident((1, D_MODEL)),
            _resident((D_MODEL, 2 * SG_HALF)),
            _resident((1, SG_HALF)),
            _resident((1, SG_HALF)),
            _resident((SG_GROUPS, SG_CHUNK, SG_CHUNK)),
            _resident((SG_GROUPS, SG_CHUNK, SG_GW)),
            _resident((SG_HALF, D_MODEL)),
        ],
        out_specs=pl.BlockSpec((tm, D_MODEL), lambda i: (i, 0)),
        out_shape=jax.ShapeDtypeStruct((t, D_MODEL), F32),
        scratch_shapes=[
            pltpu.VMEM((tm, SG_HALF), F32),
            pltpu.VMEM((tm, SG_HALF), BF16),
            pltpu.VMEM((tm, SG_HALF), BF16),
        ],
        compiler_params=pltpu.CompilerParams(
            dimension_semantics=("arbitrary",), vmem_limit_bytes=_vmem_limit(est)),
        name="mixc_prompt",
    )(x, n1, win, vg, vb, ws, bsb, wout)


def _mixc_sample_kernel(x_ref, n1_ref, win_ref, vg_ref, vb_ref, scale_ref, bias_ref, wout_ref,
                        o_ref, v_ref):
    x = x_ref[...]
    u, vn = _sgu_in(x, n1_ref[...], win_ref, vg_ref[...], vb_ref[...])
    v_ref[...] = vn
    sv = vn * scale_ref[...] + bias_ref[...]
    o_ref[...] = x + _dot((u * sv).astype(BF16), wout_ref[...])


def _mixc_sample(x, n1, win, vg, vb, scale, bias, wout):
    t = x.shape[0]
    est = 3 * D_MODEL * SG_HALF * 2 + 8 * t * 2 * SG_HALF * 4
    return pl.pallas_call(
        _mixc_sample_kernel,
        grid=(1,),
        in_specs=[
            _resident((t, D_MODEL)),
            _resident((1, D_MODEL)),
            _resident((D_MODEL, 2 * SG_HALF)),
            _resident((1, SG_HALF)),
            _resident((1, SG_HALF)),
            _resident((1, SG_HALF)),
            _resident((1, SG_HALF)),
            _resident((SG_HALF, D_MODEL)),
        ],
        out_specs=[
            pl.BlockSpec((t, D_MODEL), lambda i: (0, 0)),
            pl.BlockSpec((t, SG_HALF), lambda i: (0, 0)),
        ],
        out_shape=[
            jax.ShapeDtypeStruct((t, D_MODEL), F32),
            jax.ShapeDtypeStruct((t, SG_HALF), F32),
        ],
        compiler_params=pltpu.CompilerParams(
            dimension_semantics=("arbitrary",), vmem_limit_bytes=_vmem_limit(est)),
        name="mixc_sample",
    )(x, n1, win, vg, vb, scale, bias, wout)


def _lru_scan(a, b, h0, h_s, *, tm):
    w = a.shape[1]
    groups = tm // SUBLANES
    a3 = a.reshape(groups, SUBLANES, w)
    b3 = b.reshape(groups, SUBLANES, w)
    sub = lax.broadcasted_iota(jnp.int32, (groups, SUBLANES, w), 1)
    for s in (1, 2, 4):
        a_sh = pltpu.roll(a3, s, 1)
        b_sh = pltpu.roll(b3, s, 1)
        keep = sub >= s
        b3 = jnp.where(keep, a3 * b_sh + b3, b3)
        a3 = jnp.where(keep, a3 * a_sh, a3)
    hprev = jnp.broadcast_to(h0, (SUBLANES, w))
    for r in range(groups):
        hr = a3[r] * hprev + b3[r]
        h_s[r * SUBLANES:(r + 1) * SUBLANES, :] = hr
        hprev = jnp.broadcast_to(hr[SUBLANES - 1:SUBLANES, :], (SUBLANES, w))
    return hprev[0:1, :]


def _mixa_prompt_kernel(gdec_ref, x_ref, n1_ref, win_ref, cos_ref, sin_ref, dmat_ref, zeta_ref,
                        xi_ref, s0_ref, h0_ref, cb0_ref, gn_ref, cw_ref, cbias_ref, wgate_ref,
                        ba_ref, bx_ref, lam_ref, wout_ref,
                        o_ref, s_ref, hT_ref, cT_ref,
                        q_s, qx_s, k_s, kz_s, v_s, sg_s, gg_s, xp_s, h_s, ymix_s, sb_s, *, tm):
    nchunk = tm // RET_CHUNK

    @pl.when(pl.program_id(1) == 0)
    def _():
        s_ref[...] = s0_ref[...]
        hT_ref[...] = h0_ref[...]
        cT_ref[...] = cb0_ref[...]

    x = x_ref[...]
    hn = _rms(x, n1_ref[...]).astype(BF16)
    cosf = cos_ref[...]
    sinf = sin_ref[...]
    w = RET_WIDTH
    zq = _dot(hn, win_ref[:, 0 * w:1 * w])
    zk = _dot(hn, win_ref[:, 1 * w:2 * w])
    for h in range(RET_HEADS):
        cols = slice(h * RET_DK, (h + 1) * RET_DK)
        qr = _rotary(zq[:, cols], cosf, sinf) * Q_SCALE
        q_s[:, cols] = qr.astype(BF16)
        kr = _rotary(zk[:, cols], cosf, sinf)
        k_s[:, cols] = kr.astype(BF16)
        for c in range(nchunk):
            rows = slice(c * RET_CHUNK, (c + 1) * RET_CHUNK)
            kz_s[rows, cols] = (kr[rows, :] * zeta_ref[h]).astype(BF16)
            qx_s[rows, cols] = (qr[rows, :] * xi_ref[h]).astype(BF16)
    v_s[...] = _dot(hn, win_ref[:, 2 * w:3 * w]).astype(BF16)
    sg_s[...] = _silu(_dot(hn, win_ref[:, 3 * w:4 * w]))
    xb = _dot(hn, win_ref[:, 4 * w:5 * w])
    gg_s[...] = _gelu(_dot(hn, win_ref[:, 5 * w:6 * w]))

    pad = SUBLANES
    xp_s[pad - (CONV_W - 1):pad, :] = cT_ref[0]
    xp_s[pad:pad + tm, :] = xb
    cT_ref[0] = xp_s[pad + tm - (CONV_W - 1):pad + tm, :]
    xc = cbias_ref[...]
    for j in range(CONV_W):
        off = pad - (CONV_W - 1) + j
        xc = xc + xp_s[off:off + tm, :] * cw_ref[j:j + 1, :]

    a, b_in = _lru_coeffs(xc, wgate_ref, ba_ref[...], bx_ref[...], lam_ref[...])
    hT_ref[0] = _lru_scan(a, b_in, hT_ref[0], h_s, tm=tm)
    ymix_s[:, RET_WIDTH:] = (h_s[...] * gg_s[...]).astype(BF16)

    for h in range(RET_HEADS):
        sb_s[h] = s_ref[0, h].astype(BF16)
    for c in range(nchunk):
        rows = slice(c * RET_CHUNK, (c + 1) * RET_CHUNK)
        for h in range(RET_HEADS):
            cols = slice(h * RET_DK, (h + 1) * RET_DK)
            qs = q_s[rows, cols]
            vb = v_s[rows, cols]
            sc = lax.dot_general(qs, k_s[rows, cols], (((1,), (1,)), ((), ())),
                                 preferred_element_type=F32) * dmat_ref[h]
            o = _dot(jnp.concatenate([sc.astype(BF16), qx_s[rows, cols]], axis=1),
                     jnp.concatenate([vb, sb_s[h]], axis=0))
            ya = _group_norm(o) * gn_ref[:, cols] * sg_s[rows, cols]
            ymix_s[rows, cols] = ya.astype(BF16)
            u = lax.dot_general(kz_s[rows, cols], vb, (((0,), (0,)), ((), ())),
                                preferred_element_type=F32)
            s_new = gdec_ref[h] * s_ref[0, h] + u
            s_ref[0, h] = s_new
            sb_s[h] = s_new.astype(BF16)

    o_ref[...] = x + _dot(ymix_s[...], wout_ref[...])


def _mixa_prompt(x, gdec, n1, win, cosf, sinf, dmat, zeta, xi, s0, h0, cb0, gn, cw, cbias,
                 wgate, ba, bx, lam, wout, *, batch, seq, tm):
    nt = seq // tm
    in_w = 6 * RET_WIDTH
    est = (D_MODEL * in_w * 2 + D_MODEL * D_MODEL * 2 + 4 * tm * D_MODEL * 4
           + 10 * tm * RET_WIDTH * 4 + 3 * tm * in_w * 4)
    row_blk = lambda b, t: (b * nt + t, 0)
    per_b4 = lambda b, t: (b, 0, 0, 0)
    per_b3 = lambda b, t: (b, 0, 0)
    hd = (RET_HEADS, RET_CHUNK, RET_CHUNK)
    return pl.pallas_call(
        functools.partial(_mixa_prompt_kernel, tm=tm),
        grid=(batch, nt),
        in_specs=[
            pl.BlockSpec(memory_space=pltpu.SMEM),
            pl.BlockSpec((tm, D_MODEL), row_blk),
            _resident((1, D_MODEL)),
            _resident((D_MODEL, in_w)),
            pl.BlockSpec((tm, RET_DK), lambda b, t: (t, 0)),
            pl.BlockSpec((tm, RET_DK), lambda b, t: (t, 0)),
            _resident(hd),
            _resident(hd),
            _resident(hd),
            pl.BlockSpec((1, RET_HEADS, RET_DK, RET_DK), per_b4),
            pl.BlockSpec((1, 1, LRU_WIDTH), per_b3),
            pl.BlockSpec((1, CONV_W - 1, LRU_WIDTH), per_b3),
            _resident((1, RET_WIDTH)),
            _resident((CONV_W, LRU_WIDTH)),
            _resident((1, LRU_WIDTH)),
            _resident((LRU_WIDTH // LANES, LANES, 2 * LANES)),
            _resident((1, LRU_WIDTH)),
            _resident((1, LRU_WIDTH)),
            _resident((1, LRU_WIDTH)),
            _resident((D_MODEL, D_MODEL)),
        ],
        out_specs=[
            pl.BlockSpec((tm, D_MODEL), row_blk),
            pl.BlockSpec((1, RET_HEADS, RET_DK, RET_DK), per_b4),
            pl.BlockSpec((1, 1, LRU_WIDTH), per_b3),
            pl.BlockSpec((1, CONV_W - 1, LRU_WIDTH), per_b3),
        ],
        out_shape=[
            jax.ShapeDtypeStruct((batch * seq, D_MODEL), F32),
            jax.ShapeDtypeStruct((batch, RET_HEADS, RET_DK, RET_DK), F32),
            jax.ShapeDtypeStruct((batch, 1, LRU_WIDTH), F32),
            jax.ShapeDtypeStruct((batch, CONV_W - 1, LRU_WIDTH), F32),
        ],
        scratch_shapes=[
            pltpu.VMEM((tm, RET_WIDTH), BF16),
            pltpu.VMEM((tm, RET_WIDTH), BF16),
            pltpu.VMEM((tm, RET_WIDTH), BF16),
            pltpu.VMEM((tm, RET_WIDTH), BF16),
            pltpu.VMEM((tm, RET_WIDTH), BF16),
            pltpu.VMEM((tm, RET_WIDTH), F32),
            pltpu.VMEM((tm, LRU_WIDTH), F32),
            pltpu.VMEM((tm + 2 * SUBLANES, LRU_WIDTH), F32),
            pltpu.VMEM((tm, LRU_WIDTH), F32),
            pltpu.VMEM((tm, D_MODEL), BF16),
            pltpu.VMEM((RET_HEADS, RET_DK, RET_DK), BF16),
        ],
        compiler_params=pltpu.CompilerParams(
            dimension_semantics=("arbitrary", "arbitrary"), vmem_limit_bytes=_vmem_limit(est)),
        name="mixa_prompt",
    )(gdec, x, n1, win, cosf, sinf, dmat, zeta, xi, s0, h0, cb0, gn, cw, cbias, wgate,
      ba, bx, lam, wout)


def _mixa_sample_kernel(gdec_ref, x_ref, n1_ref, win_ref, wkt_ref, cos_ref, sin_ref, cost_ref,
                        sint_ref, s0_ref, h0_ref, cb0_ref, gn_ref, cw_ref, cbias_ref, wgate_ref,
                        ba_ref, bx_ref, lam_ref, wout_ref,
                        o_ref, s_ref, hT_ref, cT_ref, ymix_s, *, bt):
    x = x_ref[...]
    hn = _rms(x, n1_ref[...]).astype(BF16)
    cosf = cos_ref[...]
    sinf = sin_ref[...]
    w = RET_WIDTH
    zq = _dot(hn, win_ref[:, 0 * w:1 * w])
    zk = _dot(hn, win_ref[:, 1 * w:2 * w])
    zv = _dot(hn, win_ref[:, 2 * w:3 * w])
    sg = _silu(_dot(hn, win_ref[:, 3 * w:4 * w]))
    xb = _dot(hn, win_ref[:, 4 * w:5 * w])
    gg = _gelu(_dot(hn, win_ref[:, 5 * w:6 * w]))
    zkt = lax.dot_general(wkt_ref[...], hn, (((1,), (1,)), ((), ())), preferred_element_type=F32)

    rowi = lax.broadcasted_iota(jnp.int32, (bt, RET_DK), 0)
    half = RET_DK // 2
    for h in range(RET_HEADS):
        cols = slice(h * RET_DK, (h + 1) * RET_DK)
        qs = (_rotary(zq[:, cols], cosf, sinf) * Q_SCALE).astype(BF16)
        kb = _rotary(zk[:, cols], cosf, sinf).astype(BF16)
        vh = zv[:, cols]
        kt = zkt[cols, :]
        kt_rolled = jnp.concatenate([kt[half:, :], kt[:half, :]], axis=0)
        ktb = (kt * cost_ref[...] + kt_rolled * sint_ref[...]).astype(BF16)
        qk = jnp.sum(qs.astype(F32) * kb.astype(F32), axis=-1, keepdims=True)
        cross = jnp.zeros((bt, RET_DK), F32)
        for b in range(bt):
            s_old = s0_ref[b, h]
            cr = _dot(qs, s_old.astype(BF16))
            cross = cross + jnp.where(rowi == b, cr, 0.0)
            u = _dot(ktb, jnp.where(rowi == b, vh, 0.0).astype(BF16))
            s_ref[b, h] = gdec_ref[h] * s_old + u
        o = qk * vh + cross * gdec_ref[h]
        ymix_s[:, cols] = (_group_norm(o) * gn_ref[:, cols] * sg[:, cols]).astype(BF16)

    xc = cbias_ref[...]
    for j in range(CONV_W - 1):
        xc = xc + cb0_ref[j] * cw_ref[j:j + 1, :]
    xc = xc + xb * cw_ref[CONV_W - 1:CONV_W, :]
    for j in range(CONV_W - 2):
        cT_ref[j] = cb0_ref[j + 1]
    cT_ref[CONV_W - 2] = xb
    a, b_in = _lru_coeffs(xc, wgate_ref, ba_ref[...], bx_ref[...], lam_ref[...])
    hnew = a * h0_ref[...] + b_in
    hT_ref[...] = hnew
    ymix_s[:, RET_WIDTH:] = (hnew * gg).astype(BF16)
    o_ref[...] = x + _dot(ymix_s[...], wout_ref[...])


def _mixa_sample(x, gdec, n1, win, wkt, cosf, sinf, cost, sint, s0, h0, cb0, gn, cw, cbias,
                 wgate, ba, bx, lam, wout, *, bt):
    batch = x.shape[0]
    in_w = 6 * RET_WIDTH
    est = (D_MODEL * (in_w + RET_WIDTH) * 2 + D_MODEL * D_MODEL * 2
           + 4 * bt * RET_HEADS * RET_DK * RET_DK * 4 + 16 * bt * in_w * 4)
    rows = lambda i: (i, 0)
    return pl.pallas_call(
        functools.partial(_mixa_sample_kernel, bt=bt),
        grid=(batch // bt,),
        in_specs=[
            pl.BlockSpec(memory_space=pltpu.SMEM),
            pl.BlockSpec((bt, D_MODEL), rows),
            _resident((1, D_MODEL)),
            _resident((D_MODEL, in_w)),
            _resident((RET_WIDTH, D_MODEL)),
            _resident((1, RET_DK)),
            _resident((1, RET_DK)),
            _resident((RET_DK, bt)),
            _resident((RET_DK, bt)),
            pl.BlockSpec((bt, RET_HEADS, RET_DK, RET_DK), lambda i: (i, 0, 0, 0)),
            pl.BlockSpec((bt, LRU_WIDTH), rows),
            pl.BlockSpec((CONV_W - 1, bt, LRU_WIDTH), lambda i: (0, i, 0)),
            _resident((1, RET_WIDTH)),
            _resident((CONV_W, LRU_WIDTH)),
            _resident((1, LRU_WIDTH)),
            _resident((LRU_WIDTH // LANES, LANES, 2 * LANES)),
            _resident((1, LRU_WIDTH)),
            _resident((1, LRU_WIDTH)),
            _resident((1, LRU_WIDTH)),
            _resident((D_MODEL, D_MODEL)),
        ],
        out_specs=[
            pl.BlockSpec((bt, D_MODEL), rows),
            pl.BlockSpec((bt, RET_HEADS, RET_DK, RET_DK), lambda i: (i, 0, 0, 0)),
            pl.BlockSpec((bt, LRU_WIDTH), rows),
            pl.BlockSpec((CONV_W - 1, bt, LRU_WIDTH), lambda i: (0, i, 0)),
        ],
        out_shape=[
            jax.ShapeDtypeStruct((batch, D_MODEL), F32),
            jax.ShapeDtypeStruct((batch, RET_HEADS, RET_DK, RET_DK), F32),
            jax.ShapeDtypeStruct((batch, LRU_WIDTH), F32),
            jax.ShapeDtypeStruct((CONV_W - 1, batch, LRU_WIDTH), F32),
        ],
        scratch_shapes=[pltpu.VMEM((bt, D_MODEL), BF16)],
        compiler_params=pltpu.CompilerParams(
            dimension_semantics=("arbitrary",), vmem_limit_bytes=_vmem_limit(est)),
        name="mixa_sample",
    )(gdec, x, n1, win, wkt, cosf, sinf, cost, sint, s0, h0, cb0, gn, cw, cbias, wgate,
      ba, bx, lam, wout)


def _rope_tables(pos):
    half = RET_DK // 2
    inv = ROPE_BASE ** (-jnp.arange(half, dtype=F32) / half)
    ang = pos[:, None] * inv[None, :]
    cos, sin = jnp.cos(ang), jnp.sin(ang)
    return jnp.concatenate([cos, cos], axis=1), jnp.concatenate([-sin, sin], axis=1)


def _decay_tables(c):
    lg = jnp.log1p(-jnp.exp2(-5.0 - jnp.arange(RET_HEADS, dtype=F32)))
    idx = jnp.arange(c, dtype=F32)
    diff = idx[:, None] - idx[None, :]
    causal = diff >= 0
    dmat = jnp.where(causal[None], jnp.exp(jnp.where(causal, diff, 0.0)[None] * lg[:, None, None]), 0.0)
    zeta = jnp.exp((c - 1.0 - idx)[None, :] * lg[:, None])
    xi = jnp.exp((idx + 1.0)[None, :] * lg[:, None])
    gdec = jnp.exp(c * lg)
    return lg, dmat, zeta, xi, gdec


def _gate_tiles(wa, wx):
    z = jnp.zeros((LRU_BW, LRU_BW), wa.dtype)
    bd = lambda w, g: jnp.block([[w[2 * g], z], [z, w[2 * g + 1]]])
    tiles = [jnp.concatenate([bd(wa, g), bd(wx, g)], axis=1) for g in range(LRU_BLOCKS // 2)]
    return jnp.stack(tiles).astype(BF16)


def kernel(x_prompt, x_sample, state_ret, state_lru, state_conv, norm1, norm2, norm_f, w_in_a, ret_gn,
           conv_w, conv_b, lru_wa, lru_ba, lru_wx, lru_bx, lru_lambda, w_out_a, w_in_c, sg_norm_g,
           sg_norm_b, sg_ws, sg_bs, w_out_c, ffn_wg, ffn_wu, ffn_wd):
    bp, lp, _ = x_prompt.shape
    bs = x_sample.shape[0]
    row = lambda v: v.reshape(1, -1)

    win_a = w_in_a[0].astype(BF16)
    wkt_a = win_a[:, RET_WIDTH:2 * RET_WIDTH].T
    wout_a = w_out_a[0].astype(BF16)
    wgate = _gate_tiles(lru_wa[0], lru_wx[0])
    win_c = w_in_c[0].astype(BF16)
    wout_c = w_out_c[0].astype(BF16)
    wg = ffn_wg.astype(BF16)
    wu = ffn_wu.astype(BF16)
    wd = ffn_wd.astype(BF16)
    mixa_w = (row(ret_gn[0]), conv_w[0], row(conv_b[0]), wgate, row(lru_ba[0]),
              row(lru_bx[0]), row(lru_lambda[0]), wout_a)

    tm = 512
    cos_p, sin_p = _rope_tables(jnp.arange(lp, dtype=F32))
    _, dmat, zeta, xi, gdec_p = _decay_tables(RET_CHUNK)
    bc = lambda t: jnp.broadcast_to(t[:, :, None], (RET_HEADS, RET_CHUNK, RET_DK))
    zeros = lambda *s: jnp.zeros(s, F32)
    xp = x_prompt.reshape(bp * lp, D_MODEL)
    xp, ret_p, lru_p, conv_p = _mixa_prompt(
        xp, gdec_p, row(norm1[0]), win_a, cos_p, sin_p, dmat, bc(zeta), bc(xi),
        zeros(bp, RET_HEADS, RET_DK, RET_DK), zeros(bp, 1, LRU_WIDTH),
        zeros(bp, CONV_W - 1, LRU_WIDTH), *mixa_w, batch=bp, seq=lp, tm=tm)
    xp = _ffn(xp, row(norm2[0]), wg, wu, wd, row(norm_f), layer=0, final_norm=False, tm=tm)
    bsb = jnp.broadcast_to(sg_bs[0][:, :, None], (SG_GROUPS, SG_CHUNK, SG_GW))
    xp = _mixc_prompt(xp, row(norm1[1]), win_c, row(sg_norm_g[0]), row(sg_norm_b[0]), sg_ws[0],
                      bsb, wout_c, tm=tm)
    y_prompt = _ffn(xp, row(norm2[1]), wg, wu, wd, row(norm_f), layer=1, final_norm=True, tm=tm)

    bt = 16
    cos_s, sin_s = _rope_tables(PAST_LEN + jnp.arange(1, dtype=F32))
    _, _, _, _, gdec_s = _decay_tables(1)
    cost = jnp.broadcast_to(cos_s.reshape(RET_DK, 1), (RET_DK, bt))
    sint = jnp.broadcast_to(sin_s.reshape(RET_DK, 1), (RET_DK, bt))
    xs = x_sample.reshape(bs, D_MODEL)
    xs, ret_s, lru_s, conv_s = _mixa_sample(
        xs, gdec_s, row(norm1[0]), win_a, wkt_a, cos_s, sin_s, cost, sint, state_ret[0],
        state_lru[0], jnp.transpose(state_conv[0], (1, 0, 2)), *mixa_w, bt=bt)
    xs = _ffn(xs, row(norm2[0]), wg, wu, wd, row(norm_f), layer=0, final_norm=False, tm=bs)
    sg_scale = row(jnp.repeat(sg_ws[0][:, 0, 0], SG_GW))
    sg_bias = row(jnp.repeat(sg_bs[0][:, 0], SG_GW))
    xs, v_s = _mixc_sample(xs, row(norm1[1]), win_c, row(sg_norm_g[0]), row(sg_norm_b[0]),
                           sg_scale, sg_bias, wout_c)
    y_sample = _ffn(xs, row(norm2[1]), wg, wu, wd, row(norm_f), layer=1, final_norm=True, tm=bs)

    return (y_prompt.reshape(bp, lp, D_MODEL),
            y_sample.reshape(bs, 1, D_MODEL),
            ret_p[None],
            ret_s[None],
            lru_p.reshape(1, bp, LRU_WIDTH),
            lru_s[None],
            conv_p[None],
            jnp.transpose(conv_s, (1, 0, 2))[None],
            v_s.reshape(1, bs, 1, SG_HALF))
```

```python
import functools
import math

import jax
import jax.numpy as jnp
from jax import lax
from jax.experimental import pallas as pl
from jax.experimental.pallas import tpu as pltpu

F32 = jnp.float32
BF16 = jnp.bfloat16

D_MODEL = 1024
RET_WIDTH = 512
RET_HEADS = 4
RET_DK = 128
RET_CHUNK = 128
ROPE_BASE = 10000.0
LRU_WIDTH = 512
LRU_BLOCKS = 8
LRU_BW = 64
LRU_C = 8.0
CONV_W = 4
SG_CHUNK = 128
SG_HALF = 1024
SG_GROUPS = 8
SG_GW = 128
D_FF = 2816
EPS = 1e-6
PAST_LEN = 16384

V7X_VMEM_BYTES = 64 * 1024 * 1024
SUBLANES = 8
LANES = 128

Q_SCALE = RET_DK ** -0.5
GELU_C = math.sqrt(2.0 / math.pi)


def _vmem_limit(estimate_bytes):
    return int(min(estimate_bytes * 3 // 2 + (8 << 20), V7X_VMEM_BYTES - (6 << 20)))


def _resident(shape):
    nd = len(shape)
    return pl.BlockSpec(shape, lambda *_: (0,) * nd, pipeline_mode=pl.Buffered(1))


def _dot(a, b):
    return jnp.dot(a, b, preferred_element_type=F32)


def _rms(x, g):
    ms = jnp.mean(x * x, axis=-1, keepdims=True)
    return x * lax.rsqrt(ms + EPS) * g


def _gelu(x):
    return x * (0.5 * (1.0 + jnp.tanh(GELU_C * (x + 0.044715 * (x * x * x)))))


def _silu(x):
    return x * jax.nn.sigmoid(x)


def _softplus(x):
    return jnp.maximum(x, 0.0) + jnp.log1p(jnp.exp(-jnp.abs(x)))


def _group_norm(o):
    mu = jnp.mean(o, axis=-1, keepdims=True)
    oc = o - mu
    var = jnp.mean(oc * oc, axis=-1, keepdims=True)
    return oc * lax.rsqrt(var + EPS)


def _rotary(x, cosf, sinf):
    return x * cosf + pltpu.roll(x, RET_DK // 2, 1) * sinf


def _lru_group(xc, wgate, ba, bx, sp):
    both = _dot(xc.astype(BF16), wgate)
    r = jax.nn.sigmoid(both[:, :LANES] + ba)
    i = jax.nn.sigmoid(both[:, LANES:] + bx)
    log_a = (-LRU_C * r) * sp
    a = jnp.exp(log_a)
    mult = jnp.sqrt(-jnp.tanh(log_a) * (a * a + 1.0))
    return a, xc * i * mult


def _lru_coeffs(xc, wgate_ref, ba, bx, lam):
    xcb = xc.astype(BF16)
    ra, rx = [], []
    for g in range(LRU_WIDTH // LANES):
        cols = slice(g * LANES, (g + 1) * LANES)
        both = _dot(xcb[:, cols], wgate_ref[g])
        ra.append(both[:, :LANES])
        rx.append(both[:, LANES:])
    r = jax.nn.sigmoid(jnp.concatenate(ra, axis=1) + ba)
    i = jax.nn.sigmoid(jnp.concatenate(rx, axis=1) + bx)
    log_a = (-LRU_C * r) * _softplus(-lam)
    a = jnp.exp(log_a)
    mult = jnp.sqrt(-jnp.tanh(log_a) * (a * a + 1.0))
    return a, xc * i * mult


def _ffn_kernel(x_ref, n2_ref, wg_ref, wu_ref, wd_ref, nf_ref, o_ref, *, final_norm):
    x = x_ref[...]
    h = _rms(x, n2_ref[...]).astype(BF16)
    act = (_silu(_dot(h, wg_ref[...])) * _dot(h, wu_ref[...])).astype(BF16)
    y = x + _dot(act, wd_ref[...])
    if final_norm:
        y = _rms(y, nf_ref[...])
    o_ref[...] = y


def _ffn(x, n2, wg, wu, wd, nf, *, layer, final_norm, tm):
    t = x.shape[0]
    est = 3 * D_MODEL * D_FF * 2 + 4 * tm * D_MODEL * 4 + 3 * tm * D_FF * 4
    layer_blk = lambda shape: pl.BlockSpec((None,) + shape, lambda i: (layer, 0, 0),
                                           pipeline_mode=pl.Buffered(1))
    return pl.pallas_call(
        functools.partial(_ffn_kernel, final_norm=final_norm),
        grid=(t // tm,),
        in_specs=[
            pl.BlockSpec((tm, D_MODEL), lambda i: (i, 0)),
            _resident((1, D_MODEL)),
            layer_blk((D_MODEL, D_FF)),
            layer_blk((D_MODEL, D_FF)),
            layer_blk((D_FF, D_MODEL)),
            _resident((1, D_MODEL)),
        ],
        out_specs=pl.BlockSpec((tm, D_MODEL), lambda i: (i, 0)),
        out_shape=jax.ShapeDtypeStruct((t, D_MODEL), F32),
        compiler_params=pltpu.CompilerParams(
            dimension_semantics=("arbitrary",), vmem_limit_bytes=_vmem_limit(est)),
        name="ffn_final" if final_norm else "ffn",
    )(x, n2, wg, wu, wd, nf)


def _sgu_in(x, n1, win_ref, vg, vb):
    hn = _rms(x, n1).astype(BF16)
    u = _gelu(_dot(hn, win_ref[:, :SG_HALF]))
    zv = _gelu(_dot(hn, win_ref[:, SG_HALF:]))
    mu = jnp.mean(zv, axis=-1, keepdims=True)
    vc = zv - mu
    var = jnp.mean(vc * vc, axis=-1, keepdims=True)
    return u, vc * lax.rsqrt(var + EPS) * vg + vb


def _mixc_prompt_kernel(x_ref, n1_ref, win_ref, vg_ref, vb_ref, ws_ref, bsb_ref, wout_ref,
                        o_ref, u_s, v_s, gated_s, *, tm):
    x = x_ref[...]
    u, vn = _sgu_in(x, n1_ref[...], win_ref, vg_ref[...], vb_ref[...])
    u_s[...] = u
    v_s[...] = vn.astype(BF16)
    ri = lax.broadcasted_iota(jnp.int32, (SG_CHUNK, SG_CHUNK), 0)
    ci = lax.broadcasted_iota(jnp.int32, (SG_CHUNK, SG_CHUNK), 1)
    for g in range(SG_GROUPS):
        cols = slice(g * SG_GW, (g + 1) * SG_GW)
        wm = jnp.where(ri >= ci, ws_ref[g], 0.0).astype(BF16)
        nchunk = tm // SG_CHUNK
        vcat = jnp.concatenate(
            [v_s[c * SG_CHUNK:(c + 1) * SG_CHUNK, cols] for c in range(nchunk)], axis=1)
        sv_all = _dot(wm, vcat)
        for c in range(nchunk):
            rows = slice(c * SG_CHUNK, (c + 1) * SG_CHUNK)
            sv = sv_all[:, c * SG_GW:(c + 1) * SG_GW] + bsb_ref[g]
            gated_s[rows, cols] = (u_s[rows, cols] * sv).astype(BF16)
    o_ref[...] = x + _dot(gated_s[...], wout_ref[...])


def _mixc_prompt(x, n1, win, vg, vb, ws, bsb, wout, *, tm):
    t = x.shape[0]
    est = 3 * D_MODEL * SG_HALF * 2 + 4 * tm * D_MODEL * 4 + 4 * tm * 2 * SG_HALF * 4
    return pl.pallas_call(
        functools.partial(_mixc_prompt_kernel, tm=tm),
        grid=(t // tm,),
        in_specs=[
            pl.BlockSpec((tm, D_MODEL), lambda i: (i, 0)),
            _resident((1, D_MODEL)),
            _resident((D_MODEL, 2 * SG_HALF)),
            _resident((1, SG_HALF)),
            _resident((1, SG_HALF)),
            _resident((SG_GROUPS, SG_CHUNK, SG_CHUNK)),
            _resident((SG_GROUPS, SG_CHUNK, SG_GW)),
            _resident((SG_HALF, D_MODEL)),
        ],
        out_specs=pl.BlockSpec((tm, D_MODEL), lambda i: (i, 0)),
        out_shape=jax.ShapeDtypeStruct((t, D_MODEL), F32),
        scratch_shapes=[
            pltpu.VMEM((tm, SG_HALF), F32),
            pltpu.VMEM((tm, SG_HALF), BF16),
            pltpu.VMEM((tm, SG_HALF), BF16),
        ],
        compiler_params=pltpu.CompilerParams(
            dimension_semantics=("arbitrary",), vmem_limit_bytes=_vmem_limit(est)),
        name="mixc_prompt",
    )(x, n1, win, vg, vb, ws, bsb, wout)


def _mixc_sample_kernel(x_ref, n1_ref, win_ref, vg_ref, vb_ref, scale_ref, bias_ref, wout_ref,
                        o_ref, v_ref):
    x = x_ref[...]
    u, vn = _sgu_in(x, n1_ref[...], win_ref, vg_ref[...], vb_ref[...])
    v_ref[...] = vn
    sv = vn * scale_ref[...] + bias_ref[...]
    o_ref[...] = x + _dot((u * sv).astype(BF16), wout_ref[...])


def _mixc_sample(x, n1, win, vg, vb, scale, bias, wout):
    t = x.shape[0]
    est = 3 * D_MODEL * SG_HALF * 2 + 8 * t * 2 * SG_HALF * 4
    return pl.pallas_call(
        _mixc_sample_kernel,
        grid=(1,),
        in_specs=[
            _resident((t, D_MODEL)),
            _resident((1, D_MODEL)),
            _resident((D_MODEL, 2 * SG_HALF)),
            _resident((1, SG_HALF)),
            _resident((1, SG_HALF)),
            _resident((1, SG_HALF)),
            _resident((1, SG_HALF)),
            _resident((SG_HALF, D_MODEL)),
        ],
        out_specs=[
            pl.BlockSpec((t, D_MODEL), lambda i: (0, 0)),
            pl.BlockSpec((t, SG_HALF), lambda i: (0, 0)),
        ],
        out_shape=[
            jax.ShapeDtypeStruct((t, D_MODEL), F32),
            jax.ShapeDtypeStruct((t, SG_HALF), F32),
        ],
        compiler_params=pltpu.CompilerParams(
            dimension_semantics=("arbitrary",), vmem_limit_bytes=_vmem_limit(est)),
        name="mixc_sample",
    )(x, n1, win, vg, vb, scale, bias, wout)


def _lru_scan(a, b, h0):
    tm, w = a.shape
    groups = tm // SUBLANES
    a3 = a.reshape(groups, SUBLANES, w)
    b3 = b.reshape(groups, SUBLANES, w)
    sub = lax.broadcasted_iota(jnp.int32, (groups, SUBLANES, w), 1)
    for s in (1, 2, 4):
        a_sh = pltpu.roll(a3, s, 1)
        b_sh = pltpu.roll(b3, s, 1)
        keep = sub >= s
        b3 = jnp.where(keep, a3 * b_sh + b3, b3)
        a3 = jnp.where(keep, a3 * a_sh, a3)
    hprev = jnp.broadcast_to(h0, (SUBLANES, w))
    hs = []
    for r in range(groups):
        hr = a3[r] * hprev + b3[r]
        hs.append(hr)
        hprev = jnp.broadcast_to(hr[SUBLANES - 1:SUBLANES, :], (SUBLANES, w))
    return jnp.concatenate(hs, axis=0), hprev[0:1, :]


def _mixa_prompt_kernel(gdec_ref, x_ref, n1_ref, win_ref, cos_ref, sin_ref, dmat_ref, zeta_ref,
                        xi_ref, s0_ref, h0_ref, cb0_ref, gn_ref, cw_ref, cbias_ref, wgate_ref,
                        ba_ref, bx_ref, lam_ref, wout_ref,
                        o_ref, s_ref, hT_ref, cT_ref,
                        xp_s, ymix_s, sb_s, *, tm):
    nchunk = tm // RET_CHUNK
    w = RET_WIDTH
    pad = SUBLANES
    tail = CONV_W - 1

    @pl.when(pl.program_id(1) == 0)
    def _():
        s_ref[...] = s0_ref[...]
        hT_ref[...] = h0_ref[...]
        cT_ref[...] = cb0_ref[...]

    xp_s[pad - tail:pad, :] = cT_ref[0]
    for h in range(RET_HEADS):
        sb_s[h] = s_ref[0, h].astype(BF16)
    ngroup = LRU_WIDTH // LANES
    lanes = [slice(g * LANES, (g + 1) * LANES) for g in range(ngroup)]
    hprev = [hT_ref[0, :, lanes[g]] for g in range(ngroup)]
    sp = _softplus(-lam_ref[...])

    def projection(c):
        z = {}

        def part0():
            z["x"] = x_ref[c * RET_CHUNK:(c + 1) * RET_CHUNK, :]
            z["hn"] = _rms(z["x"], n1_ref[...]).astype(BF16)
            xp_s[pad + c * RET_CHUNK:pad + (c + 1) * RET_CHUNK, :] = _dot(
                z["hn"], win_ref[:, 4 * w:5 * w])

        def part1():
            z["gb"] = _dot(z["hn"], win_ref[:, 5 * w:6 * w])
            z["q"] = _dot(z["hn"], win_ref[:, 0 * w:1 * w])

        def part2():
            z["k"] = _dot(z["hn"], win_ref[:, 1 * w:2 * w])
            z["v"] = _dot(z["hn"], win_ref[:, 2 * w:3 * w])

        def part3():
            z["g"] = _dot(z["hn"], win_ref[:, 3 * w:4 * w])

        return z, (part0, part1, part2, part3)

    cur, parts = projection(0)
    for part in parts:
        part()
    for c in range(nchunk):
        rows = slice(c * RET_CHUNK, (c + 1) * RET_CHUNK)
        nxt, parts = projection(c + 1) if c + 1 < nchunk else (None, None)
        cosf = cos_ref[rows, :]
        sinf = sin_ref[rows, :]
        for h in range(RET_HEADS):
            if parts is not None:
                parts[h]()
            x, zgb, zq, zk, zv, zg = (cur[n] for n in ("x", "gb", "q", "k", "v", "g"))
            ln = lanes[h]
            xc = cbias_ref[:, ln]
            for j in range(CONV_W):
                off = pad - tail + j + c * RET_CHUNK
                xc = xc + xp_s[off:off + RET_CHUNK, ln] * cw_ref[j:j + 1, ln]
            a, b_in = _lru_group(xc, wgate_ref[h], ba_ref[:, ln], bx_ref[:, ln], sp[:, ln])
            hl, hprev[h] = _lru_scan(a, b_in, hprev[h])
            ymix_s[rows, RET_WIDTH + h * LANES:RET_WIDTH + (h + 1) * LANES] = (
                hl * _gelu(zgb[:, ln])).astype(BF16)

            cols = slice(h * RET_DK, (h + 1) * RET_DK)
            sg = _silu(zg[:, cols])
            qr = _rotary(zq[:, cols], cosf, sinf) * Q_SCALE
            kr = _rotary(zk[:, cols], cosf, sinf)
            vb = zv[:, cols].astype(BF16)
            sc = lax.dot_general(qr.astype(BF16), kr.astype(BF16), (((1,), (1,)), ((), ())),
                                 preferred_element_type=F32) * dmat_ref[h]
            o = _dot(jnp.concatenate([sc.astype(BF16), (qr * xi_ref[h]).astype(BF16)], axis=1),
                     jnp.concatenate([vb, sb_s[h]], axis=0))
            ya = _group_norm(o) * gn_ref[:, cols] * sg
            ymix_s[rows, cols] = ya.astype(BF16)
            u = lax.dot_general((kr * zeta_ref[h]).astype(BF16), vb, (((0,), (0,)), ((), ())),
                                preferred_element_type=F32)
            s_new = gdec_ref[h] * s_ref[0, h] + u
            s_ref[0, h] = s_new
            sb_s[h] = s_new.astype(BF16)

        o_ref[rows, :] = cur["x"] + _dot(ymix_s[rows, :], wout_ref[...])
        cur = nxt

    for g in range(ngroup):
        hT_ref[0, :, lanes[g]] = hprev[g]
    cT_ref[0] = xp_s[pad + tm - tail:pad + tm, :]


def _mixa_prompt(x, gdec, n1, win, cosf, sinf, dmat, zeta, xi, s0, h0, cb0, gn, cw, cbias,
                 wgate, ba, bx, lam, wout, *, batch, seq, tm):
    nt = seq // tm
    in_w = 6 * RET_WIDTH
    est = (D_MODEL * in_w * 2 + D_MODEL * D_MODEL * 2 + 4 * tm * D_MODEL * 4
           + 10 * tm * RET_WIDTH * 4 + 3 * tm * in_w * 4)
    row_blk = lambda b, t: (b * nt + t, 0)
    per_b4 = lambda b, t: (b, 0, 0, 0)
    per_b3 = lambda b, t: (b, 0, 0)
    hd = (RET_HEADS, RET_CHUNK, RET_CHUNK)
    return pl.pallas_call(
        functools.partial(_mixa_prompt_kernel, tm=tm),
        grid=(batch, nt),
        in_specs=[
            pl.BlockSpec(memory_space=pltpu.SMEM),
            pl.BlockSpec((tm, D_MODEL), row_blk),
            _resident((1, D_MODEL)),
            _resident((D_MODEL, in_w)),
            pl.BlockSpec((tm, RET_DK), lambda b, t: (t, 0)),
            pl.BlockSpec((tm, RET_DK), lambda b, t: (t, 0)),
            _resident(hd),
            _resident(hd),
            _resident(hd),
            pl.BlockSpec((1, RET_HEADS, RET_DK, RET_DK), per_b4),
            pl.BlockSpec((1, 1, LRU_WIDTH), per_b3),
            pl.BlockSpec((1, CONV_W - 1, LRU_WIDTH), per_b3),
            _resident((1, RET_WIDTH)),
            _resident((CONV_W, LRU_WIDTH)),
            _resident((1, LRU_WIDTH)),
            _resident((LRU_WIDTH // LANES, LANES, 2 * LANES)),
            _resident((1, LRU_WIDTH)),
            _resident((1, LRU_WIDTH)),
            _resident((1, LRU_WIDTH)),
            _resident((D_MODEL, D_MODEL)),
        ],
        out_specs=[
            pl.BlockSpec((tm, D_MODEL), row_blk),
            pl.BlockSpec((1, RET_HEADS, RET_DK, RET_DK), per_b4),
            pl.BlockSpec((1, 1, LRU_WIDTH), per_b3),
            pl.BlockSpec((1, CONV_W - 1, LRU_WIDTH), per_b3),
        ],
        out_shape=[
            jax.ShapeDtypeStruct((batch * seq, D_MODEL), F32),
            jax.ShapeDtypeStruct((batch, RET_HEADS, RET_DK, RET_DK), F32),
            jax.ShapeDtypeStruct((batch, 1, LRU_WIDTH), F32),
            jax.ShapeDtypeStruct((batch, CONV_W - 1, LRU_WIDTH), F32),
        ],
        scratch_shapes=[
            pltpu.VMEM((tm + 2 * SUBLANES, LRU_WIDTH), F32),
            pltpu.VMEM((tm, D_MODEL), BF16),
            pltpu.VMEM((RET_HEADS, RET_DK, RET_DK), BF16),
        ],
        compiler_params=pltpu.CompilerParams(
            dimension_semantics=("arbitrary", "arbitrary"), vmem_limit_bytes=_vmem_limit(est)),
        name="mixa_prompt",
    )(gdec, x, n1, win, cosf, sinf, dmat, zeta, xi, s0, h0, cb0, gn, cw, cbias, wgate,
      ba, bx, lam, wout)


def _mixa_sample_kernel(gdec_ref, x_ref, n1_ref, win_ref, cos_ref, sin_ref,
                        s0_ref, h0_ref, cb0_ref, gn_ref, cw_ref, cbias_ref, wgate_ref,
                        ba_ref, bx_ref, lam_ref, wout_ref,
                        o_ref, s_ref, hT_ref, cT_ref, ymix_s, *, bt):
    x = x_ref[...]
    hn = _rms(x, n1_ref[...]).astype(BF16)
    cosf = cos_ref[...]
    sinf = sin_ref[...]
    w = RET_WIDTH
    zq = _dot(hn, win_ref[:, 0 * w:1 * w])
    zk = _dot(hn, win_ref[:, 1 * w:2 * w])
    zv = _dot(hn, win_ref[:, 2 * w:3 * w])
    sg = _silu(_dot(hn, win_ref[:, 3 * w:4 * w]))
    xb = _dot(hn, win_ref[:, 4 * w:5 * w])
    gg = _gelu(_dot(hn, win_ref[:, 5 * w:6 * w]))

    rowi = lax.broadcasted_iota(jnp.int32, (bt, RET_DK), 0)
    for h in range(RET_HEADS):
        cols = slice(h * RET_DK, (h + 1) * RET_DK)
        qs = (_rotary(zq[:, cols], cosf, sinf) * Q_SCALE).astype(BF16)
        kb = _rotary(zk[:, cols], cosf, sinf).astype(BF16)
        vh = zv[:, cols]
        qk = jnp.sum(qs.astype(F32) * kb.astype(F32), axis=-1, keepdims=True)
        cross = jnp.zeros((bt, RET_DK), F32)
        for b in range(bt):
            s_old = s0_ref[b, h]
            cr = _dot(qs, s_old.astype(BF16))
            cross = cross + jnp.where(rowi == b, cr, 0.0)
            u = lax.dot_general(kb, jnp.where(rowi == b, vh, 0.0).astype(BF16),
                                (((0,), (0,)), ((), ())), preferred_element_type=F32)
            s_ref[b, h] = gdec_ref[h] * s_old + u
        o = qk * vh + cross * gdec_ref[h]
        ymix_s[:, cols] = (_group_norm(o) * gn_ref[:, cols] * sg[:, cols]).astype(BF16)

    xc = cbias_ref[...]
    for j in range(CONV_W - 1):
        xc = xc + cb0_ref[j] * cw_ref[j:j + 1, :]
    xc = xc + xb * cw_ref[CONV_W - 1:CONV_W, :]
    for j in range(CONV_W - 2):
        cT_ref[j] = cb0_ref[j + 1]
    cT_ref[CONV_W - 2] = xb
    a, b_in = _lru_coeffs(xc, wgate_ref, ba_ref[...], bx_ref[...], lam_ref[...])
    hnew = a * h0_ref[...] + b_in
    hT_ref[...] = hnew
    ymix_s[:, RET_WIDTH:] = (hnew * gg).astype(BF16)
    o_ref[...] = x + _dot(ymix_s[...], wout_ref[...])


def _mixa_sample(x, gdec, n1, win, cosf, sinf, s0, h0, cb0, gn, cw, cbias,
                 wgate, ba, bx, lam, wout, *, bt):
    batch = x.shape[0]
    in_w = 6 * RET_WIDTH
    est = (D_MODEL * (in_w + RET_WIDTH) * 2 + D_MODEL * D_MODEL * 2
           + 4 * bt * RET_HEADS * RET_DK * RET_DK * 4 + 16 * bt * in_w * 4)
    rows = lambda i: (i, 0)
    return pl.pallas_call(
        functools.partial(_mixa_sample_kernel, bt=bt),
        grid=(batch // bt,),
        in_specs=[
            pl.BlockSpec(memory_space=pltpu.SMEM),
            pl.BlockSpec((bt, D_MODEL), rows),
            _resident((1, D_MODEL)),
            _resident((D_MODEL, in_w)),
            _resident((1, RET_DK)),
            _resident((1, RET_DK)),
            pl.BlockSpec((bt, RET_HEADS, RET_DK, RET_DK), lambda i: (i, 0, 0, 0)),
            pl.BlockSpec((bt, LRU_WIDTH), rows),
            pl.BlockSpec((CONV_W - 1, bt, LRU_WIDTH), lambda i: (0, i, 0)),
            _resident((1, RET_WIDTH)),
            _resident((CONV_W, LRU_WIDTH)),
            _resident((1, LRU_WIDTH)),
            _resident((LRU_WIDTH // LANES, LANES, 2 * LANES)),
            _resident((1, LRU_WIDTH)),
            _resident((1, LRU_WIDTH)),
            _resident((1, LRU_WIDTH)),
            _resident((D_MODEL, D_MODEL)),
        ],
        out_specs=[
            pl.BlockSpec((bt, D_MODEL), rows),
            pl.BlockSpec((bt, RET_HEADS, RET_DK, RET_DK), lambda i: (i, 0, 0, 0)),
            pl.BlockSpec((bt, LRU_WIDTH), rows),
            pl.BlockSpec((CONV_W - 1, bt, LRU_WIDTH), lambda i: (0, i, 0)),
        ],
        out_shape=[
            jax.ShapeDtypeStruct((batch, D_MODEL), F32),
            jax.ShapeDtypeStruct((batch, RET_HEADS, RET_DK, RET_DK), F32),
            jax.ShapeDtypeStruct((batch, LRU_WIDTH), F32),
            jax.ShapeDtypeStruct((CONV_W - 1, batch, LRU_WIDTH), F32),
        ],
        scratch_shapes=[pltpu.VMEM((bt, D_MODEL), BF16)],
        compiler_params=pltpu.CompilerParams(
            dimension_semantics=("arbitrary",), vmem_limit_bytes=_vmem_limit(est)),
        name="mixa_sample",
    )(gdec, x, n1, win, cosf, sinf, s0, h0, cb0, gn, cw, cbias, wgate,
      ba, bx, lam, wout)


def _rope_tables(pos):
    half = RET_DK // 2
    inv = ROPE_BASE ** (-jnp.arange(half, dtype=F32) / half)
    ang = pos[:, None] * inv[None, :]
    cos, sin = jnp.cos(ang), jnp.sin(ang)
    return jnp.concatenate([cos, cos], axis=1), jnp.concatenate([-sin, sin], axis=1)


def _decay_tables(c):
    lg = jnp.log1p(-jnp.exp2(-5.0 - jnp.arange(RET_HEADS, dtype=F32)))
    idx = jnp.arange(c, dtype=F32)
    diff = idx[:, None] - idx[None, :]
    causal = diff >= 0
    dmat = jnp.where(causal[None], jnp.exp(jnp.where(causal, diff, 0.0)[None] * lg[:, None, None]), 0.0)
    zeta = jnp.exp((c - 1.0 - idx)[None, :] * lg[:, None])
    xi = jnp.exp((idx + 1.0)[None, :] * lg[:, None])
    gdec = jnp.exp(c * lg)
    return lg, dmat, zeta, xi, gdec


def _gate_tiles(wa, wx):
    z = jnp.zeros((LRU_BW, LRU_BW), wa.dtype)
    bd = lambda w, g: jnp.block([[w[2 * g], z], [z, w[2 * g + 1]]])
    tiles = [jnp.concatenate([bd(wa, g), bd(wx, g)], axis=1) for g in range(LRU_BLOCKS // 2)]
    return jnp.stack(tiles).astype(BF16)


def kernel(x_prompt, x_sample, state_ret, state_lru, state_conv, norm1, norm2, norm_f, w_in_a, ret_gn,
           conv_w, conv_b, lru_wa, lru_ba, lru_wx, lru_bx, lru_lambda, w_out_a, w_in_c, sg_norm_g,
           sg_norm_b, sg_ws, sg_bs, w_out_c, ffn_wg, ffn_wu, ffn_wd):
    bp, lp, _ = x_prompt.shape
    bs = x_sample.shape[0]
    row = lambda v: v.reshape(1, -1)

    win_a = w_in_a[0].astype(BF16)
    wout_a = w_out_a[0].astype(BF16)
    wgate = _gate_tiles(lru_wa[0], lru_wx[0])
    win_c = w_in_c[0].astype(BF16)
    wout_c = w_out_c[0].astype(BF16)
    wg = ffn_wg.astype(BF16)
    wu = ffn_wu.astype(BF16)
    wd = ffn_wd.astype(BF16)
    mixa_w = (row(ret_gn[0]), conv_w[0], row(conv_b[0]), wgate, row(lru_ba[0]),
              row(lru_bx[0]), row(lru_lambda[0]), wout_a)

    tm = 512
    cos_p, sin_p = _rope_tables(jnp.arange(lp, dtype=F32))
    _, dmat, zeta, xi, gdec_p = _decay_tables(RET_CHUNK)
    bc = lambda t: jnp.broadcast_to(t[:, :, None], (RET_HEADS, RET_CHUNK, RET_DK))
    zeros = lambda *s: jnp.zeros(s, F32)
    xp = x_prompt.reshape(bp * lp, D_MODEL)
    xp, ret_p, lru_p, conv_p = _mixa_prompt(
        xp, gdec_p, row(norm1[0]), win_a, cos_p, sin_p, dmat, bc(zeta), bc(xi),
        zeros(bp, RET_HEADS, RET_DK, RET_DK), zeros(bp, 1, LRU_WIDTH),
        zeros(bp, CONV_W - 1, LRU_WIDTH), *mixa_w, batch=bp, seq=lp, tm=tm)
    xp = _ffn(xp, row(norm2[0]), wg, wu, wd, row(norm_f), layer=0, final_norm=False, tm=tm)
    bsb = jnp.broadcast_to(sg_bs[0][:, :, None], (SG_GROUPS, SG_CHUNK, SG_GW))
    xp = _mixc_prompt(xp, row(norm1[1]), win_c, row(sg_norm_g[0]), row(sg_norm_b[0]), sg_ws[0],
                      bsb, wout_c, tm=tm)
    y_prompt = _ffn(xp, row(norm2[1]), wg, wu, wd, row(norm_f), layer=1, final_norm=True, tm=tm)

    bt = 16
    cos_s, sin_s = _rope_tables(PAST_LEN + jnp.arange(1, dtype=F32))
    _, _, _, _, gdec_s = _decay_tables(1)
    xs = x_sample.reshape(bs, D_MODEL)
    xs, ret_s, lru_s, conv_s = _mixa_sample(
        xs, gdec_s, row(norm1[0]), win_a, cos_s, sin_s, state_ret[0],
        state_lru[0], jnp.transpose(state_conv[0], (1, 0, 2)), *mixa_w, bt=bt)
    xs = _ffn(xs, row(norm2[0]), wg, wu, wd, row(norm_f), layer=0, final_norm=False, tm=bs)
    sg_scale = row(jnp.repeat(sg_ws[0][:, 0, 0], SG_GW))
    sg_bias = row(jnp.repeat(sg_bs[0][:, 0], SG_GW))
    xs, v_s = _mixc_sample(xs, row(norm1[1]), win_c, row(sg_norm_g[0]), row(sg_norm_b[0]),
                           sg_scale, sg_bias, wout_c)
    y_sample = _ffn(xs, row(norm2[1]), wg, wu, wd, row(norm_f), layer=1, final_norm=True, tm=bs)

    return (y_prompt.reshape(bp, lp, D_MODEL),
            y_sample.reshape(bs, 1, D_MODEL),
            ret_p[None],
            ret_s[None],
            lru_p.reshape(1, bp, LRU_WIDTH),
            lru_s[None],
            conv_p[None],
            jnp.transpose(conv_s, (1, 0, 2))[None],
            v_s.reshape(1, bs, 1, SG_HALF))
```

```python
import functools
import math

import jax
import jax.numpy as jnp
from jax import lax
from jax.experimental import pallas as pl
from jax.experimental.pallas import tpu as pltpu

F32 = jnp.float32
BF16 = jnp.bfloat16

D_MODEL = 1024
RET_WIDTH = 512
RET_HEADS = 4
RET_DK = 128
RET_CHUNK = 128
ROPE_BASE = 10000.0
LRU_WIDTH = 512
LRU_BLOCKS = 8
LRU_BW = 64
LRU_C = 8.0
CONV_W = 4
SG_CHUNK = 128
SG_HALF = 1024
SG_GROUPS = 8
SG_GW = 128
D_FF = 2816
EPS = 1e-6
PAST_LEN = 16384

V7X_VMEM_BYTES = 64 * 1024 * 1024
SUBLANES = 8
LANES = 128

Q_SCALE = RET_DK ** -0.5
GELU_C = math.sqrt(2.0 / math.pi)


def _vmem_limit(estimate_bytes):
    return int(min(estimate_bytes * 3 // 2 + (8 << 20), V7X_VMEM_BYTES - (6 << 20)))


def _resident(shape):
    nd = len(shape)
    return pl.BlockSpec(shape, lambda *_: (0,) * nd, pipeline_mode=pl.Buffered(1))


def _dot(a, b):
    return lax.dot_general(a, b, (((1,), (0,)), ((), ())), preferred_element_type=F32)


def _rms(x, g):
    ms = jnp.mean(x * x, axis=-1, keepdims=True)
    return x * lax.rsqrt(ms + EPS) * g


def _gelu(x):
    return x * (0.5 * (1.0 + jnp.tanh(GELU_C * (x + 0.044715 * (x * x * x)))))


def _silu(x):
    return x * jax.nn.sigmoid(x)


def _softplus(x):
    return jnp.maximum(x, 0.0) + jnp.log1p(jnp.exp(-jnp.abs(x)))


def _sqrt_nonneg(y):
    return jnp.where(y > 0.0, y * lax.rsqrt(y), 0.0)


def _group_norm(o):
    mu = jnp.mean(o, axis=-1, keepdims=True)
    oc = o - mu
    var = jnp.mean(oc * oc, axis=-1, keepdims=True)
    return oc * lax.rsqrt(var + EPS)


def _rotary(x, cosf, sinf):
    return x * cosf + pltpu.roll(x, RET_DK // 2, 1) * sinf


def _lru_group(xc, wgate, ba, bx, sp):
    both = _dot(xc.astype(BF16), wgate)
    r = jax.nn.sigmoid(both[:, :LANES] + ba)
    i = jax.nn.sigmoid(both[:, LANES:] + bx)
    log_a = (-LRU_C * r) * sp
    a = jnp.exp(log_a)
    mult = _sqrt_nonneg(-jnp.tanh(log_a) * (a * a + 1.0))
    return a, xc * i * mult


def _lru_coeffs(xc, wgate_ref, ba, bx, lam):
    xcb = xc.astype(BF16)
    ra, rx = [], []
    for g in range(LRU_WIDTH // LANES):
        cols = slice(g * LANES, (g + 1) * LANES)
        both = _dot(xcb[:, cols], wgate_ref[g])
        ra.append(both[:, :LANES])
        rx.append(both[:, LANES:])
    r = jax.nn.sigmoid(jnp.concatenate(ra, axis=1) + ba)
    i = jax.nn.sigmoid(jnp.concatenate(rx, axis=1) + bx)
    log_a = (-LRU_C * r) * _softplus(-lam)
    a = jnp.exp(log_a)
    mult = _sqrt_nonneg(-jnp.tanh(log_a) * (a * a + 1.0))
    return a, xc * i * mult


def _ffn_kernel(x_ref, n2_ref, wg_ref, wu_ref, wd_ref, nf_ref, o_ref, *, final_norm):
    x = x_ref[...]
    h = _rms(x, n2_ref[...]).astype(BF16)
    act = (_silu(_dot(h, wg_ref[...])) * _dot(h, wu_ref[...])).astype(BF16)
    y = x + _dot(act, wd_ref[...])
    if final_norm:
        y = _rms(y, nf_ref[...])
    o_ref[...] = y


def _ffn(x, n2, wg, wu, wd, nf, *, layer, final_norm, tm):
    t = x.shape[0]
    est = 3 * D_MODEL * D_FF * 2 + 4 * tm * D_MODEL * 4 + 3 * tm * D_FF * 4
    layer_blk = lambda shape: pl.BlockSpec((None,) + shape, lambda i: (layer, 0, 0),
                                           pipeline_mode=pl.Buffered(1))
    return pl.pallas_call(
        functools.partial(_ffn_kernel, final_norm=final_norm),
        grid=(t // tm,),
        in_specs=[
            pl.BlockSpec((tm, D_MODEL), lambda i: (i, 0)),
            _resident((1, D_MODEL)),
            layer_blk((D_MODEL, D_FF)),
            layer_blk((D_MODEL, D_FF)),
            layer_blk((D_FF, D_MODEL)),
            _resident((1, D_MODEL)),
        ],
        out_specs=pl.BlockSpec((tm, D_MODEL), lambda i: (i, 0)),
        out_shape=jax.ShapeDtypeStruct((t, D_MODEL), F32),
        compiler_params=pltpu.CompilerParams(
            dimension_semantics=("arbitrary",), vmem_limit_bytes=_vmem_limit(est)),
        name="ffn_final" if final_norm else "ffn",
    )(x, n2, wg, wu, wd, nf)


def _sgu_in(x, n1, win_ref, vg, vb):
    hn = _rms(x, n1).astype(BF16)
    u = _gelu(_dot(hn, win_ref[:, :SG_HALF]))
    zv = _gelu(_dot(hn, win_ref[:, SG_HALF:]))
    mu = jnp.mean(zv, axis=-1, keepdims=True)
    vc = zv - mu
    var = jnp.mean(vc * vc, axis=-1, keepdims=True)
    return u, vc * lax.rsqrt(var + EPS) * vg + vb


def _mixc_prompt_kernel(x_ref, n1_ref, win_ref, vg_ref, vb_ref, ws_ref, bsb_ref, wout_ref,
                        o_ref, u_s, v_s, gated_s, *, tm):
    x = x_ref[...]
    u, vn = _sgu_in(x, n1_ref[...], win_ref, vg_ref[...], vb_ref[...])
    u_s[...] = u
    v_s[...] = vn.astype(BF16)
    ri = lax.broadcasted_iota(jnp.int32, (SG_CHUNK, SG_CHUNK), 0)
    ci = lax.broadcasted_iota(jnp.int32, (SG_CHUNK, SG_CHUNK), 1)
    for g in range(SG_GROUPS):
        cols = slice(g * SG_GW, (g + 1) * SG_GW)
        wm = jnp.where(ri >= ci, ws_ref[g], 0.0).astype(BF16)
        nchunk = tm // SG_CHUNK
        vcat = jnp.concatenate(
            [v_s[c * SG_CHUNK:(c + 1) * SG_CHUNK, cols] for c in range(nchunk)], axis=1)
        sv_all = _dot(wm, vcat)
        for c in range(nchunk):
            rows = slice(c * SG_CHUNK, (c + 1) * SG_CHUNK)
            sv = sv_all[:, c * SG_GW:(c + 1) * SG_GW] + bsb_ref[g]
            gated_s[rows, cols] = (u_s[rows, cols] * sv).astype(BF16)
    o_ref[...] = x + _dot(gated_s[...], wout_ref[...])


def _mixc_prompt(x, n1, win, vg, vb, ws, bsb, wout, *, tm):
    t = x.shape[0]
    est = 3 * D_MODEL * SG_HALF * 2 + 4 * tm * D_MODEL * 4 + 4 * tm * 2 * SG_HALF * 4
    return pl.pallas_call(
        functools.partial(_mixc_prompt_kernel, tm=tm),
        grid=(t // tm,),
        in_specs=[
            pl.BlockSpec((tm, D_MODEL), lambda i: (i, 0)),
            _resident((1, D_MODEL)),
            _resident((D_MODEL, 2 * SG_HALF)),
            _resident((1, SG_HALF)),
            _resident((1, SG_HALF)),
            _resident((SG_GROUPS, SG_CHUNK, SG_CHUNK)),
            _resident((SG_GROUPS, SG_CHUNK, SG_GW)),
            _resident((SG_HALF, D_MODEL)),
        ],
        out_specs=pl.BlockSpec((tm, D_MODEL), lambda i: (i, 0)),
        out_shape=jax.ShapeDtypeStruct((t, D_MODEL), F32),
        scratch_shapes=[
            pltpu.VMEM((tm, SG_HALF), F32),
            pltpu.VMEM((tm, SG_HALF), BF16),
            pltpu.VMEM((tm, SG_HALF), BF16),
        ],
        compiler_params=pltpu.CompilerParams(
            dimension_semantics=("arbitrary",), vmem_limit_bytes=_vmem_limit(est)),
        name="mixc_prompt",
    )(x, n1, win, vg, vb, ws, bsb, wout)


def _mixc_sample_kernel(x_ref, n1_ref, win_ref, vg_ref, vb_ref, scale_ref, bias_ref, wout_ref,
                        o_ref, v_ref):
    x = x_ref[...]
    u, vn = _sgu_in(x, n1_ref[...], win_ref, vg_ref[...], vb_ref[...])
    v_ref[...] = vn
    sv = vn * scale_ref[...] + bias_ref[...]
    o_ref[...] = x + _dot((u * sv).astype(BF16), wout_ref[...])


def _mixc_sample(x, n1, win, vg, vb, scale, bias, wout):
    t = x.shape[0]
    est = 3 * D_MODEL * SG_HALF * 2 + 8 * t * 2 * SG_HALF * 4
    return pl.pallas_call(
        _mixc_sample_kernel,
        grid=(1,),
        in_specs=[
            _resident((t, D_MODEL)),
            _resident((1, D_MODEL)),
            _resident((D_MODEL, 2 * SG_HALF)),
            _resident((1, SG_HALF)),
            _resident((1, SG_HALF)),
            _resident((1, SG_HALF)),
            _resident((1, SG_HALF)),
            _resident((SG_HALF, D_MODEL)),
        ],
        out_specs=[
            pl.BlockSpec((t, D_MODEL), lambda i: (0, 0)),
            pl.BlockSpec((t, SG_HALF), lambda i: (0, 0)),
        ],
        out_shape=[
            jax.ShapeDtypeStruct((t, D_MODEL), F32),
            jax.ShapeDtypeStruct((t, SG_HALF), F32),
        ],
        compiler_params=pltpu.CompilerParams(
            dimension_semantics=("arbitrary",), vmem_limit_bytes=_vmem_limit(est)),
        name="mixc_sample",
    )(x, n1, win, vg, vb, scale, bias, wout)


def _lru_scan(a, b, h0):
    tm, w = a.shape
    groups = tm // SUBLANES
    a3 = a.reshape(groups, SUBLANES, w)
    b3 = b.reshape(groups, SUBLANES, w)
    sub = lax.broadcasted_iota(jnp.int32, (groups, SUBLANES, w), 1)
    for s in (1, 2, 4):
        a_sh = pltpu.roll(a3, s, 1)
        b_sh = pltpu.roll(b3, s, 1)
        keep = sub >= s
        b3 = jnp.where(keep, a3 * b_sh + b3, b3)
        a3 = jnp.where(keep, a3 * a_sh, a3)
    hprev = jnp.broadcast_to(h0, (SUBLANES, w))
    hs = []
    for r in range(groups):
        hr = a3[r] * hprev + b3[r]
        hs.append(hr)
        hprev = jnp.broadcast_to(hr[SUBLANES - 1:SUBLANES, :], (SUBLANES, w))
    return jnp.concatenate(hs, axis=0), hprev[0:1, :]


def _mixa_prompt_kernel(gdec_ref, x_ref, xn_ref, n1_ref, win_ref, cos_ref, sin_ref, dmat_ref,
                        zeta_ref, xi_ref, s0_ref, h0_ref, cb0_ref, gn_ref, cw_ref, cbias_ref,
                        wgate_ref, ba_ref, bx_ref, lam_ref, wout_ref,
                        o_ref, s_ref, hT_ref, cT_ref,
                        z_s, xbn_s, xp_s, ymix_s, sb_s, *, tm, nt):
    half = tm // 2
    w = RET_WIDTH
    pad = SUBLANES
    tail = CONV_W - 1
    step = pl.program_id(0)
    ZQ, ZK, ZV, ZG, ZGB = range(5)

    def stash_xb(v):
        xbn_s[...] = v

    def window_xb(v):
        xp_s[pad + half:pad + tm, :] = v

    def projection(src_ref, src_row0, z_row0, put_xb):
        hn = []

        def norm_and_xb():
            hn.append(_rms(src_ref[src_row0:src_row0 + half, :], n1_ref[...]))
            put_xb(_dot(hn[0], win_ref[4]))

        def block(win_col, z_col):
            def run():
                z_s[z_row0:z_row0 + half, z_col * w:(z_col + 1) * w] = _dot(
                    hn[0], win_ref[win_col])
            return run

        return (norm_and_xb, block(5, ZGB), block(0, ZQ), block(1, ZK), block(2, ZV),
                block(3, ZG))

    @pl.when(step == 0)
    def _():
        for piece in projection(x_ref, 0, 0, stash_xb):
            piece()

    @pl.when(lax.rem(step, nt) == 0)
    def _():
        s_ref[...] = s0_ref[...]
        hT_ref[...] = h0_ref[...]
        cT_ref[...] = cb0_ref[...]

    xp_s[pad - tail:pad, :] = cT_ref[0]
    xp_s[pad:pad + half, :] = xbn_s[...]
    for h in range(RET_HEADS):
        sb_s[h] = s_ref[0, h].astype(BF16)
    ngroup = LRU_WIDTH // LANES
    lanes = [slice(g * LANES, (g + 1) * LANES) for g in range(ngroup)]
    hprev = [hT_ref[0, :, lanes[g]] for g in range(ngroup)]
    sp = _softplus(-lam_ref[...])

    def unit(c, h):
        rows = slice(c * RET_CHUNK, (c + 1) * RET_CHUNK)
        ln = lanes[h]
        zcol = lambda blk: slice(blk * w + h * LANES, blk * w + (h + 1) * LANES)
        xc = cbias_ref[:, ln]
        for j in range(CONV_W):
            off = pad - tail + j + c * RET_CHUNK
            xc = xc + xp_s[off:off + RET_CHUNK, ln] * cw_ref[j:j + 1, ln]
        a, b_in = _lru_group(xc, wgate_ref[h], ba_ref[:, ln], bx_ref[:, ln], sp[:, ln])
        hl, hprev[h] = _lru_scan(a, b_in, hprev[h])
        ymix_s[rows, RET_WIDTH + h * LANES:RET_WIDTH + (h + 1) * LANES] = (
            hl * _gelu(z_s[rows, zcol(ZGB)]))
        cols = slice(h * RET_DK, (h + 1) * RET_DK)
        cosf = cos_ref[rows, :]
        sinf = sin_ref[rows, :]
        qr = _rotary(z_s[rows, zcol(ZQ)], cosf, sinf) * Q_SCALE
        kr = _rotary(z_s[rows, zcol(ZK)], cosf, sinf)
        vb = z_s[rows, zcol(ZV)].astype(BF16)
        sc = lax.dot_general(qr.astype(BF16), kr.astype(BF16), (((1,), (1,)), ((), ())),
                             preferred_element_type=F32) * dmat_ref[h]
        o = _dot(jnp.concatenate([sc.astype(BF16), (qr * xi_ref[h]).astype(BF16)], axis=1),
                 jnp.concatenate([vb, sb_s[h]], axis=0))
        ya = _group_norm(o) * gn_ref[:, cols] * _silu(z_s[rows, zcol(ZG)])
        ymix_s[rows, cols] = ya
        u = lax.dot_general((kr * zeta_ref[h]).astype(BF16), vb, (((0,), (0,)), ((), ())),
                            preferred_element_type=F32)
        s_new = gdec_ref[h] * s_ref[0, h] + u
        s_ref[0, h] = s_new
        sb_s[h] = s_new.astype(BF16)

    def out_proj(row0):
        rows = slice(row0, row0 + half)
        o_ref[rows, :] = x_ref[rows, :] + _dot(ymix_s[rows, :], wout_ref[...])

    units_per_half = (half // RET_CHUNK) * RET_HEADS

    def run_half(first_chunk, pieces):
        for n in range(units_per_half):
            if n < len(pieces):
                pieces[n]()
            unit(first_chunk + n // RET_HEADS, n % RET_HEADS)

    run_half(0, projection(x_ref, half, half, window_xb))
    run_half(half // RET_CHUNK,
             (lambda: out_proj(0),) + projection(xn_ref, 0, 0, stash_xb))
    out_proj(half)

    for g in range(ngroup):
        hT_ref[0, :, lanes[g]] = hprev[g]
    cT_ref[0] = xp_s[pad + tm - tail:pad + tm, :]


def _mixa_prompt(x, gdec, n1, win, cosf, sinf, dmat, zeta, xi, s0, h0, cb0, gn, cw, cbias,
                 wgate, ba, bx, lam, wout, *, batch, seq, tm):
    nt = seq // tm
    steps = batch * nt
    half = tm // 2
    in_w = 6 * RET_WIDTH
    est = (D_MODEL * in_w * 4 + D_MODEL * D_MODEL * 4 + 6 * tm * D_MODEL * 4
           + tm * 5 * RET_WIDTH * 4 + 4 * tm * LRU_WIDTH * 4)
    row_blk = lambda i: (i, 0)
    next_half = lambda i: (jnp.minimum(2 * (i + 1), 2 * steps - 2), 0)
    pos_blk = lambda i: (lax.rem(i, nt), 0)
    per_b4 = lambda i: (i // nt, 0, 0, 0)
    per_b3 = lambda i: (i // nt, 0, 0)
    hd = (RET_HEADS, RET_CHUNK, RET_CHUNK)
    return pl.pallas_call(
        functools.partial(_mixa_prompt_kernel, tm=tm, nt=nt),
        grid=(steps,),
        in_specs=[
            pl.BlockSpec(memory_space=pltpu.SMEM),
            pl.BlockSpec((tm, D_MODEL), row_blk),
            pl.BlockSpec((half, D_MODEL), next_half),
            _resident((1, D_MODEL)),
            _resident((in_w // RET_WIDTH, D_MODEL, RET_WIDTH)),
            pl.BlockSpec((tm, RET_DK), pos_blk),
            pl.BlockSpec((tm, RET_DK), pos_blk),
            _resident(hd),
            _resident(hd),
            _resident(hd),
            pl.BlockSpec((1, RET_HEADS, RET_DK, RET_DK), per_b4),
            pl.BlockSpec((1, 1, LRU_WIDTH), per_b3),
            pl.BlockSpec((1, CONV_W - 1, LRU_WIDTH), per_b3),
            _resident((1, RET_WIDTH)),
            _resident((CONV_W, LRU_WIDTH)),
            _resident((1, LRU_WIDTH)),
            _resident((LRU_WIDTH // LANES, LANES, 2 * LANES)),
            _resident((1, LRU_WIDTH)),
            _resident((1, LRU_WIDTH)),
            _resident((1, LRU_WIDTH)),
            _resident((D_MODEL, D_MODEL)),
        ],
        out_specs=[
            pl.BlockSpec((tm, D_MODEL), row_blk),
            pl.BlockSpec((1, RET_HEADS, RET_DK, RET_DK), per_b4),
            pl.BlockSpec((1, 1, LRU_WIDTH), per_b3),
            pl.BlockSpec((1, CONV_W - 1, LRU_WIDTH), per_b3),
        ],
        out_shape=[
            jax.ShapeDtypeStruct((batch * seq, D_MODEL), F32),
            jax.ShapeDtypeStruct((batch, RET_HEADS, RET_DK, RET_DK), F32),
            jax.ShapeDtypeStruct((batch, 1, LRU_WIDTH), F32),
            jax.ShapeDtypeStruct((batch, CONV_W - 1, LRU_WIDTH), F32),
        ],
        scratch_shapes=[
            pltpu.VMEM((tm, 5 * RET_WIDTH), F32),
            pltpu.VMEM((half, LRU_WIDTH), F32),
            pltpu.VMEM((tm + 2 * SUBLANES, LRU_WIDTH), F32),
            pltpu.VMEM((tm, D_MODEL), F32),
            pltpu.VMEM((RET_HEADS, RET_DK, RET_DK), BF16),
        ],
        compiler_params=pltpu.CompilerParams(
            dimension_semantics=("arbitrary",), vmem_limit_bytes=_vmem_limit(est)),
        name="mixa_prompt",
    )(gdec, x, x, n1, win, cosf, sinf, dmat, zeta, xi, s0, h0, cb0, gn, cw, cbias, wgate,
      ba, bx, lam, wout)


def _mixa_sample_kernel(gdec_ref, x_ref, n1_ref, win_ref, cos_ref, sin_ref,
                        s0_ref, h0_ref, cb0_ref, gn_ref, cw_ref, cbias_ref, wgate_ref,
                        ba_ref, bx_ref, lam_ref, wout_ref,
                        o_ref, s_ref, hT_ref, cT_ref, ymix_s, *, bt):
    x = x_ref[...]
    hn = _rms(x, n1_ref[...])
    cosf = cos_ref[...]
    sinf = sin_ref[...]
    w = RET_WIDTH
    zq = _dot(hn, win_ref[0])
    zk = _dot(hn, win_ref[1])
    zv = _dot(hn, win_ref[2])
    sg = _silu(_dot(hn, win_ref[3]))
    xb = _dot(hn, win_ref[4])
    gg = _gelu(_dot(hn, win_ref[5]))

    rowi = lax.broadcasted_iota(jnp.int32, (bt, RET_DK), 0)
    for h in range(RET_HEADS):
        cols = slice(h * RET_DK, (h + 1) * RET_DK)
        qs = (_rotary(zq[:, cols], cosf, sinf) * Q_SCALE).astype(BF16)
        kb = _rotary(zk[:, cols], cosf, sinf).astype(BF16)
        vh = zv[:, cols]
        qk = jnp.sum(qs.astype(F32) * kb.astype(F32), axis=-1, keepdims=True)
        cross = jnp.zeros((bt, RET_DK), F32)
        for b in range(bt):
            s_old = s0_ref[b, h]
            cr = _dot(qs, s_old.astype(BF16))
            cross = cross + jnp.where(rowi == b, cr, 0.0)
            u = lax.dot_general(kb, jnp.where(rowi == b, vh, 0.0).astype(BF16),
                                (((0,), (0,)), ((), ())), preferred_element_type=F32)
            s_ref[b, h] = gdec_ref[h] * s_old + u
        o = qk * vh + cross * gdec_ref[h]
        ymix_s[:, cols] = _group_norm(o) * gn_ref[:, cols] * sg[:, cols]

    xc = cbias_ref[...]
    for j in range(CONV_W - 1):
        xc = xc + cb0_ref[j] * cw_ref[j:j + 1, :]
    xc = xc + xb * cw_ref[CONV_W - 1:CONV_W, :]
    for j in range(CONV_W - 2):
        cT_ref[j] = cb0_ref[j + 1]
    cT_ref[CONV_W - 2] = xb
    a, b_in = _lru_coeffs(xc, wgate_ref, ba_ref[...], bx_ref[...], lam_ref[...])
    hnew = a * h0_ref[...] + b_in
    hT_ref[...] = hnew
    ymix_s[:, RET_WIDTH:] = hnew * gg
    o_ref[...] = x + _dot(ymix_s[...], wout_ref[...])


def _mixa_sample(x, gdec, n1, win, cosf, sinf, s0, h0, cb0, gn, cw, cbias,
                 wgate, ba, bx, lam, wout, *, bt):
    batch = x.shape[0]
    in_w = 6 * RET_WIDTH
    est = (D_MODEL * in_w * 4 + D_MODEL * D_MODEL * 4
           + 4 * bt * RET_HEADS * RET_DK * RET_DK * 4 + 16 * bt * in_w * 4)
    rows = lambda i: (i, 0)
    return pl.pallas_call(
        functools.partial(_mixa_sample_kernel, bt=bt),
        grid=(batch // bt,),
        in_specs=[
            pl.BlockSpec(memory_space=pltpu.SMEM),
            pl.BlockSpec((bt, D_MODEL), rows),
            _resident((1, D_MODEL)),
            _resident((in_w // RET_WIDTH, D_MODEL, RET_WIDTH)),
            _resident((1, RET_DK)),
            _resident((1, RET_DK)),
            pl.BlockSpec((bt, RET_HEADS, RET_DK, RET_DK), lambda i: (i, 0, 0, 0)),
            pl.BlockSpec((bt, LRU_WIDTH), rows),
            pl.BlockSpec((CONV_W - 1, bt, LRU_WIDTH), lambda i: (0, i, 0)),
            _resident((1, RET_WIDTH)),
            _resident((CONV_W, LRU_WIDTH)),
            _resident((1, LRU_WIDTH)),
            _resident((LRU_WIDTH // LANES, LANES, 2 * LANES)),
            _resident((1, LRU_WIDTH)),
            _resident((1, LRU_WIDTH)),
            _resident((1, LRU_WIDTH)),
            _resident((D_MODEL, D_MODEL)),
        ],
        out_specs=[
            pl.BlockSpec((bt, D_MODEL), rows),
            pl.BlockSpec((bt, RET_HEADS, RET_DK, RET_DK), lambda i: (i, 0, 0, 0)),
            pl.BlockSpec((bt, LRU_WIDTH), rows),
            pl.BlockSpec((CONV_W - 1, bt, LRU_WIDTH), lambda i: (0, i, 0)),
        ],
        out_shape=[
            jax.ShapeDtypeStruct((batch, D_MODEL), F32),
            jax.ShapeDtypeStruct((batch, RET_HEADS, RET_DK, RET_DK), F32),
            jax.ShapeDtypeStruct((batch, LRU_WIDTH), F32),
            jax.ShapeDtypeStruct((CONV_W - 1, batch, LRU_WIDTH), F32),
        ],
        scratch_shapes=[pltpu.VMEM((bt, D_MODEL), F32)],
        compiler_params=pltpu.CompilerParams(
            dimension_semantics=("arbitrary",), vmem_limit_bytes=_vmem_limit(est)),
        name="mixa_sample",
    )(gdec, x, n1, win, cosf, sinf, s0, h0, cb0, gn, cw, cbias, wgate,
      ba, bx, lam, wout)


def _rope_tables(pos):
    half = RET_DK // 2
    inv = ROPE_BASE ** (-jnp.arange(half, dtype=F32) / half)
    ang = pos[:, None] * inv[None, :]
    cos, sin = jnp.cos(ang), jnp.sin(ang)
    return jnp.concatenate([cos, cos], axis=1), jnp.concatenate([-sin, sin], axis=1)


def _decay_tables(c):
    lg = jnp.log1p(-jnp.exp2(-5.0 - jnp.arange(RET_HEADS, dtype=F32)))
    idx = jnp.arange(c, dtype=F32)
    diff = idx[:, None] - idx[None, :]
    causal = diff >= 0
    dmat = jnp.where(causal[None], jnp.exp(jnp.where(causal, diff, 0.0)[None] * lg[:, None, None]), 0.0)
    zeta = jnp.exp((c - 1.0 - idx)[None, :] * lg[:, None])
    xi = jnp.exp((idx + 1.0)[None, :] * lg[:, None])
    gdec = jnp.exp(c * lg)
    return lg, dmat, zeta, xi, gdec


def _gate_tiles(wa, wx):
    z = jnp.zeros((LRU_BW, LRU_BW), wa.dtype)
    bd = lambda w, g: jnp.block([[w[2 * g], z], [z, w[2 * g + 1]]])
    tiles = [jnp.concatenate([bd(wa, g), bd(wx, g)], axis=1) for g in range(LRU_BLOCKS // 2)]
    return jnp.stack(tiles).astype(BF16)


def kernel(x_prompt, x_sample, state_ret, state_lru, state_conv, norm1, norm2, norm_f, w_in_a, ret_gn,
           conv_w, conv_b, lru_wa, lru_ba, lru_wx, lru_bx, lru_lambda, w_out_a, w_in_c, sg_norm_g,
           sg_norm_b, sg_ws, sg_bs, w_out_c, ffn_wg, ffn_wu, ffn_wd):
    bp, lp, _ = x_prompt.shape
    bs = x_sample.shape[0]
    row = lambda v: v.reshape(1, -1)

    win_a = jnp.transpose(w_in_a[0].reshape(D_MODEL, -1, RET_WIDTH), (1, 0, 2)).astype(BF16)
    wout_a = w_out_a[0].astype(BF16)
    wgate = _gate_tiles(lru_wa[0], lru_wx[0])
    win_c = w_in_c[0].astype(BF16)
    wout_c = w_out_c[0].astype(BF16)
    wg = ffn_wg.astype(BF16)
    wu = ffn_wu.astype(BF16)
    wd = ffn_wd.astype(BF16)
    mixa_w = (row(ret_gn[0]), conv_w[0], row(conv_b[0]), wgate, row(lru_ba[0]),
              row(lru_bx[0]), row(lru_lambda[0]), wout_a)

    tm = 512
    cos_p, sin_p = _rope_tables(jnp.arange(lp, dtype=F32))
    _, dmat, zeta, xi, gdec_p = _decay_tables(RET_CHUNK)
    bc = lambda t: jnp.broadcast_to(t[:, :, None], (RET_HEADS, RET_CHUNK, RET_DK))
    zeros = lambda *s: jnp.zeros(s, F32)
    xp = x_prompt.reshape(bp * lp, D_MODEL)
    xp, ret_p, lru_p, conv_p = _mixa_prompt(
        xp, gdec_p, row(norm1[0]), win_a, cos_p, sin_p, dmat, bc(zeta), bc(xi),
        zeros(bp, RET_HEADS, RET_DK, RET_DK), zeros(bp, 1, LRU_WIDTH),
        zeros(bp, CONV_W - 1, LRU_WIDTH), *mixa_w, batch=bp, seq=lp, tm=tm)
    xp = _ffn(xp, row(norm2[0]), wg, wu, wd, row(norm_f), layer=0, final_norm=False, tm=tm)
    bsb = jnp.broadcast_to(sg_bs[0][:, :, None], (SG_GROUPS, SG_CHUNK, SG_GW))
    xp = _mixc_prompt(xp, row(norm1[1]), win_c, row(sg_norm_g[0]), row(sg_norm_b[0]), sg_ws[0],
                      bsb, wout_c, tm=tm)
    y_prompt = _ffn(xp, row(norm2[1]), wg, wu, wd, row(norm_f), layer=1, final_norm=True, tm=tm)

    bt = 16
    cos_s, sin_s = _rope_tables(PAST_LEN + jnp.arange(1, dtype=F32))
    _, _, _, _, gdec_s = _decay_tables(1)
    xs = x_sample.reshape(bs, D_MODEL)
    xs, ret_s, lru_s, conv_s = _mixa_sample(
        xs, gdec_s, row(norm1[0]), win_a, cos_s, sin_s, state_ret[0],
        state_lru[0], jnp.transpose(state_conv[0], (1, 0, 2)), *mixa_w, bt=bt)
    xs = _ffn(xs, row(norm2[0]), wg, wu, wd, row(norm_f), layer=0, final_norm=False, tm=bs)
    sg_scale = row(jnp.repeat(sg_ws[0][:, 0, 0], SG_GW))
    sg_bias = row(jnp.repeat(sg_bs[0][:, 0], SG_GW))
    xs, v_s = _mixc_sample(xs, row(norm1[1]), win_c, row(sg_norm_g[0]), row(sg_norm_b[0]),
                           sg_scale, sg_bias, wout_c)
    y_sample = _ffn(xs, row(norm2[1]), wg, wu, wd, row(norm_f), layer=1, final_norm=True, tm=bs)

    return (y_prompt.reshape(bp, lp, D_MODEL),
            y_sample.reshape(bs, 1, D_MODEL),
            ret_p[None],
            ret_s[None],
            lru_p.reshape(1, bp, LRU_WIDTH),
            lru_s[None],
            conv_p[None],
            jnp.transpose(conv_s, (1, 0, 2))[None],
            v_s.reshape(1, bs, 1, SG_HALF))
```

```python
import functools
import math

import jax
import jax.numpy as jnp
from jax import lax
from jax.experimental import pallas as pl
from jax.experimental.pallas import tpu as pltpu

F32 = jnp.float32
BF16 = jnp.bfloat16

D_MODEL = 1024
RET_WIDTH = 512
RET_HEADS = 4
RET_DK = 128
RET_CHUNK = 128
ROPE_BASE = 10000.0
LRU_WIDTH = 512
LRU_BLOCKS = 8
LRU_BW = 64
LRU_C = 8.0
CONV_W = 4
SG_CHUNK = 128
SG_HALF = 1024
SG_GROUPS = 8
SG_GW = 128
D_FF = 2816
EPS = 1e-6
PAST_LEN = 16384

V7X_VMEM_BYTES = 64 * 1024 * 1024
SUBLANES = 8
LANES = 128

Q_SCALE = RET_DK ** -0.5
GELU_C = math.sqrt(2.0 / math.pi)


def _vmem_limit(estimate_bytes):
    return int(min(estimate_bytes * 3 // 2 + (8 << 20), V7X_VMEM_BYTES - (6 << 20)))


def _resident(shape):
    nd = len(shape)
    return pl.BlockSpec(shape, lambda *_: (0,) * nd, pipeline_mode=pl.Buffered(1))


def _dot(a, b):
    return lax.dot_general(a, b, (((1,), (0,)), ((), ())), preferred_element_type=F32)


def _rms(x, g):
    ms = jnp.mean(x * x, axis=-1, keepdims=True)
    return x * lax.rsqrt(ms + EPS) * g


def _gelu(x):
    return x * (0.5 * (1.0 + jnp.tanh(GELU_C * (x + 0.044715 * (x * x * x)))))


def _silu(x):
    return x * jax.nn.sigmoid(x)


def _softplus(x):
    return jnp.maximum(x, 0.0) + jnp.log1p(jnp.exp(-jnp.abs(x)))


def _sqrt_nonneg(y):
    return jnp.where(y > 0.0, y * lax.rsqrt(y), 0.0)


def _group_norm(o):
    mu = jnp.mean(o, axis=-1, keepdims=True)
    oc = o - mu
    var = jnp.mean(oc * oc, axis=-1, keepdims=True)
    return oc * lax.rsqrt(var + EPS)


def _rotary(x, cosf, sinf):
    return x * cosf + pltpu.roll(x, RET_DK // 2, 1) * sinf


def _lru_group(xc, wgate, ba, bx, sp):
    both = _dot(xc.astype(BF16), wgate)
    r = jax.nn.sigmoid(both[:, :LANES] + ba)
    i = jax.nn.sigmoid(both[:, LANES:] + bx)
    log_a = (-LRU_C * r) * sp
    a = jnp.exp(log_a)
    mult = _sqrt_nonneg(-jnp.tanh(log_a) * (a * a + 1.0))
    return a, xc * i * mult


def _lru_coeffs(xc, wgate_ref, ba, bx, lam):
    xcb = xc.astype(BF16)
    ra, rx = [], []
    for g in range(LRU_WIDTH // LANES):
        cols = slice(g * LANES, (g + 1) * LANES)
        both = _dot(xcb[:, cols], wgate_ref[g])
        ra.append(both[:, :LANES])
        rx.append(both[:, LANES:])
    r = jax.nn.sigmoid(jnp.concatenate(ra, axis=1) + ba)
    i = jax.nn.sigmoid(jnp.concatenate(rx, axis=1) + bx)
    log_a = (-LRU_C * r) * _softplus(-lam)
    a = jnp.exp(log_a)
    mult = _sqrt_nonneg(-jnp.tanh(log_a) * (a * a + 1.0))
    return a, xc * i * mult


def _ffn_rows(x, n2, wg_ref, wu_ref, wd_ref, nf, final_norm):
    h = _rms(x, n2).astype(BF16)
    act = (_silu(_dot(h, wg_ref[...])) * _dot(h, wu_ref[...])).astype(BF16)
    y = x + _dot(act, wd_ref[...])
    return _rms(y, nf) if final_norm else y


def _ffn_kernel(x_ref, xs_ref, n2_ref, wg_ref, wu_ref, wd_ref, nf_ref, o_ref, os_ref, *,
                final_norm, prompt_steps):
    step = pl.program_id(0)
    args = (n2_ref[...], wg_ref, wu_ref, wd_ref, nf_ref[...], final_norm)

    @pl.when(step < prompt_steps)
    def _():
        o_ref[...] = _ffn_rows(x_ref[...], *args)

    @pl.when(step == prompt_steps)
    def _():
        os_ref[...] = _ffn_rows(xs_ref[...], *args)


def _ffn(x, xs, n2, wg, wu, wd, nf, *, layer, final_norm, tm):
    t, ts = x.shape[0], xs.shape[0]
    steps = t // tm
    est = (3 * D_MODEL * D_FF * 2 + 4 * tm * D_MODEL * 4 + 4 * ts * D_MODEL * 4
           + 3 * tm * D_FF * 4)
    layer_blk = lambda shape: pl.BlockSpec((None,) + shape, lambda i: (layer, 0, 0),
                                           pipeline_mode=pl.Buffered(1))
    prompt_blk = pl.BlockSpec((tm, D_MODEL), lambda i: (jnp.minimum(i, steps - 1), 0))
    return pl.pallas_call(
        functools.partial(_ffn_kernel, final_norm=final_norm, prompt_steps=steps),
        grid=(steps + 1,),
        in_specs=[
            prompt_blk,
            _resident((ts, D_MODEL)),
            _resident((1, D_MODEL)),
            layer_blk((D_MODEL, D_FF)),
            layer_blk((D_MODEL, D_FF)),
            layer_blk((D_FF, D_MODEL)),
            _resident((1, D_MODEL)),
        ],
        out_specs=[prompt_blk, pl.BlockSpec((ts, D_MODEL), lambda i: (0, 0))],
        out_shape=[jax.ShapeDtypeStruct((t, D_MODEL), F32),
                   jax.ShapeDtypeStruct((ts, D_MODEL), F32)],
        compiler_params=pltpu.CompilerParams(
            dimension_semantics=("arbitrary",), vmem_limit_bytes=_vmem_limit(est)),
        name="ffn_final" if final_norm else "ffn",
    )(x, xs, n2, wg, wu, wd, nf)


def _sgu_u(hn, win_ref):
    return _gelu(_dot(hn, win_ref[0]))


def _sgu_v(hn, win_ref, vg, vb):
    zv = _gelu(_dot(hn, win_ref[1]))
    mu = jnp.mean(zv, axis=-1, keepdims=True)
    vc = zv - mu
    var = jnp.mean(vc * vc, axis=-1, keepdims=True)
    return vc * lax.rsqrt(var + EPS) * vg + vb


def _mixc_prompt_kernel(x_ref, n1_ref, win_ref, vg_ref, vb_ref, ws_ref, bsb_ref, wout_ref,
                        o_ref, u_s, v_s, gated_s, wm_s, *, tm):
    half = tm // 2
    nchunk = half // SG_CHUNK
    ri = lax.broadcasted_iota(jnp.int32, (SG_CHUNK, SG_CHUNK), 0)
    ci = lax.broadcasted_iota(jnp.int32, (SG_CHUNK, SG_CHUNK), 1)
    for g in range(SG_GROUPS):
        wm_s[g] = jnp.where(ri >= ci, ws_ref[g], 0.0).astype(BF16)

    def norm(r0):
        return _rms(x_ref[r0:r0 + half, :], n1_ref[...])

    def put_u(r0, hn):
        u_s[r0:r0 + half, :] = _sgu_u(hn, win_ref)

    def put_v(r0, hn):
        v_s[r0:r0 + half, :] = _sgu_v(hn, win_ref, vg_ref[...], vb_ref[...]).astype(BF16)

    def gate(r0):
        for g in range(SG_GROUPS):
            cols = slice(g * SG_GW, (g + 1) * SG_GW)
            vcat = jnp.concatenate(
                [v_s[r0 + c * SG_CHUNK:r0 + (c + 1) * SG_CHUNK, cols] for c in range(nchunk)],
                axis=1)
            sv_all = _dot(wm_s[g], vcat)
            for c in range(nchunk):
                rows = slice(r0 + c * SG_CHUNK, r0 + (c + 1) * SG_CHUNK)
                sv = sv_all[:, c * SG_GW:(c + 1) * SG_GW] + bsb_ref[g]
                gated_s[rows, cols] = u_s[rows, cols] * sv

    def out_proj(r0):
        rows = slice(r0, r0 + half)
        o_ref[rows, :] = x_ref[rows, :] + _dot(gated_s[rows, :], wout_ref[...])

    hn_a = norm(0)
    put_u(0, hn_a)
    put_v(0, hn_a)
    hn_b = norm(half)
    put_u(half, hn_b)
    gate(0)
    put_v(half, hn_b)
    out_proj(0)
    gate(half)
    out_proj(half)


def _mixc_prompt(x, n1, win, vg, vb, ws, bsb, wout, *, tm):
    t = x.shape[0]
    est = 3 * D_MODEL * SG_HALF * 2 + 4 * tm * D_MODEL * 4 + 5 * tm * 2 * SG_HALF * 4
    return pl.pallas_call(
        functools.partial(_mixc_prompt_kernel, tm=tm),
        grid=(t // tm,),
        in_specs=[
            pl.BlockSpec((tm, D_MODEL), lambda i: (i, 0)),
            _resident((1, D_MODEL)),
            _resident((2, D_MODEL, SG_HALF)),
            _resident((1, SG_HALF)),
            _resident((1, SG_HALF)),
            _resident((SG_GROUPS, SG_CHUNK, SG_CHUNK)),
            _resident((SG_GROUPS, SG_CHUNK, SG_GW)),
            _resident((SG_HALF, D_MODEL)),
        ],
        out_specs=pl.BlockSpec((tm, D_MODEL), lambda i: (i, 0)),
        out_shape=jax.ShapeDtypeStruct((t, D_MODEL), F32),
        scratch_shapes=[
            pltpu.VMEM((tm, SG_HALF), F32),
            pltpu.VMEM((tm, SG_HALF), BF16),
            pltpu.VMEM((tm, SG_HALF), F32),
            pltpu.VMEM((SG_GROUPS, SG_CHUNK, SG_CHUNK), BF16),
        ],
        compiler_params=pltpu.CompilerParams(
            dimension_semantics=("arbitrary",), vmem_limit_bytes=_vmem_limit(est)),
        name="mixc_prompt",
    )(x, n1, win, vg, vb, ws, bsb, wout)


def _mixc_sample_kernel(x_ref, n1_ref, win_ref, vg_ref, vb_ref, scale_ref, bias_ref, wout_ref,
                        o_ref, v_ref):
    x = x_ref[...]
    hn = _rms(x, n1_ref[...])
    u = _sgu_u(hn, win_ref)
    vn = _sgu_v(hn, win_ref, vg_ref[...], vb_ref[...])
    v_ref[...] = vn
    sv = vn * scale_ref[...] + bias_ref[...]
    o_ref[...] = x + _dot(u * sv, wout_ref[...])


def _mixc_sample(x, n1, win, vg, vb, scale, bias, wout):
    t = x.shape[0]
    est = 3 * D_MODEL * SG_HALF * 2 + 8 * t * 2 * SG_HALF * 4
    return pl.pallas_call(
        _mixc_sample_kernel,
        grid=(1,),
        in_specs=[
            _resident((t, D_MODEL)),
            _resident((1, D_MODEL)),
            _resident((2, D_MODEL, SG_HALF)),
            _resident((1, SG_HALF)),
            _resident((1, SG_HALF)),
            _resident((1, SG_HALF)),
            _resident((1, SG_HALF)),
            _resident((SG_HALF, D_MODEL)),
        ],
        out_specs=[
            pl.BlockSpec((t, D_MODEL), lambda i: (0, 0)),
            pl.BlockSpec((t, SG_HALF), lambda i: (0, 0)),
        ],
        out_shape=[
            jax.ShapeDtypeStruct((t, D_MODEL), F32),
            jax.ShapeDtypeStruct((t, SG_HALF), F32),
        ],
        compiler_params=pltpu.CompilerParams(
            dimension_semantics=("arbitrary",), vmem_limit_bytes=_vmem_limit(est)),
        name="mixc_sample",
    )(x, n1, win, vg, vb, scale, bias, wout)


def _lru_scan(a, b, h0):
    tm, w = a.shape
    groups = tm // SUBLANES
    a3 = a.reshape(groups, SUBLANES, w)
    b3 = b.reshape(groups, SUBLANES, w)
    sub = lax.broadcasted_iota(jnp.int32, (groups, SUBLANES, w), 1)
    for s in (1, 2, 4):
        a_sh = pltpu.roll(a3, s, 1)
        b_sh = pltpu.roll(b3, s, 1)
        keep = sub >= s
        b3 = jnp.where(keep, a3 * b_sh + b3, b3)
        a3 = jnp.where(keep, a3 * a_sh, a3)
    hprev = jnp.broadcast_to(h0, (SUBLANES, w))
    hs = []
    for r in range(groups):
        hr = a3[r] * hprev + b3[r]
        hs.append(hr)
        hprev = jnp.broadcast_to(hr[SUBLANES - 1:SUBLANES, :], (SUBLANES, w))
    return jnp.concatenate(hs, axis=0), hprev[0:1, :]


def _mixa_prompt_kernel(gdec_ref, x_ref, xn_ref, n1_ref, win_ref, cos_ref, sin_ref, dmat_ref,
                        zeta_ref, xi_ref, s0_ref, h0_ref, cb0_ref, gn_ref, cw_ref, cbias_ref,
                        wgate_ref, ba_ref, bx_ref, lam_ref, wout_ref,
                        o_ref, s_ref, hT_ref, cT_ref,
                        z_s, xbn_s, xp_s, ymix_s, sb_s, *, tm, nt):
    half = tm // 2
    w = RET_WIDTH
    pad = SUBLANES
    tail = CONV_W - 1
    step = pl.program_id(0)
    ZQ, ZK, ZV, ZG, ZGB = range(5)

    def stash_xb(v):
        xbn_s[...] = v

    def window_xb(v):
        xp_s[pad + half:pad + tm, :] = v

    def projection(src_ref, src_row0, z_row0, put_xb):
        hn = []

        def norm_and_xb():
            hn.append(_rms(src_ref[src_row0:src_row0 + half, :], n1_ref[...]))
            put_xb(_dot(hn[0], win_ref[4]))

        def block(win_col, z_col):
            def run():
                z_s[z_row0:z_row0 + half, z_col * w:(z_col + 1) * w] = _dot(
                    hn[0], win_ref[win_col])
            return run

        return (norm_and_xb, block(5, ZGB), block(0, ZQ), block(1, ZK), block(2, ZV),
                block(3, ZG))

    @pl.when(step == 0)
    def _():
        for piece in projection(x_ref, 0, 0, stash_xb):
            piece()

    @pl.when(lax.rem(step, nt) == 0)
    def _():
        s_ref[...] = s0_ref[...]
        hT_ref[...] = h0_ref[...]
        cT_ref[...] = cb0_ref[...]

    xp_s[pad - tail:pad, :] = cT_ref[0]
    xp_s[pad:pad + half, :] = xbn_s[...]
    for h in range(RET_HEADS):
        sb_s[h] = s_ref[0, h].astype(BF16)
    ngroup = LRU_WIDTH // LANES
    lanes = [slice(g * LANES, (g + 1) * LANES) for g in range(ngroup)]
    hprev = [hT_ref[0, :, lanes[g]] for g in range(ngroup)]
    sp = _softplus(-lam_ref[...])

    def unit(c, h):
        rows = slice(c * RET_CHUNK, (c + 1) * RET_CHUNK)
        ln = lanes[h]
        zcol = lambda blk: slice(blk * w + h * LANES, blk * w + (h + 1) * LANES)
        xc = cbias_ref[:, ln]
        for j in range(CONV_W):
            off = pad - tail + j + c * RET_CHUNK
            xc = xc + xp_s[off:off + RET_CHUNK, ln] * cw_ref[j:j + 1, ln]
        a, b_in = _lru_group(xc, wgate_ref[h], ba_ref[:, ln], bx_ref[:, ln], sp[:, ln])
        hl, hprev[h] = _lru_scan(a, b_in, hprev[h])
        ymix_s[rows, RET_WIDTH + h * LANES:RET_WIDTH + (h + 1) * LANES] = (
            hl * _gelu(z_s[rows, zcol(ZGB)]))
        cols = slice(h * RET_DK, (h + 1) * RET_DK)
        cosf = cos_ref[rows, :]
        sinf = sin_ref[rows, :]
        qr = _rotary(z_s[rows, zcol(ZQ)], cosf, sinf) * Q_SCALE
        kr = _rotary(z_s[rows, zcol(ZK)], cosf, sinf)
        vb = z_s[rows, zcol(ZV)].astype(BF16)
        sc = lax.dot_general(qr.astype(BF16), kr.astype(BF16), (((1,), (1,)), ((), ())),
                             preferred_element_type=F32) * dmat_ref[h]
        o = _dot(jnp.concatenate([sc.astype(BF16), (qr * xi_ref[h]).astype(BF16)], axis=1),
                 jnp.concatenate([vb, sb_s[h]], axis=0))
        ya = _group_norm(o) * gn_ref[:, cols] * _silu(z_s[rows, zcol(ZG)])
        ymix_s[rows, cols] = ya
        u = lax.dot_general((kr * zeta_ref[h]).astype(BF16), vb, (((0,), (0,)), ((), ())),
                            preferred_element_type=F32)
        s_new = gdec_ref[h] * s_ref[0, h] + u
        s_ref[0, h] = s_new
        sb_s[h] = s_new.astype(BF16)

    def out_proj(row0):
        rows = slice(row0, row0 + half)
        o_ref[rows, :] = x_ref[rows, :] + _dot(ymix_s[rows, :], wout_ref[...])

    units_per_half = (half // RET_CHUNK) * RET_HEADS

    def run_half(first_chunk, pieces):
        for n in range(units_per_half):
            if n < len(pieces):
                pieces[n]()
            unit(first_chunk + n // RET_HEADS, n % RET_HEADS)

    run_half(0, projection(x_ref, half, half, window_xb))
    run_half(half // RET_CHUNK,
             (lambda: out_proj(0),) + projection(xn_ref, 0, 0, stash_xb))
    out_proj(half)

    for g in range(ngroup):
        hT_ref[0, :, lanes[g]] = hprev[g]
    cT_ref[0] = xp_s[pad + tm - tail:pad + tm, :]


def _mixa_prompt(x, gdec, n1, win, cosf, sinf, dmat, zeta, xi, s0, h0, cb0, gn, cw, cbias,
                 wgate, ba, bx, lam, wout, *, batch, seq, tm):
    nt = seq // tm
    steps = batch * nt
    half = tm // 2
    in_w = 6 * RET_WIDTH
    est = (D_MODEL * in_w * 4 + D_MODEL * D_MODEL * 4 + 6 * tm * D_MODEL * 4
           + tm * 5 * RET_WIDTH * 4 + 4 * tm * LRU_WIDTH * 4)
    row_blk = lambda i: (i, 0)
    next_half = lambda i: (jnp.minimum(2 * (i + 1), 2 * steps - 2), 0)
    pos_blk = lambda i: (lax.rem(i, nt), 0)
    per_b4 = lambda i: (i // nt, 0, 0, 0)
    per_b3 = lambda i: (i // nt, 0, 0)
    hd = (RET_HEADS, RET_CHUNK, RET_CHUNK)
    return pl.pallas_call(
        functools.partial(_mixa_prompt_kernel, tm=tm, nt=nt),
        grid=(steps,),
        in_specs=[
            pl.BlockSpec(memory_space=pltpu.SMEM),
            pl.BlockSpec((tm, D_MODEL), row_blk),
            pl.BlockSpec((half, D_MODEL), next_half),
            _resident((1, D_MODEL)),
            _resident((in_w // RET_WIDTH, D_MODEL, RET_WIDTH)),
            pl.BlockSpec((tm, RET_DK), pos_blk),
            pl.BlockSpec((tm, RET_DK), pos_blk),
            _resident(hd),
            _resident(hd),
            _resident(hd),
            pl.BlockSpec((1, RET_HEADS, RET_DK, RET_DK), per_b4),
            pl.BlockSpec((1, 1, LRU_WIDTH), per_b3),
            pl.BlockSpec((1, CONV_W - 1, LRU_WIDTH), per_b3),
            _resident((1, RET_WIDTH)),
            _resident((CONV_W, LRU_WIDTH)),
            _resident((1, LRU_WIDTH)),
            _resident((LRU_WIDTH // LANES, LANES, 2 * LANES)),
            _resident((1, LRU_WIDTH)),
            _resident((1, LRU_WIDTH)),
            _resident((1, LRU_WIDTH)),
            _resident((D_MODEL, D_MODEL)),
        ],
        out_specs=[
            pl.BlockSpec((tm, D_MODEL), row_blk),
            pl.BlockSpec((1, RET_HEADS, RET_DK, RET_DK), per_b4),
            pl.BlockSpec((1, 1, LRU_WIDTH), per_b3),
            pl.BlockSpec((1, CONV_W - 1, LRU_WIDTH), per_b3),
        ],
        out_shape=[
            jax.ShapeDtypeStruct((batch * seq, D_MODEL), F32),
            jax.ShapeDtypeStruct((batch, RET_HEADS, RET_DK, RET_DK), F32),
            jax.ShapeDtypeStruct((batch, 1, LRU_WIDTH), F32),
            jax.ShapeDtypeStruct((batch, CONV_W - 1, LRU_WIDTH), F32),
        ],
        scratch_shapes=[
            pltpu.VMEM((tm, 5 * RET_WIDTH), F32),
            pltpu.VMEM((half, LRU_WIDTH), F32),
            pltpu.VMEM((tm + 2 * SUBLANES, LRU_WIDTH), F32),
            pltpu.VMEM((tm, D_MODEL), F32),
            pltpu.VMEM((RET_HEADS, RET_DK, RET_DK), BF16),
        ],
        compiler_params=pltpu.CompilerParams(
            dimension_semantics=("arbitrary",), vmem_limit_bytes=_vmem_limit(est)),
        name="mixa_prompt",
    )(gdec, x, x, n1, win, cosf, sinf, dmat, zeta, xi, s0, h0, cb0, gn, cw, cbias, wgate,
      ba, bx, lam, wout)


def _mixa_sample_kernel(gdec_ref, x_ref, n1_ref, win_ref, cos_ref, sin_ref,
                        s0_ref, h0_ref, cb0_ref, gn_ref, cw_ref, cbias_ref, wgate_ref,
                        ba_ref, bx_ref, lam_ref, wout_ref,
                        o_ref, s_ref, hT_ref, cT_ref, ymix_s, *, bt):
    x = x_ref[...]
    hn = _rms(x, n1_ref[...])
    cosf = cos_ref[...]
    sinf = sin_ref[...]
    w = RET_WIDTH
    zq = _dot(hn, win_ref[0])
    zk = _dot(hn, win_ref[1])
    zv = _dot(hn, win_ref[2])
    sg = _silu(_dot(hn, win_ref[3]))
    xb = _dot(hn, win_ref[4])
    gg = _gelu(_dot(hn, win_ref[5]))

    rowi = lax.broadcasted_iota(jnp.int32, (bt, RET_DK), 0)
    for h in range(RET_HEADS):
        cols = slice(h * RET_DK, (h + 1) * RET_DK)
        qs = (_rotary(zq[:, cols], cosf, sinf) * Q_SCALE).astype(BF16)
        kb = _rotary(zk[:, cols], cosf, sinf).astype(BF16)
        vh = zv[:, cols]
        qk = jnp.sum(qs.astype(F32) * kb.astype(F32), axis=-1, keepdims=True)
        cross = jnp.zeros((bt, RET_DK), F32)
        for b in range(bt):
            s_old = s0_ref[b, h]
            cr = _dot(qs, s_old.astype(BF16))
            cross = cross + jnp.where(rowi == b, cr, 0.0)
            u = lax.dot_general(kb, jnp.where(rowi == b, vh, 0.0).astype(BF16),
                                (((0,), (0,)), ((), ())), preferred_element_type=F32)
            s_ref[b, h] = gdec_ref[h] * s_old + u
        o = qk * vh + cross * gdec_ref[h]
        ymix_s[:, cols] = _group_norm(o) * gn_ref[:, cols] * sg[:, cols]

    xc = cbias_ref[...]
    for j in range(CONV_W - 1):
        xc = xc + cb0_ref[j] * cw_ref[j:j + 1, :]
    xc = xc + xb * cw_ref[CONV_W - 1:CONV_W, :]
    for j in range(CONV_W - 2):
        cT_ref[j] = cb0_ref[j + 1]
    cT_ref[CONV_W - 2] = xb
    a, b_in = _lru_coeffs(xc, wgate_ref, ba_ref[...], bx_ref[...], lam_ref[...])
    hnew = a * h0_ref[...] + b_in
    hT_ref[...] = hnew
    ymix_s[:, RET_WIDTH:] = hnew * gg
    o_ref[...] = x + _dot(ymix_s[...], wout_ref[...])


def _mixa_sample(x, gdec, n1, win, cosf, sinf, s0, h0, cb0, gn, cw, cbias,
                 wgate, ba, bx, lam, wout, *, bt):
    batch = x.shape[0]
    in_w = 6 * RET_WIDTH
    est = (D_MODEL * in_w * 4 + D_MODEL * D_MODEL * 4
           + 4 * bt * RET_HEADS * RET_DK * RET_DK * 4 + 16 * bt * in_w * 4)
    rows = lambda i: (i, 0)
    return pl.pallas_call(
        functools.partial(_mixa_sample_kernel, bt=bt),
        grid=(batch // bt,),
        in_specs=[
            pl.BlockSpec(memory_space=pltpu.SMEM),
            pl.BlockSpec((bt, D_MODEL), rows),
            _resident((1, D_MODEL)),
            _resident((in_w // RET_WIDTH, D_MODEL, RET_WIDTH)),
            _resident((1, RET_DK)),
            _resident((1, RET_DK)),
            pl.BlockSpec((bt, RET_HEADS, RET_DK, RET_DK), lambda i: (i, 0, 0, 0)),
            pl.BlockSpec((bt, LRU_WIDTH), rows),
            pl.BlockSpec((CONV_W - 1, bt, LRU_WIDTH), lambda i: (0, i, 0)),
            _resident((1, RET_WIDTH)),
            _resident((CONV_W, LRU_WIDTH)),
            _resident((1, LRU_WIDTH)),
            _resident((LRU_WIDTH // LANES, LANES, 2 * LANES)),
            _resident((1, LRU_WIDTH)),
            _resident((1, LRU_WIDTH)),
            _resident((1, LRU_WIDTH)),
            _resident((D_MODEL, D_MODEL)),
        ],
        out_specs=[
            pl.BlockSpec((bt, D_MODEL), rows),
            pl.BlockSpec((bt, RET_HEADS, RET_DK, RET_DK), lambda i: (i, 0, 0, 0)),
            pl.BlockSpec((bt, LRU_WIDTH), rows),
            pl.BlockSpec((CONV_W - 1, bt, LRU_WIDTH), lambda i: (0, i, 0)),
        ],
        out_shape=[
            jax.ShapeDtypeStruct((batch, D_MODEL), F32),
            jax.ShapeDtypeStruct((batch, RET_HEADS, RET_DK, RET_DK), F32),
            jax.ShapeDtypeStruct((batch, LRU_WIDTH), F32),
            jax.ShapeDtypeStruct((CONV_W - 1, batch, LRU_WIDTH), F32),
        ],
        scratch_shapes=[pltpu.VMEM((bt, D_MODEL), F32)],
        compiler_params=pltpu.CompilerParams(
            dimension_semantics=("arbitrary",), vmem_limit_bytes=_vmem_limit(est)),
        name="mixa_sample",
    )(gdec, x, n1, win, cosf, sinf, s0, h0, cb0, gn, cw, cbias, wgate,
      ba, bx, lam, wout)


def _rope_tables(pos):
    half = RET_DK // 2
    inv = ROPE_BASE ** (-jnp.arange(half, dtype=F32) / half)
    ang = pos[:, None] * inv[None, :]
    cos, sin = jnp.cos(ang), jnp.sin(ang)
    return jnp.concatenate([cos, cos], axis=1), jnp.concatenate([-sin, sin], axis=1)


def _decay_tables(c):
    lg = jnp.log1p(-jnp.exp2(-5.0 - jnp.arange(RET_HEADS, dtype=F32)))
    idx = jnp.arange(c, dtype=F32)
    diff = idx[:, None] - idx[None, :]
    causal = diff >= 0
    dmat = jnp.where(causal[None], jnp.exp(jnp.where(causal, diff, 0.0)[None] * lg[:, None, None]), 0.0)
    zeta = jnp.exp((c - 1.0 - idx)[None, :] * lg[:, None])
    xi = jnp.exp((idx + 1.0)[None, :] * lg[:, None])
    gdec = jnp.exp(c * lg)
    return lg, dmat, zeta, xi, gdec


def _gate_tiles(wa, wx):
    z = jnp.zeros((LRU_BW, LRU_BW), wa.dtype)
    bd = lambda w, g: jnp.block([[w[2 * g], z], [z, w[2 * g + 1]]])
    tiles = [jnp.concatenate([bd(wa, g), bd(wx, g)], axis=1) for g in range(LRU_BLOCKS // 2)]
    return jnp.stack(tiles).astype(BF16)


def kernel(x_prompt, x_sample, state_ret, state_lru, state_conv, norm1, norm2, norm_f, w_in_a, ret_gn,
           conv_w, conv_b, lru_wa, lru_ba, lru_wx, lru_bx, lru_lambda, w_out_a, w_in_c, sg_norm_g,
           sg_norm_b, sg_ws, sg_bs, w_out_c, ffn_wg, ffn_wu, ffn_wd):
    bp, lp, _ = x_prompt.shape
    bs = x_sample.shape[0]
    row = lambda v: v.reshape(1, -1)

    win_a = jnp.stack([w_in_a[0][:, i * RET_WIDTH:(i + 1) * RET_WIDTH].astype(BF16)
                       for i in range(6)])
    wout_a = w_out_a[0].astype(BF16)
    wgate = _gate_tiles(lru_wa[0], lru_wx[0])
    win_c = jnp.stack([w_in_c[0][:, i * SG_HALF:(i + 1) * SG_HALF].astype(BF16)
                       for i in range(2)])
    wout_c = w_out_c[0].astype(BF16)
    mixa_w = (row(ret_gn[0]), conv_w[0], row(conv_b[0]), wgate, row(lru_ba[0]),
              row(lru_bx[0]), row(lru_lambda[0]), wout_a)
    wg = ffn_wg.astype(BF16)
    wu = ffn_wu.astype(BF16)
    wd = ffn_wd.astype(BF16)
    ffn = lambda xp_, xs_, layer, final: _ffn(
        xp_, xs_, row(norm2[layer]), wg, wu, wd, row(norm_f), layer=layer,
        final_norm=final, tm=tm)

    tm = 512
    bt = 16
    cos_p, sin_p = _rope_tables(jnp.arange(lp, dtype=F32))
    cos_s, sin_s = _rope_tables(PAST_LEN + jnp.arange(1, dtype=F32))
    _, dmat, zeta, xi, gdec_p = _decay_tables(RET_CHUNK)
    _, _, _, _, gdec_s = _decay_tables(1)
    bc = lambda t: jnp.broadcast_to(t[:, :, None], (RET_HEADS, RET_CHUNK, RET_DK))
    zeros = lambda *s: jnp.zeros(s, F32)
    xp = x_prompt.reshape(bp * lp, D_MODEL)
    xs = x_sample.reshape(bs, D_MODEL)

    xp, ret_p, lru_p, conv_p = _mixa_prompt(
        xp, gdec_p, row(norm1[0]), win_a, cos_p, sin_p, dmat, bc(zeta), bc(xi),
        zeros(bp, RET_HEADS, RET_DK, RET_DK), zeros(bp, 1, LRU_WIDTH),
        zeros(bp, CONV_W - 1, LRU_WIDTH), *mixa_w, batch=bp, seq=lp, tm=tm)
    xs, ret_s, lru_s, conv_s = _mixa_sample(
        xs, gdec_s, row(norm1[0]), win_a, cos_s, sin_s, state_ret[0],
        state_lru[0], jnp.transpose(state_conv[0], (1, 0, 2)), *mixa_w, bt=bt)
    xp, xs = ffn(xp, xs, 0, False)

    bsb = jnp.broadcast_to(sg_bs[0][:, :, None], (SG_GROUPS, SG_CHUNK, SG_GW))
    xp = _mixc_prompt(xp, row(norm1[1]), win_c, row(sg_norm_g[0]), row(sg_norm_b[0]), sg_ws[0],
                      bsb, wout_c, tm=tm)
    sg_scale = row(jnp.repeat(sg_ws[0][:, 0, 0], SG_GW))
    sg_bias = row(jnp.repeat(sg_bs[0][:, 0], SG_GW))
    xs, v_s = _mixc_sample(xs, row(norm1[1]), win_c, row(sg_norm_g[0]), row(sg_norm_b[0]),
                           sg_scale, sg_bias, wout_c)
    y_prompt, y_sample = ffn(xp, xs, 1, True)

    return (y_prompt.reshape(bp, lp, D_MODEL),
            y_sample.reshape(bs, 1, D_MODEL),
            ret_p[None],
            ret_s[None],
            lru_p.reshape(1, bp, LRU_WIDTH),
            lru_s[None],
            conv_p[None],
            jnp.transpose(conv_s, (1, 0, 2))[None],
            v_s.reshape(1, bs, 1, SG_HALF))
```

```python
import functools
import math

import jax
import jax.numpy as jnp
from jax import lax
from jax.experimental import pallas as pl
from jax.experimental.pallas import tpu as pltpu

F32 = jnp.float32
BF16 = jnp.bfloat16

D_MODEL = 1024
RET_WIDTH = 512
RET_HEADS = 4
RET_DK = 128
RET_CHUNK = 128
ROPE_BASE = 10000.0
LRU_WIDTH = 512
LRU_BLOCKS = 8
LRU_BW = 64
LRU_C = 8.0
CONV_W = 4
SG_CHUNK = 128
SG_HALF = 1024
SG_GROUPS = 8
SG_GW = 128
D_FF = 2816
EPS = 1e-6
PAST_LEN = 16384

V7X_VMEM_BYTES = 64 * 1024 * 1024
SUBLANES = 8
LANES = 128

Q_SCALE = RET_DK ** -0.5
GELU_C = math.sqrt(2.0 / math.pi)


def _vmem_limit(estimate_bytes):
    return int(min(estimate_bytes * 3 // 2 + (8 << 20), V7X_VMEM_BYTES - (6 << 20)))


def _resident(shape):
    nd = len(shape)
    return pl.BlockSpec(shape, lambda *_: (0,) * nd, pipeline_mode=pl.Buffered(1))


def _dot(a, b):
    return lax.dot_general(a, b, (((1,), (0,)), ((), ())), preferred_element_type=F32)


def _rms(x, g):
    ms = jnp.mean(x * x, axis=-1, keepdims=True)
    return x * lax.rsqrt(ms + EPS) * g


def _gelu(x):
    return x * (0.5 * (1.0 + jnp.tanh(GELU_C * (x + 0.044715 * (x * x * x)))))


def _silu(x):
    return x * jax.nn.sigmoid(x)


def _softplus(x):
    return jnp.maximum(x, 0.0) + jnp.log1p(jnp.exp(-jnp.abs(x)))


def _sqrt_nonneg(y):
    return jnp.where(y > 0.0, y * lax.rsqrt(y), 0.0)


def _group_norm(o):
    mu = jnp.mean(o, axis=-1, keepdims=True)
    oc = o - mu
    var = jnp.mean(oc * oc, axis=-1, keepdims=True)
    return oc * lax.rsqrt(var + EPS)


def _rotary(x, cosf, sinf):
    return x * cosf + pltpu.roll(x, RET_DK // 2, 1) * sinf


def _lru_group(xc, wgate, ba, bx, sp):
    both = _dot(xc.astype(BF16), wgate)
    r = jax.nn.sigmoid(both[:, :LANES] + ba)
    i = jax.nn.sigmoid(both[:, LANES:] + bx)
    log_a = (-LRU_C * r) * sp
    a = jnp.exp(log_a)
    mult = _sqrt_nonneg(-jnp.tanh(log_a) * (a * a + 1.0))
    return a, xc * i * mult


def _lru_coeffs(xc, wgate_ref, ba, bx, lam):
    xcb = xc.astype(BF16)
    ra, rx = [], []
    for g in range(LRU_WIDTH // LANES):
        cols = slice(g * LANES, (g + 1) * LANES)
        both = _dot(xcb[:, cols], wgate_ref[g])
        ra.append(both[:, :LANES])
        rx.append(both[:, LANES:])
    r = jax.nn.sigmoid(jnp.concatenate(ra, axis=1) + ba)
    i = jax.nn.sigmoid(jnp.concatenate(rx, axis=1) + bx)
    log_a = (-LRU_C * r) * _softplus(-lam)
    a = jnp.exp(log_a)
    mult = _sqrt_nonneg(-jnp.tanh(log_a) * (a * a + 1.0))
    return a, xc * i * mult


def _ffn_rows(x, n2, wg_ref, wu_ref, wd_ref, nf, final_norm):
    h = _rms(x, n2).astype(BF16)
    act = (_silu(_dot(h, wg_ref[...])) * _dot(h, wu_ref[...])).astype(BF16)
    y = x + _dot(act, wd_ref[...])
    return _rms(y, nf) if final_norm else y


def _ffn_kernel(x_ref, xs_ref, n2_ref, wg_ref, wu_ref, wd_ref, nf_ref, o_ref, os_ref, *,
                final_norm, prompt_steps):
    step = pl.program_id(0)
    args = (n2_ref[...], wg_ref, wu_ref, wd_ref, nf_ref[...], final_norm)

    @pl.when(step < prompt_steps)
    def _():
        o_ref[...] = _ffn_rows(x_ref[...], *args)

    @pl.when(step == prompt_steps)
    def _():
        os_ref[...] = _ffn_rows(xs_ref[...], *args)


def _ffn(x, xs, n2, wg, wu, wd, nf, *, layer, final_norm, tm):
    t, ts = x.shape[0], xs.shape[0]
    steps = t // tm
    est = (3 * D_MODEL * D_FF * 2 + 4 * tm * D_MODEL * 4 + 4 * ts * D_MODEL * 4
           + 3 * tm * D_FF * 4)
    layer_blk = lambda shape: pl.BlockSpec((None,) + shape, lambda i: (layer, 0, 0),
                                           pipeline_mode=pl.Buffered(1))
    prompt_blk = pl.BlockSpec((tm, D_MODEL), lambda i: (jnp.minimum(i, steps - 1), 0))
    return pl.pallas_call(
        functools.partial(_ffn_kernel, final_norm=final_norm, prompt_steps=steps),
        grid=(steps + 1,),
        in_specs=[
            prompt_blk,
            _resident((ts, D_MODEL)),
            _resident((1, D_MODEL)),
            layer_blk((D_MODEL, D_FF)),
            layer_blk((D_MODEL, D_FF)),
            layer_blk((D_FF, D_MODEL)),
            _resident((1, D_MODEL)),
        ],
        out_specs=[prompt_blk, pl.BlockSpec((ts, D_MODEL), lambda i: (0, 0))],
        out_shape=[jax.ShapeDtypeStruct((t, D_MODEL), F32),
                   jax.ShapeDtypeStruct((ts, D_MODEL), F32)],
        compiler_params=pltpu.CompilerParams(
            dimension_semantics=("arbitrary",), vmem_limit_bytes=_vmem_limit(est)),
        name="ffn_final" if final_norm else "ffn",
    )(x, xs, n2, wg, wu, wd, nf)


def _sgu_u(hn, win_ref):
    return _gelu(_dot(hn, win_ref[0]))


def _sgu_v(hn, win_ref, vg, vb):
    zv = _gelu(_dot(hn, win_ref[1]))
    mu = jnp.mean(zv, axis=-1, keepdims=True)
    vc = zv - mu
    var = jnp.mean(vc * vc, axis=-1, keepdims=True)
    return vc * lax.rsqrt(var + EPS) * vg + vb


def _mixc_prompt_kernel(x_ref, n1_ref, win_ref, vg_ref, vb_ref, ws_ref, bsb_ref, wout_ref,
                        o_ref, u_s, v_s, gated_s, wm_s, *, tm):
    half = tm // 2
    nchunk = half // SG_CHUNK
    ri = lax.broadcasted_iota(jnp.int32, (SG_CHUNK, SG_CHUNK), 0)
    ci = lax.broadcasted_iota(jnp.int32, (SG_CHUNK, SG_CHUNK), 1)
    for g in range(SG_GROUPS):
        wm_s[g] = jnp.where(ri >= ci, ws_ref[g], 0.0).astype(BF16)

    def norm(r0):
        return _rms(x_ref[r0:r0 + half, :], n1_ref[...])

    def put_u(r0, hn):
        u_s[r0:r0 + half, :] = _sgu_u(hn, win_ref)

    def put_v(r0, hn):
        v_s[r0:r0 + half, :] = _sgu_v(hn, win_ref, vg_ref[...], vb_ref[...]).astype(BF16)

    def gate(r0):
        for g in range(SG_GROUPS):
            cols = slice(g * SG_GW, (g + 1) * SG_GW)
            vcat = jnp.concatenate(
                [v_s[r0 + c * SG_CHUNK:r0 + (c + 1) * SG_CHUNK, cols] for c in range(nchunk)],
                axis=1)
            sv_all = _dot(wm_s[g], vcat)
            for c in range(nchunk):
                rows = slice(r0 + c * SG_CHUNK, r0 + (c + 1) * SG_CHUNK)
                sv = sv_all[:, c * SG_GW:(c + 1) * SG_GW] + bsb_ref[g]
                gated_s[rows, cols] = u_s[rows, cols] * sv

    def out_proj(r0):
        rows = slice(r0, r0 + half)
        o_ref[rows, :] = x_ref[rows, :] + _dot(gated_s[rows, :], wout_ref[...])

    hn_a = norm(0)
    put_u(0, hn_a)
    put_v(0, hn_a)
    hn_b = norm(half)
    put_u(half, hn_b)
    gate(0)
    put_v(half, hn_b)
    out_proj(0)
    gate(half)
    out_proj(half)


def _mixc_prompt(x, n1, win, vg, vb, ws, bsb, wout, *, tm):
    t = x.shape[0]
    est = 3 * D_MODEL * SG_HALF * 2 + 4 * tm * D_MODEL * 4 + 5 * tm * 2 * SG_HALF * 4
    return pl.pallas_call(
        functools.partial(_mixc_prompt_kernel, tm=tm),
        grid=(t // tm,),
        in_specs=[
            pl.BlockSpec((tm, D_MODEL), lambda i: (i, 0)),
            _resident((1, D_MODEL)),
            _resident((2, D_MODEL, SG_HALF)),
            _resident((1, SG_HALF)),
            _resident((1, SG_HALF)),
            _resident((SG_GROUPS, SG_CHUNK, SG_CHUNK)),
            _resident((SG_GROUPS, SG_CHUNK, SG_GW)),
            _resident((SG_HALF, D_MODEL)),
        ],
        out_specs=pl.BlockSpec((tm, D_MODEL), lambda i: (i, 0)),
        out_shape=jax.ShapeDtypeStruct((t, D_MODEL), F32),
        scratch_shapes=[
            pltpu.VMEM((tm, SG_HALF), F32),
            pltpu.VMEM((tm, SG_HALF), BF16),
            pltpu.VMEM((tm, SG_HALF), F32),
            pltpu.VMEM((SG_GROUPS, SG_CHUNK, SG_CHUNK), BF16),
        ],
        compiler_params=pltpu.CompilerParams(
            dimension_semantics=("arbitrary",), vmem_limit_bytes=_vmem_limit(est)),
        name="mixc_prompt",
    )(x, n1, win, vg, vb, ws, bsb, wout)


def _mixc_sample_kernel(x_ref, n1_ref, win_ref, vg_ref, vb_ref, scale_ref, bias_ref, wout_ref,
                        o_ref, v_ref):
    x = x_ref[...]
    hn = _rms(x, n1_ref[...])
    u = _sgu_u(hn, win_ref)
    vn = _sgu_v(hn, win_ref, vg_ref[...], vb_ref[...])
    v_ref[...] = vn
    sv = vn * scale_ref[...] + bias_ref[...]
    o_ref[...] = x + _dot(u * sv, wout_ref[...])


def _mixc_sample(x, n1, win, vg, vb, scale, bias, wout):
    t = x.shape[0]
    est = 3 * D_MODEL * SG_HALF * 2 + 8 * t * 2 * SG_HALF * 4
    return pl.pallas_call(
        _mixc_sample_kernel,
        grid=(1,),
        in_specs=[
            _resident((t, D_MODEL)),
            _resident((1, D_MODEL)),
            _resident((2, D_MODEL, SG_HALF)),
            _resident((1, SG_HALF)),
            _resident((1, SG_HALF)),
            _resident((1, SG_HALF)),
            _resident((1, SG_HALF)),
            _resident((SG_HALF, D_MODEL)),
        ],
        out_specs=[
            pl.BlockSpec((t, D_MODEL), lambda i: (0, 0)),
            pl.BlockSpec((t, SG_HALF), lambda i: (0, 0)),
        ],
        out_shape=[
            jax.ShapeDtypeStruct((t, D_MODEL), F32),
            jax.ShapeDtypeStruct((t, SG_HALF), F32),
        ],
        compiler_params=pltpu.CompilerParams(
            dimension_semantics=("arbitrary",), vmem_limit_bytes=_vmem_limit(est)),
        name="mixc_sample",
    )(x, n1, win, vg, vb, scale, bias, wout)


def _lru_scan(a, b, h0):
    tm, w = a.shape
    groups = tm // SUBLANES
    a3 = a.reshape(groups, SUBLANES, w)
    b3 = b.reshape(groups, SUBLANES, w)
    sub = lax.broadcasted_iota(jnp.int32, (groups, SUBLANES, w), 1)
    for s in (1, 2, 4):
        a_sh = pltpu.roll(a3, s, 1)
        b_sh = pltpu.roll(b3, s, 1)
        keep = sub >= s
        b3 = jnp.where(keep, a3 * b_sh + b3, b3)
        a3 = jnp.where(keep, a3 * a_sh, a3)
    hprev = jnp.broadcast_to(h0, (SUBLANES, w))
    hs = []
    for r in range(groups):
        hr = a3[r] * hprev + b3[r]
        hs.append(hr)
        hprev = jnp.broadcast_to(hr[SUBLANES - 1:SUBLANES, :], (SUBLANES, w))
    return jnp.concatenate(hs, axis=0), hprev[0:1, :]


def _mixa_prompt_kernel(gdec_ref, x_ref, xn_ref, n1_ref, win_ref, cos_ref, sin_ref, dmat_ref,
                        zeta_ref, xi_ref, s0_ref, h0_ref, cb0_ref, gn_ref, cw_ref, cbias_ref,
                        wgate_ref, ba_ref, bx_ref, lam_ref, wout_ref,
                        wg_ref, wu_ref, wd_ref, winc_ref, woutc_ref,
                        o_ref, s_ref, hT_ref, cT_ref,
                        wg_o, wu_o, wd_o, winc_o, woutc_o,
                        z_s, xbn_s, xp_s, ymix_s, sb_s, *, tm, nt):
    half = tm // 2
    w = RET_WIDTH
    pad = SUBLANES
    tail = CONV_W - 1
    step = pl.program_id(0)
    ZQ, ZK, ZV, ZG, ZGB = range(5)

    def stash_xb(v):
        xbn_s[...] = v

    def window_xb(v):
        xp_s[pad + half:pad + tm, :] = v

    def projection(src_ref, src_row0, z_row0, put_xb):
        hn = []

        def norm_and_xb():
            hn.append(_rms(src_ref[src_row0:src_row0 + half, :], n1_ref[...]))
            put_xb(_dot(hn[0], win_ref[4]))

        def block(win_col, z_col):
            def run():
                z_s[z_row0:z_row0 + half, z_col * w:(z_col + 1) * w] = _dot(
                    hn[0], win_ref[win_col])
            return run

        return (norm_and_xb, block(5, ZGB), block(0, ZQ), block(1, ZK), block(2, ZV),
                block(3, ZG))

    @pl.when(step == 0)
    def _():
        for piece in projection(x_ref, 0, 0, stash_xb):
            piece()

    @pl.when(lax.rem(step, nt) == 0)
    def _():
        s_ref[...] = s0_ref[...]
        hT_ref[...] = h0_ref[...]
        cT_ref[...] = cb0_ref[...]

    xp_s[pad - tail:pad, :] = cT_ref[0]
    xp_s[pad:pad + half, :] = xbn_s[...]
    for h in range(RET_HEADS):
        sb_s[h] = s_ref[0, h].astype(BF16)
    ngroup = LRU_WIDTH // LANES
    lanes = [slice(g * LANES, (g + 1) * LANES) for g in range(ngroup)]
    hprev = [hT_ref[0, :, lanes[g]] for g in range(ngroup)]
    sp = _softplus(-lam_ref[...])

    def unit(c, h):
        rows = slice(c * RET_CHUNK, (c + 1) * RET_CHUNK)
        ln = lanes[h]
        zcol = lambda blk: slice(blk * w + h * LANES, blk * w + (h + 1) * LANES)
        xc = cbias_ref[:, ln]
        for j in range(CONV_W):
            off = pad - tail + j + c * RET_CHUNK
            xc = xc + xp_s[off:off + RET_CHUNK, ln] * cw_ref[j:j + 1, ln]
        a, b_in = _lru_group(xc, wgate_ref[h], ba_ref[:, ln], bx_ref[:, ln], sp[:, ln])
        hl, hprev[h] = _lru_scan(a, b_in, hprev[h])
        ymix_s[rows, RET_WIDTH + h * LANES:RET_WIDTH + (h + 1) * LANES] = (
            hl * _gelu(z_s[rows, zcol(ZGB)]))
        cols = slice(h * RET_DK, (h + 1) * RET_DK)
        cosf = cos_ref[rows, :]
        sinf = sin_ref[rows, :]
        qr = _rotary(z_s[rows, zcol(ZQ)], cosf, sinf) * Q_SCALE
        kr = _rotary(z_s[rows, zcol(ZK)], cosf, sinf)
        vb = z_s[rows, zcol(ZV)].astype(BF16)
        sc = lax.dot_general(qr.astype(BF16), kr.astype(BF16), (((1,), (1,)), ((), ())),
                             preferred_element_type=F32) * dmat_ref[h]
        o = _dot(jnp.concatenate([sc.astype(BF16), (qr * xi_ref[h]).astype(BF16)], axis=1),
                 jnp.concatenate([vb, sb_s[h]], axis=0))
        ya = _group_norm(o) * gn_ref[:, cols] * _silu(z_s[rows, zcol(ZG)])
        ymix_s[rows, cols] = ya
        u = lax.dot_general((kr * zeta_ref[h]).astype(BF16), vb, (((0,), (0,)), ((), ())),
                            preferred_element_type=F32)
        s_new = gdec_ref[h] * s_ref[0, h] + u
        s_ref[0, h] = s_new
        sb_s[h] = s_new.astype(BF16)

    def out_proj(row0):
        rows = slice(row0, row0 + half)
        o_ref[rows, :] = x_ref[rows, :] + _dot(ymix_s[rows, :], wout_ref[...])

    units_per_half = (half // RET_CHUNK) * RET_HEADS

    def run_half(first_chunk, pieces):
        for n in range(units_per_half):
            if n < len(pieces):
                pieces[n]()
            unit(first_chunk + n // RET_HEADS, n % RET_HEADS)

    run_half(0, projection(x_ref, half, half, window_xb))
    run_half(half // RET_CHUNK,
             (lambda: out_proj(0),) + projection(xn_ref, 0, 0, stash_xb))
    out_proj(half)

    for g in range(ngroup):
        hT_ref[0, :, lanes[g]] = hprev[g]
    cT_ref[0] = xp_s[pad + tm - tail:pad + tm, :]

    wg_o[...] = wg_ref[...].astype(BF16)
    wu_o[...] = wu_ref[...].astype(BF16)
    wd_o[...] = wd_ref[...].astype(BF16)
    for i in range(2):
        winc_o[i] = winc_ref[:, i * SG_HALF:(i + 1) * SG_HALF].astype(BF16)
    woutc_o[...] = woutc_ref[...].astype(BF16)


def _mixa_prompt(x, gdec, n1, win, cosf, sinf, dmat, zeta, xi, s0, h0, cb0, gn, cw, cbias,
                 wgate, ba, bx, lam, wout, wg, wu, wd, winc, woutc, *, batch, seq, tm):
    nt = seq // tm
    steps = batch * nt
    nlayer = wg.shape[0]
    per_layer = steps // nlayer
    ff_rows = D_MODEL // per_layer
    fd_rows = D_FF // per_layer
    c_rows = D_MODEL // steps
    band = lambda i: (i // per_layer, lax.rem(i, per_layer), 0)
    half = tm // 2
    in_w = 6 * RET_WIDTH
    est = (D_MODEL * in_w * 4 + D_MODEL * D_MODEL * 4 + 6 * tm * D_MODEL * 4
           + tm * 5 * RET_WIDTH * 4 + 4 * tm * LRU_WIDTH * 4)
    row_blk = lambda i: (i, 0)
    next_half = lambda i: (jnp.minimum(2 * (i + 1), 2 * steps - 2), 0)
    pos_blk = lambda i: (lax.rem(i, nt), 0)
    per_b4 = lambda i: (i // nt, 0, 0, 0)
    per_b3 = lambda i: (i // nt, 0, 0)
    hd = (RET_HEADS, RET_CHUNK, RET_CHUNK)
    return pl.pallas_call(
        functools.partial(_mixa_prompt_kernel, tm=tm, nt=nt),
        grid=(steps,),
        in_specs=[
            pl.BlockSpec(memory_space=pltpu.SMEM),
            pl.BlockSpec((tm, D_MODEL), row_blk),
            pl.BlockSpec((half, D_MODEL), next_half),
            _resident((1, D_MODEL)),
            _resident((in_w // RET_WIDTH, D_MODEL, RET_WIDTH)),
            pl.BlockSpec((tm, RET_DK), pos_blk),
            pl.BlockSpec((tm, RET_DK), pos_blk),
            _resident(hd),
            _resident(hd),
            _resident(hd),
            pl.BlockSpec((1, RET_HEADS, RET_DK, RET_DK), per_b4),
            pl.BlockSpec((1, 1, LRU_WIDTH), per_b3),
            pl.BlockSpec((1, CONV_W - 1, LRU_WIDTH), per_b3),
            _resident((1, RET_WIDTH)),
            _resident((CONV_W, LRU_WIDTH)),
            _resident((1, LRU_WIDTH)),
            _resident((LRU_WIDTH // LANES, LANES, 2 * LANES)),
            _resident((1, LRU_WIDTH)),
            _resident((1, LRU_WIDTH)),
            _resident((1, LRU_WIDTH)),
            _resident((D_MODEL, D_MODEL)),
            pl.BlockSpec((1, ff_rows, D_FF), band),
            pl.BlockSpec((1, ff_rows, D_FF), band),
            pl.BlockSpec((1, fd_rows, D_MODEL), band),
            pl.BlockSpec((c_rows, 2 * SG_HALF), row_blk),
            pl.BlockSpec((c_rows, D_MODEL), row_blk),
        ],
        out_specs=[
            pl.BlockSpec((tm, D_MODEL), row_blk),
            pl.BlockSpec((1, RET_HEADS, RET_DK, RET_DK), per_b4),
            pl.BlockSpec((1, 1, LRU_WIDTH), per_b3),
            pl.BlockSpec((1, CONV_W - 1, LRU_WIDTH), per_b3),
            pl.BlockSpec((1, ff_rows, D_FF), band),
            pl.BlockSpec((1, ff_rows, D_FF), band),
            pl.BlockSpec((1, fd_rows, D_MODEL), band),
            pl.BlockSpec((2, c_rows, SG_HALF), lambda i: (0, i, 0)),
            pl.BlockSpec((c_rows, D_MODEL), row_blk),
        ],
        out_shape=[
            jax.ShapeDtypeStruct((batch * seq, D_MODEL), F32),
            jax.ShapeDtypeStruct((batch, RET_HEADS, RET_DK, RET_DK), F32),
            jax.ShapeDtypeStruct((batch, 1, LRU_WIDTH), F32),
            jax.ShapeDtypeStruct((batch, CONV_W - 1, LRU_WIDTH), F32),
            jax.ShapeDtypeStruct(wg.shape, BF16),
            jax.ShapeDtypeStruct(wu.shape, BF16),
            jax.ShapeDtypeStruct(wd.shape, BF16),
            jax.ShapeDtypeStruct((2, D_MODEL, SG_HALF), BF16),
            jax.ShapeDtypeStruct(woutc.shape, BF16),
        ],
        scratch_shapes=[
            pltpu.VMEM((tm, 5 * RET_WIDTH), F32),
            pltpu.VMEM((half, LRU_WIDTH), F32),
            pltpu.VMEM((tm + 2 * SUBLANES, LRU_WIDTH), F32),
            pltpu.VMEM((tm, D_MODEL), F32),
            pltpu.VMEM((RET_HEADS, RET_DK, RET_DK), BF16),
        ],
        compiler_params=pltpu.CompilerParams(
            dimension_semantics=("arbitrary",), vmem_limit_bytes=_vmem_limit(est)),
        name="mixa_prompt",
    )(gdec, x, x, n1, win, cosf, sinf, dmat, zeta, xi, s0, h0, cb0, gn, cw, cbias, wgate,
      ba, bx, lam, wout, wg, wu, wd, winc, woutc)


def _mixa_sample_kernel(gdec_ref, x_ref, n1_ref, win_ref, cos_ref, sin_ref,
                        s0_ref, h0_ref, cb0_ref, gn_ref, cw_ref, cbias_ref, wgate_ref,
                        ba_ref, bx_ref, lam_ref, wout_ref,
                        o_ref, s_ref, hT_ref, cT_ref, ymix_s, *, bt):
    x = x_ref[...]
    hn = _rms(x, n1_ref[...])
    cosf = cos_ref[...]
    sinf = sin_ref[...]
    w = RET_WIDTH
    zq = _dot(hn, win_ref[0])
    zk = _dot(hn, win_ref[1])
    zv = _dot(hn, win_ref[2])
    sg = _silu(_dot(hn, win_ref[3]))
    xb = _dot(hn, win_ref[4])
    gg = _gelu(_dot(hn, win_ref[5]))

    rowi = lax.broadcasted_iota(jnp.int32, (bt, RET_DK), 0)
    for h in range(RET_HEADS):
        cols = slice(h * RET_DK, (h + 1) * RET_DK)
        qs = (_rotary(zq[:, cols], cosf, sinf) * Q_SCALE).astype(BF16)
        kb = _rotary(zk[:, cols], cosf, sinf).astype(BF16)
        vh = zv[:, cols]
        qk = jnp.sum(qs.astype(F32) * kb.astype(F32), axis=-1, keepdims=True)
        cross = jnp.zeros((bt, RET_DK), F32)
        for b in range(bt):
            s_old = s0_ref[b, h]
            cr = _dot(qs, s_old.astype(BF16))
            cross = cross + jnp.where(rowi == b, cr, 0.0)
            u = lax.dot_general(kb, jnp.where(rowi == b, vh, 0.0).astype(BF16),
                                (((0,), (0,)), ((), ())), preferred_element_type=F32)
            s_ref[b, h] = gdec_ref[h] * s_old + u
        o = qk * vh + cross * gdec_ref[h]
        ymix_s[:, cols] = _group_norm(o) * gn_ref[:, cols] * sg[:, cols]

    xc = cbias_ref[...]
    for j in range(CONV_W - 1):
        xc = xc + cb0_ref[j] * cw_ref[j:j + 1, :]
    xc = xc + xb * cw_ref[CONV_W - 1:CONV_W, :]
    for j in range(CONV_W - 2):
        cT_ref[j] = cb0_ref[j + 1]
    cT_ref[CONV_W - 2] = xb
    a, b_in = _lru_coeffs(xc, wgate_ref, ba_ref[...], bx_ref[...], lam_ref[...])
    hnew = a * h0_ref[...] + b_in
    hT_ref[...] = hnew
    ymix_s[:, RET_WIDTH:] = hnew * gg
    o_ref[...] = x + _dot(ymix_s[...], wout_ref[...])


def _mixa_sample(x, gdec, n1, win, cosf, sinf, s0, h0, cb0, gn, cw, cbias,
                 wgate, ba, bx, lam, wout, *, bt):
    batch = x.shape[0]
    in_w = 6 * RET_WIDTH
    est = (D_MODEL * in_w * 4 + D_MODEL * D_MODEL * 4
           + 4 * bt * RET_HEADS * RET_DK * RET_DK * 4 + 16 * bt * in_w * 4)
    rows = lambda i: (i, 0)
    return pl.pallas_call(
        functools.partial(_mixa_sample_kernel, bt=bt),
        grid=(batch // bt,),
        in_specs=[
            pl.BlockSpec(memory_space=pltpu.SMEM),
            pl.BlockSpec((bt, D_MODEL), rows),
            _resident((1, D_MODEL)),
            _resident((in_w // RET_WIDTH, D_MODEL, RET_WIDTH)),
            _resident((1, RET_DK)),
            _resident((1, RET_DK)),
            pl.BlockSpec((bt, RET_HEADS, RET_DK, RET_DK), lambda i: (i, 0, 0, 0)),
            pl.BlockSpec((bt, LRU_WIDTH), rows),
            pl.BlockSpec((CONV_W - 1, bt, LRU_WIDTH), lambda i: (0, i, 0)),
            _resident((1, RET_WIDTH)),
            _resident((CONV_W, LRU_WIDTH)),
            _resident((1, LRU_WIDTH)),
            _resident((LRU_WIDTH // LANES, LANES, 2 * LANES)),
            _resident((1, LRU_WIDTH)),
            _resident((1, LRU_WIDTH)),
            _resident((1, LRU_WIDTH)),
            _resident((D_MODEL, D_MODEL)),
        ],
        out_specs=[
            pl.BlockSpec((bt, D_MODEL), rows),
            pl.BlockSpec((bt, RET_HEADS, RET_DK, RET_DK), lambda i: (i, 0, 0, 0)),
            pl.BlockSpec((bt, LRU_WIDTH), rows),
            pl.BlockSpec((CONV_W - 1, bt, LRU_WIDTH), lambda i: (0, i, 0)),
        ],
        out_shape=[
            jax.ShapeDtypeStruct((batch, D_MODEL), F32),
            jax.ShapeDtypeStruct((batch, RET_HEADS, RET_DK, RET_DK), F32),
            jax.ShapeDtypeStruct((batch, LRU_WIDTH), F32),
            jax.ShapeDtypeStruct((CONV_W - 1, batch, LRU_WIDTH), F32),
        ],
        scratch_shapes=[pltpu.VMEM((bt, D_MODEL), F32)],
        compiler_params=pltpu.CompilerParams(
            dimension_semantics=("arbitrary",), vmem_limit_bytes=_vmem_limit(est)),
        name="mixa_sample",
    )(gdec, x, n1, win, cosf, sinf, s0, h0, cb0, gn, cw, cbias, wgate,
      ba, bx, lam, wout)


def _rope_tables(pos):
    half = RET_DK // 2
    inv = ROPE_BASE ** (-jnp.arange(half, dtype=F32) / half)
    ang = pos[:, None] * inv[None, :]
    cos, sin = jnp.cos(ang), jnp.sin(ang)
    return jnp.concatenate([cos, cos], axis=1), jnp.concatenate([-sin, sin], axis=1)


def _decay_tables(c):
    lg = jnp.log1p(-jnp.exp2(-5.0 - jnp.arange(RET_HEADS, dtype=F32)))
    idx = jnp.arange(c, dtype=F32)
    diff = idx[:, None] - idx[None, :]
    causal = diff >= 0
    dmat = jnp.where(causal[None], jnp.exp(jnp.where(causal, diff, 0.0)[None] * lg[:, None, None]), 0.0)
    zeta = jnp.exp((c - 1.0 - idx)[None, :] * lg[:, None])
    xi = jnp.exp((idx + 1.0)[None, :] * lg[:, None])
    gdec = jnp.exp(c * lg)
    return lg, dmat, zeta, xi, gdec


def _gate_tiles(wa, wx):
    z = jnp.zeros((LRU_BW, LRU_BW), wa.dtype)
    bd = lambda w, g: jnp.block([[w[2 * g], z], [z, w[2 * g + 1]]])
    tiles = [jnp.concatenate([bd(wa, g), bd(wx, g)], axis=1) for g in range(LRU_BLOCKS // 2)]
    return jnp.stack(tiles).astype(BF16)


def kernel(x_prompt, x_sample, state_ret, state_lru, state_conv, norm1, norm2, norm_f, w_in_a, ret_gn,
           conv_w, conv_b, lru_wa, lru_ba, lru_wx, lru_bx, lru_lambda, w_out_a, w_in_c, sg_norm_g,
           sg_norm_b, sg_ws, sg_bs, w_out_c, ffn_wg, ffn_wu, ffn_wd):
    bp, lp, _ = x_prompt.shape
    bs = x_sample.shape[0]
    row = lambda v: v.reshape(1, -1)

    win_a = jnp.stack([w_in_a[0][:, i * RET_WIDTH:(i + 1) * RET_WIDTH].astype(BF16)
                       for i in range(6)])
    wout_a = w_out_a[0].astype(BF16)
    wgate = _gate_tiles(lru_wa[0], lru_wx[0])
    mixa_w = (row(ret_gn[0]), conv_w[0], row(conv_b[0]), wgate, row(lru_ba[0]),
              row(lru_bx[0]), row(lru_lambda[0]), wout_a)

    tm = 512
    bt = 16
    cos_p, sin_p = _rope_tables(jnp.arange(lp, dtype=F32))
    cos_s, sin_s = _rope_tables(PAST_LEN + jnp.arange(1, dtype=F32))
    _, dmat, zeta, xi, gdec_p = _decay_tables(RET_CHUNK)
    _, _, _, _, gdec_s = _decay_tables(1)
    bc = lambda t: jnp.broadcast_to(t[:, :, None], (RET_HEADS, RET_CHUNK, RET_DK))
    zeros = lambda *s: jnp.zeros(s, F32)
    xp = x_prompt.reshape(bp * lp, D_MODEL)
    xs = x_sample.reshape(bs, D_MODEL)

    xp, ret_p, lru_p, conv_p, wg, wu, wd, win_c, wout_c = _mixa_prompt(
        xp, gdec_p, row(norm1[0]), win_a, cos_p, sin_p, dmat, bc(zeta), bc(xi),
        zeros(bp, RET_HEADS, RET_DK, RET_DK), zeros(bp, 1, LRU_WIDTH),
        zeros(bp, CONV_W - 1, LRU_WIDTH), *mixa_w, ffn_wg, ffn_wu, ffn_wd, w_in_c[0],
        w_out_c[0], batch=bp, seq=lp, tm=tm)
    ffn = lambda xp_, xs_, layer, final: _ffn(
        xp_, xs_, row(norm2[layer]), wg, wu, wd, row(norm_f), layer=layer,
        final_norm=final, tm=tm)
    xs, ret_s, lru_s, conv_s = _mixa_sample(
        xs, gdec_s, row(norm1[0]), win_a, cos_s, sin_s, state_ret[0],
        state_lru[0], jnp.transpose(state_conv[0], (1, 0, 2)), *mixa_w, bt=bt)
    xp, xs = ffn(xp, xs, 0, False)

    bsb = jnp.broadcast_to(sg_bs[0][:, :, None], (SG_GROUPS, SG_CHUNK, SG_GW))
    xp = _mixc_prompt(xp, row(norm1[1]), win_c, row(sg_norm_g[0]), row(sg_norm_b[0]), sg_ws[0],
                      bsb, wout_c, tm=2 * tm)
    sg_scale = row(jnp.repeat(sg_ws[0][:, 0, 0], SG_GW))
    sg_bias = row(jnp.repeat(sg_bs[0][:, 0], SG_GW))
    xs, v_s = _mixc_sample(xs, row(norm1[1]), win_c, row(sg_norm_g[0]), row(sg_norm_b[0]),
                           sg_scale, sg_bias, wout_c)
    y_prompt, y_sample = ffn(xp, xs, 1, True)

    return (y_prompt.reshape(bp, lp, D_MODEL),
            y_sample.reshape(bs, 1, D_MODEL),
            ret_p[None],
            ret_s[None],
            lru_p.reshape(1, bp, LRU_WIDTH),
            lru_s[None],
            conv_p[None],
            jnp.transpose(conv_s, (1, 0, 2))[None],
            v_s.reshape(1, bs, 1, SG_HALF))
```

```python
import functools
import math

import jax
import jax.numpy as jnp
from jax import lax
from jax.experimental import pallas as pl
from jax.experimental.pallas import tpu as pltpu

F32 = jnp.float32
BF16 = jnp.bfloat16

D_MODEL = 1024
RET_WIDTH = 512
RET_HEADS = 4
RET_DK = 128
RET_CHUNK = 128
ROPE_BASE = 10000.0
LRU_WIDTH = 512
LRU_BLOCKS = 8
LRU_BW = 64
LRU_C = 8.0
CONV_W = 4
SG_CHUNK = 128
SG_HALF = 1024
SG_GROUPS = 8
SG_GW = 128
D_FF = 2816
EPS = 1e-6
PAST_LEN = 16384

V7X_VMEM_BYTES = 64 * 1024 * 1024
SUBLANES = 8
LANES = 128
V7X_MXU_COLS = 256
PIECE_COLS = V7X_MXU_COLS
PIECES_PER_STAGE = 1

Q_SCALE = RET_DK ** -0.5
GELU_C = math.sqrt(2.0 / math.pi)


def _vmem_limit(estimate_bytes):
    return int(min(estimate_bytes * 3 // 2 + (8 << 20), V7X_VMEM_BYTES - (6 << 20)))


def _resident(shape):
    nd = len(shape)
    return pl.BlockSpec(shape, lambda *_: (0,) * nd, pipeline_mode=pl.Buffered(1))


def _dot(a, b):
    return lax.dot_general(a, b, (((1,), (0,)), ((), ())), preferred_element_type=F32)


def _rms(x, g):
    ms = jnp.mean(x * x, axis=-1, keepdims=True)
    return x * lax.rsqrt(ms + EPS) * g


def _gelu(x):
    return x * (0.5 * (1.0 + jnp.tanh(GELU_C * (x + 0.044715 * (x * x * x)))))


def _silu(x):
    return x * jax.nn.sigmoid(x)


def _softplus(x):
    return jnp.maximum(x, 0.0) + jnp.log1p(jnp.exp(-jnp.abs(x)))


def _sqrt_nonneg(y):
    return jnp.where(y > 0.0, y * lax.rsqrt(y), 0.0)


def _group_norm(o):
    mu = jnp.mean(o, axis=-1, keepdims=True)
    oc = o - mu
    var = jnp.mean(oc * oc, axis=-1, keepdims=True)
    return oc * lax.rsqrt(var + EPS)


def _rotary(x, cosf, sinf):
    return x * cosf + pltpu.roll(x, RET_DK // 2, 1) * sinf


def _lru_group(xc, wgate, ba, bx, sp):
    both = _dot(xc.astype(BF16), wgate)
    r = jax.nn.sigmoid(both[:, :LANES] + ba)
    i = jax.nn.sigmoid(both[:, LANES:] + bx)
    log_a = (-LRU_C * r) * sp
    a = jnp.exp(log_a)
    mult = _sqrt_nonneg(-jnp.tanh(log_a) * (a * a + 1.0))
    return a, xc * i * mult


def _lru_coeffs(xc, wgate_ref, ba, bx, sp):
    xcb = xc.astype(BF16)
    ra, rx = [], []
    for g in range(LRU_WIDTH // LANES):
        cols = slice(g * LANES, (g + 1) * LANES)
        both = _dot(xcb[:, cols], wgate_ref[g])
        ra.append(both[:, :LANES])
        rx.append(both[:, LANES:])
    r = jax.nn.sigmoid(jnp.concatenate(ra, axis=1) + ba)
    i = jax.nn.sigmoid(jnp.concatenate(rx, axis=1) + bx)
    log_a = (-LRU_C * r) * sp
    a = jnp.exp(log_a)
    mult = _sqrt_nonneg(-jnp.tanh(log_a) * (a * a + 1.0))
    return a, xc * i * mult


def _ffn_rows(x, n2, wg_ref, wu_ref, wd_ref, nf, final_norm):
    h = _rms(x, n2).astype(BF16)
    act = (_silu(_dot(h, wg_ref[...])) * _dot(h, wu_ref[...])).astype(BF16)
    y = x + _dot(act, wd_ref[...])
    return _rms(y, nf) if final_norm else y


def _ffn_kernel(x_ref, xs_ref, n2_ref, wg_ref, wu_ref, wd_ref, nf_ref, o_ref, os_ref, *,
                final_norm, prompt_steps):
    step = pl.program_id(0)
    args = (n2_ref[...], wg_ref, wu_ref, wd_ref, nf_ref[...], final_norm)

    @pl.when(step < prompt_steps)
    def _():
        o_ref[...] = _ffn_rows(x_ref[...], *args)

    @pl.when(step == prompt_steps)
    def _():
        os_ref[...] = _ffn_rows(xs_ref[...], *args)


def _ffn(x, xs, n2, wg, wu, wd, nf, *, layer, final_norm, tm):
    t, ts = x.shape[0], xs.shape[0]
    steps = t // tm
    est = (3 * D_MODEL * D_FF * 2 + 4 * tm * D_MODEL * 4 + 4 * ts * D_MODEL * 4
           + 3 * tm * D_FF * 4)
    layer_blk = lambda shape: pl.BlockSpec((None,) + shape, lambda i: (layer, 0, 0),
                                           pipeline_mode=pl.Buffered(1))
    prompt_blk = pl.BlockSpec((tm, D_MODEL), lambda i: (jnp.minimum(i, steps - 1), 0))
    return pl.pallas_call(
        functools.partial(_ffn_kernel, final_norm=final_norm, prompt_steps=steps),
        grid=(steps + 1,),
        in_specs=[
            prompt_blk,
            _resident((ts, D_MODEL)),
            _resident((1, D_MODEL)),
            layer_blk((D_MODEL, D_FF)),
            layer_blk((D_MODEL, D_FF)),
            layer_blk((D_FF, D_MODEL)),
            _resident((1, D_MODEL)),
        ],
        out_specs=[prompt_blk, pl.BlockSpec((ts, D_MODEL), lambda i: (0, 0))],
        out_shape=[jax.ShapeDtypeStruct((t, D_MODEL), F32),
                   jax.ShapeDtypeStruct((ts, D_MODEL), F32)],
        compiler_params=pltpu.CompilerParams(
            dimension_semantics=("arbitrary",), vmem_limit_bytes=_vmem_limit(est)),
        name="ffn_final" if final_norm else "ffn",
    )(x, xs, n2, wg, wu, wd, nf)


def _sgu_u(hn, win_ref):
    return _gelu(_dot(hn, win_ref[0]))


def _sgu_v(hn, win_ref, vg, vb):
    zv = _gelu(_dot(hn, win_ref[1]))
    mu = jnp.mean(zv, axis=-1, keepdims=True)
    vc = zv - mu
    var = jnp.mean(vc * vc, axis=-1, keepdims=True)
    return vc * lax.rsqrt(var + EPS) * vg + vb


def _mixc_prompt_kernel(x_ref, n1_ref, win_ref, vg_ref, vb_ref, ws_ref, bsb_ref, wout_ref,
                        o_ref, u_s, v_s, gated_s, wm_s, *, tm):
    half = tm // 2
    nchunk = half // SG_CHUNK
    ri = lax.broadcasted_iota(jnp.int32, (SG_CHUNK, SG_CHUNK), 0)
    ci = lax.broadcasted_iota(jnp.int32, (SG_CHUNK, SG_CHUNK), 1)
    for g in range(SG_GROUPS):
        wm_s[g] = jnp.where(ri >= ci, ws_ref[g], 0.0).astype(BF16)

    def norm(r0):
        return _rms(x_ref[r0:r0 + half, :], n1_ref[...])

    def put_u(r0, hn):
        u_s[r0:r0 + half, :] = _sgu_u(hn, win_ref)

    def put_v(r0, hn):
        v_s[r0:r0 + half, :] = _sgu_v(hn, win_ref, vg_ref[...], vb_ref[...]).astype(BF16)

    def gate(r0):
        for g in range(SG_GROUPS):
            cols = slice(g * SG_GW, (g + 1) * SG_GW)
            vcat = jnp.concatenate(
                [v_s[r0 + c * SG_CHUNK:r0 + (c + 1) * SG_CHUNK, cols] for c in range(nchunk)],
                axis=1)
            sv_all = _dot(wm_s[g], vcat)
            for c in range(nchunk):
                rows = slice(r0 + c * SG_CHUNK, r0 + (c + 1) * SG_CHUNK)
                sv = sv_all[:, c * SG_GW:(c + 1) * SG_GW] + bsb_ref[g]
                gated_s[rows, cols] = u_s[rows, cols] * sv

    def out_proj(r0):
        rows = slice(r0, r0 + half)
        o_ref[rows, :] = x_ref[rows, :] + _dot(gated_s[rows, :], wout_ref[...])

    hn_a = norm(0)
    put_u(0, hn_a)
    put_v(0, hn_a)
    hn_b = norm(half)
    put_u(half, hn_b)
    gate(0)
    put_v(half, hn_b)
    out_proj(0)
    gate(half)
    out_proj(half)


def _mixc_prompt(x, n1, win, vg, vb, ws, bsb, wout, *, tm):
    t = x.shape[0]
    est = 3 * D_MODEL * SG_HALF * 2 + 4 * tm * D_MODEL * 4 + 5 * tm * 2 * SG_HALF * 4
    return pl.pallas_call(
        functools.partial(_mixc_prompt_kernel, tm=tm),
        grid=(t // tm,),
        in_specs=[
            pl.BlockSpec((tm, D_MODEL), lambda i: (i, 0)),
            _resident((1, D_MODEL)),
            _resident((2, D_MODEL, SG_HALF)),
            _resident((1, SG_HALF)),
            _resident((1, SG_HALF)),
            _resident((SG_GROUPS, SG_CHUNK, SG_CHUNK)),
            _resident((SG_GROUPS, SG_CHUNK, SG_GW)),
            _resident((SG_HALF, D_MODEL)),
        ],
        out_specs=pl.BlockSpec((tm, D_MODEL), lambda i: (i, 0)),
        out_shape=jax.ShapeDtypeStruct((t, D_MODEL), F32),
        scratch_shapes=[
            pltpu.VMEM((tm, SG_HALF), F32),
            pltpu.VMEM((tm, SG_HALF), BF16),
            pltpu.VMEM((tm, SG_HALF), F32),
            pltpu.VMEM((SG_GROUPS, SG_CHUNK, SG_CHUNK), BF16),
        ],
        compiler_params=pltpu.CompilerParams(
            dimension_semantics=("arbitrary",), vmem_limit_bytes=_vmem_limit(est)),
        name="mixc_prompt",
    )(x, n1, win, vg, vb, ws, bsb, wout)


def _mixc_sample_kernel(x_ref, n1_ref, win_ref, vg_ref, vb_ref, scale_ref, bias_ref, wout_ref,
                        o_ref, v_ref):
    x = x_ref[...]
    hn = _rms(x, n1_ref[...])
    u = _sgu_u(hn, win_ref)
    vn = _sgu_v(hn, win_ref, vg_ref[...], vb_ref[...])
    v_ref[...] = vn
    sv = vn * scale_ref[...] + bias_ref[...]
    o_ref[...] = x + _dot(u * sv, wout_ref[...])


def _mixc_sample(x, n1, win, vg, vb, scale, bias, wout):
    t = x.shape[0]
    est = 3 * D_MODEL * SG_HALF * 2 + 8 * t * 2 * SG_HALF * 4
    return pl.pallas_call(
        _mixc_sample_kernel,
        grid=(1,),
        in_specs=[
            _resident((t, D_MODEL)),
            _resident((1, D_MODEL)),
            _resident((2, D_MODEL, SG_HALF)),
            _resident((1, SG_HALF)),
            _resident((1, SG_HALF)),
            _resident((1, SG_HALF)),
            _resident((1, SG_HALF)),
            _resident((SG_HALF, D_MODEL)),
        ],
        out_specs=[
            pl.BlockSpec((t, D_MODEL), lambda i: (0, 0)),
            pl.BlockSpec((t, SG_HALF), lambda i: (0, 0)),
        ],
        out_shape=[
            jax.ShapeDtypeStruct((t, D_MODEL), F32),
            jax.ShapeDtypeStruct((t, SG_HALF), F32),
        ],
        compiler_params=pltpu.CompilerParams(
            dimension_semantics=("arbitrary",), vmem_limit_bytes=_vmem_limit(est)),
        name="mixc_sample",
    )(x, n1, win, vg, vb, scale, bias, wout)


def _lru_scan(a, b, h0):
    tm, w = a.shape
    groups = tm // SUBLANES
    a3 = a.reshape(groups, SUBLANES, w)
    b3 = b.reshape(groups, SUBLANES, w)
    sub = lax.broadcasted_iota(jnp.int32, (groups, SUBLANES, w), 1)
    for s in (1, 2, 4):
        a_sh = pltpu.roll(a3, s, 1)
        b_sh = pltpu.roll(b3, s, 1)
        keep = sub >= s
        b3 = jnp.where(keep, a3 * b_sh + b3, b3)
        a3 = jnp.where(keep, a3 * a_sh, a3)
    hprev = jnp.broadcast_to(h0, (SUBLANES, w))
    hs = []
    for r in range(groups):
        hr = a3[r] * hprev + b3[r]
        hs.append(hr)
        hprev = jnp.broadcast_to(hr[SUBLANES - 1:SUBLANES, :], (SUBLANES, w))
    return jnp.concatenate(hs, axis=0), hprev[0:1, :]


def _mixa_prompt_kernel(gdec_ref, x_ref, xn_ref, n1_ref, win_ref, cos_ref, sin_ref, dmat_ref,
                        zeta_ref, xi_ref, s0_ref, h0_ref, cb0_ref, gn_ref, cw_ref, cbias_ref,
                        wgate_ref, ba_ref, bx_ref, lam_ref, wout_ref,
                        wg_ref, wu_ref, wd_ref, winc_ref, woutc_ref,
                        o_ref, s_ref, hT_ref, cT_ref,
                        wg_o, wu_o, wd_o, winc_o, woutc_o,
                        z_s, xbn_s, xp_s, ymix_s, sb_s, *, tm, nt):
    half = tm // 2
    w = RET_WIDTH
    pad = SUBLANES
    tail = CONV_W - 1
    step = pl.program_id(0)
    ZQ, ZK, ZV, ZG, ZGB = range(5)

    pw = PIECE_COLS
    per_blk = w // pw

    def stash_xb(j, v):
        xbn_s[:, j * pw:(j + 1) * pw] = v

    def window_xb(j, v):
        xp_s[pad + half:pad + tm, j * pw:(j + 1) * pw] = v

    def projection(src_ref, src_row0, z_row0, put_xb):
        hn = []

        def xb_piece(j):
            def run():
                if j == 0:
                    hn.append(
                        _rms(src_ref[src_row0:src_row0 + half, :], n1_ref[...]).astype(BF16))
                put_xb(j, _dot(hn[0], win_ref[4 * per_blk + j]))
            return run

        def z_piece(win_blk, z_blk, j):
            def run():
                z_s[z_row0:z_row0 + half, z_blk * w + j * pw:z_blk * w + (j + 1) * pw] = _dot(
                    hn[0], win_ref[win_blk * per_blk + j])
            return run

        pieces = [xb_piece(j) for j in range(per_blk)]
        for win_blk, z_blk in ((5, ZGB), (0, ZQ), (1, ZK), (2, ZV), (3, ZG)):
            pieces += [z_piece(win_blk, z_blk, j) for j in range(per_blk)]
        return pieces

    @pl.when(step == 0)
    def _():
        for piece in projection(x_ref, 0, 0, stash_xb):
            piece()

    @pl.when(lax.rem(step, nt) == 0)
    def _():
        s_ref[...] = s0_ref[...]
        hT_ref[...] = h0_ref[...]
        cT_ref[...] = cb0_ref[...]

    xp_s[pad - tail:pad, :] = cT_ref[0]
    xp_s[pad:pad + half, :] = xbn_s[...]
    for h in range(RET_HEADS):
        sb_s[h] = s_ref[0, h].astype(BF16)
    hprev = [hT_ref[0]]
    sp = _softplus(-lam_ref[...])
    heads = range(RET_HEADS)

    def lru_unit(c, emit):
        rows = slice(c * RET_CHUNK, (c + 1) * RET_CHUNK)
        xc = cbias_ref[...]
        for j in range(CONV_W):
            off = pad - tail + j + c * RET_CHUNK
            xc = xc + xp_s[off:off + RET_CHUNK, :] * cw_ref[j:j + 1, :]
        emit()
        a, b_in = _lru_coeffs(xc, wgate_ref, ba_ref[...], bx_ref[...], sp)
        emit()
        hl, hprev[0] = _lru_scan(a, b_in, hprev[0])
        ymix_s[rows, RET_WIDTH:] = (hl * _gelu(z_s[rows, ZGB * w:(ZGB + 1) * w])).astype(BF16)

    def ret_unit(c, emit):
        rows = slice(c * RET_CHUNK, (c + 1) * RET_CHUNK)
        zcol = lambda blk, h: slice(blk * w + h * LANES, blk * w + (h + 1) * LANES)
        cols = lambda h: slice(h * RET_DK, (h + 1) * RET_DK)
        cosf = cos_ref[rows, :]
        sinf = sin_ref[rows, :]
        qr = [_rotary(z_s[rows, zcol(ZQ, h)], cosf, sinf) * Q_SCALE for h in heads]
        kr = [_rotary(z_s[rows, zcol(ZK, h)], cosf, sinf) for h in heads]
        vb = [z_s[rows, zcol(ZV, h)].astype(BF16) for h in heads]
        emit()
        sc = [lax.dot_general(qr[h].astype(BF16), kr[h].astype(BF16), (((1,), (1,)), ((), ())),
                              preferred_element_type=F32) for h in heads]
        emit()
        lhs = [jnp.concatenate([(sc[h] * dmat_ref[h]).astype(BF16),
                                (qr[h] * xi_ref[h]).astype(BF16)], axis=1) for h in heads]
        o = [_dot(lhs[h], jnp.concatenate([vb[h], sb_s[h]], axis=0)) for h in heads]
        u = [lax.dot_general((kr[h] * zeta_ref[h]).astype(BF16), vb[h], (((0,), (0,)), ((), ())),
                             preferred_element_type=F32) for h in heads]
        emit()
        for h in heads:
            ya = _group_norm(o[h]) * gn_ref[:, cols(h)] * _silu(z_s[rows, zcol(ZG, h)])
            ymix_s[rows, cols(h)] = ya.astype(BF16)
        emit()
        for h in heads:
            s_new = gdec_ref[h] * s_ref[0, h] + u[h]
            s_ref[0, h] = s_new
            sb_s[h] = s_new.astype(BF16)

    def out_proj(row0):
        rows = slice(row0, row0 + half)

        def piece(j):
            def run():
                cols = slice(j * pw, (j + 1) * pw)
                o_ref[rows, cols] = x_ref[rows, cols] + _dot(ymix_s[rows, :], wout_ref[j])
            return run

        return [piece(j) for j in range(D_MODEL // pw)]

    def run_half(first_chunk, pieces):
        pending = iter(pieces)

        def emit():
            for _ in range(PIECES_PER_STAGE):
                next(pending, lambda: None)()

        for c in range(first_chunk, first_chunk + half // RET_CHUNK):
            emit()
            lru_unit(c, emit)
            emit()
            ret_unit(c, emit)
        for piece in pending:
            piece()

    run_half(0, projection(x_ref, half, half, window_xb))
    run_half(half // RET_CHUNK, out_proj(0) + projection(xn_ref, 0, 0, stash_xb))
    for piece in out_proj(half):
        piece()

    hT_ref[0] = hprev[0]
    cT_ref[0] = xp_s[pad + tm - tail:pad + tm, :]

    wg_o[...] = wg_ref[...].astype(BF16)
    wu_o[...] = wu_ref[...].astype(BF16)
    wd_o[...] = wd_ref[...].astype(BF16)
    for i in range(2):
        winc_o[i] = winc_ref[:, i * SG_HALF:(i + 1) * SG_HALF].astype(BF16)
    woutc_o[...] = woutc_ref[...].astype(BF16)


def _mixa_prompt(x, gdec, n1, win, cosf, sinf, dmat, zeta, xi, s0, h0, cb0, gn, cw, cbias,
                 wgate, ba, bx, lam, wout, wg, wu, wd, winc, woutc, *, batch, seq, tm):
    nt = seq // tm
    steps = batch * nt
    nlayer = wg.shape[0]
    per_layer = steps // nlayer
    ff_rows = D_MODEL // per_layer
    fd_rows = D_FF // per_layer
    c_rows = D_MODEL // steps
    band = lambda i: (i // per_layer, lax.rem(i, per_layer), 0)
    half = tm // 2
    in_w = 6 * RET_WIDTH
    est = (D_MODEL * in_w * 4 + D_MODEL * D_MODEL * 4 + 6 * tm * D_MODEL * 4
           + tm * 5 * RET_WIDTH * 4 + 4 * tm * LRU_WIDTH * 4)
    row_blk = lambda i: (i, 0)
    next_half = lambda i: (jnp.minimum(2 * (i + 1), 2 * steps - 2), 0)
    pos_blk = lambda i: (lax.rem(i, nt), 0)
    per_b4 = lambda i: (i // nt, 0, 0, 0)
    per_b3 = lambda i: (i // nt, 0, 0)
    hd = (RET_HEADS, RET_CHUNK, RET_CHUNK)
    return pl.pallas_call(
        functools.partial(_mixa_prompt_kernel, tm=tm, nt=nt),
        grid=(steps,),
        in_specs=[
            pl.BlockSpec(memory_space=pltpu.SMEM),
            pl.BlockSpec((tm, D_MODEL), row_blk),
            pl.BlockSpec((half, D_MODEL), next_half),
            _resident((1, D_MODEL)),
            _resident((in_w // PIECE_COLS, D_MODEL, PIECE_COLS)),
            pl.BlockSpec((tm, RET_DK), pos_blk),
            pl.BlockSpec((tm, RET_DK), pos_blk),
            _resident(hd),
            _resident(hd),
            _resident(hd),
            pl.BlockSpec((1, RET_HEADS, RET_DK, RET_DK), per_b4),
            pl.BlockSpec((1, 1, LRU_WIDTH), per_b3),
            pl.BlockSpec((1, CONV_W - 1, LRU_WIDTH), per_b3),
            _resident((1, RET_WIDTH)),
            _resident((CONV_W, LRU_WIDTH)),
            _resident((1, LRU_WIDTH)),
            _resident((LRU_WIDTH // LANES, LANES, 2 * LANES)),
            _resident((1, LRU_WIDTH)),
            _resident((1, LRU_WIDTH)),
            _resident((1, LRU_WIDTH)),
            _resident((D_MODEL // PIECE_COLS, D_MODEL, PIECE_COLS)),
            pl.BlockSpec((1, ff_rows, D_FF), band),
            pl.BlockSpec((1, ff_rows, D_FF), band),
            pl.BlockSpec((1, fd_rows, D_MODEL), band),
            pl.BlockSpec((c_rows, 2 * SG_HALF), row_blk),
            pl.BlockSpec((c_rows, D_MODEL), row_blk),
        ],
        out_specs=[
            pl.BlockSpec((tm, D_MODEL), row_blk),
            pl.BlockSpec((1, RET_HEADS, RET_DK, RET_DK), per_b4),
            pl.BlockSpec((1, 1, LRU_WIDTH), per_b3),
            pl.BlockSpec((1, CONV_W - 1, LRU_WIDTH), per_b3),
            pl.BlockSpec((1, ff_rows, D_FF), band),
            pl.BlockSpec((1, ff_rows, D_FF), band),
            pl.BlockSpec((1, fd_rows, D_MODEL), band),
            pl.BlockSpec((2, c_rows, SG_HALF), lambda i: (0, i, 0)),
            pl.BlockSpec((c_rows, D_MODEL), row_blk),
        ],
        out_shape=[
            jax.ShapeDtypeStruct((batch * seq, D_MODEL), F32),
            jax.ShapeDtypeStruct((batch, RET_HEADS, RET_DK, RET_DK), F32),
            jax.ShapeDtypeStruct((batch, 1, LRU_WIDTH), F32),
            jax.ShapeDtypeStruct((batch, CONV_W - 1, LRU_WIDTH), F32),
            jax.ShapeDtypeStruct(wg.shape, BF16),
            jax.ShapeDtypeStruct(wu.shape, BF16),
            jax.ShapeDtypeStruct(wd.shape, BF16),
            jax.ShapeDtypeStruct((2, D_MODEL, SG_HALF), BF16),
            jax.ShapeDtypeStruct(woutc.shape, BF16),
        ],
        scratch_shapes=[
            pltpu.VMEM((tm, 5 * RET_WIDTH), F32),
            pltpu.VMEM((half, LRU_WIDTH), F32),
            pltpu.VMEM((tm + 2 * SUBLANES, LRU_WIDTH), F32),
            pltpu.VMEM((tm, D_MODEL), BF16),
            pltpu.VMEM((RET_HEADS, RET_DK, RET_DK), BF16),
        ],
        compiler_params=pltpu.CompilerParams(
            dimension_semantics=("arbitrary",), vmem_limit_bytes=_vmem_limit(est)),
        name="mixa_prompt",
    )(gdec, x, x, n1, win, cosf, sinf, dmat, zeta, xi, s0, h0, cb0, gn, cw, cbias, wgate,
      ba, bx, lam, wout, wg, wu, wd, winc, woutc)


def _mixa_sample_kernel(gdec_ref, x_ref, n1_ref, win_ref, cos_ref, sin_ref,
                        s0_ref, h0_ref, cb0_ref, gn_ref, cw_ref, cbias_ref, wgate_ref,
                        ba_ref, bx_ref, lam_ref, wout_ref,
                        o_ref, s_ref, hT_ref, cT_ref, ymix_s, *, bt):
    x = x_ref[...]
    hn = _rms(x, n1_ref[...])
    cosf = cos_ref[...]
    sinf = sin_ref[...]
    w = RET_WIDTH
    per_blk = w // PIECE_COLS
    proj = lambda blk: jnp.concatenate(
        [_dot(hn, win_ref[blk * per_blk + j]) for j in range(per_blk)], axis=1)
    zq = proj(0)
    zk = proj(1)
    zv = proj(2)
    sg = _silu(proj(3))
    xb = proj(4)
    gg = _gelu(proj(5))

    rowi = lax.broadcasted_iota(jnp.int32, (bt, RET_DK), 0)
    for h in range(RET_HEADS):
        cols = slice(h * RET_DK, (h + 1) * RET_DK)
        qs = (_rotary(zq[:, cols], cosf, sinf) * Q_SCALE).astype(BF16)
        kb = _rotary(zk[:, cols], cosf, sinf).astype(BF16)
        vh = zv[:, cols]
        qk = jnp.sum(qs.astype(F32) * kb.astype(F32), axis=-1, keepdims=True)
        cross = jnp.zeros((bt, RET_DK), F32)
        for b in range(bt):
            s_old = s0_ref[b, h]
            cr = _dot(qs, s_old.astype(BF16))
            cross = cross + jnp.where(rowi == b, cr, 0.0)
            u = lax.dot_general(kb, jnp.where(rowi == b, vh, 0.0).astype(BF16),
                                (((0,), (0,)), ((), ())), preferred_element_type=F32)
            s_ref[b, h] = gdec_ref[h] * s_old + u
        o = qk * vh + cross * gdec_ref[h]
        ymix_s[:, cols] = _group_norm(o) * gn_ref[:, cols] * sg[:, cols]

    xc = cbias_ref[...]
    for j in range(CONV_W - 1):
        xc = xc + cb0_ref[j] * cw_ref[j:j + 1, :]
    xc = xc + xb * cw_ref[CONV_W - 1:CONV_W, :]
    for j in range(CONV_W - 2):
        cT_ref[j] = cb0_ref[j + 1]
    cT_ref[CONV_W - 2] = xb
    a, b_in = _lru_coeffs(xc, wgate_ref, ba_ref[...], bx_ref[...], _softplus(-lam_ref[...]))
    hnew = a * h0_ref[...] + b_in
    hT_ref[...] = hnew
    ymix_s[:, RET_WIDTH:] = hnew * gg
    ymix = ymix_s[...]
    o_ref[...] = x + jnp.concatenate(
        [_dot(ymix, wout_ref[j]) for j in range(D_MODEL // PIECE_COLS)], axis=1)


def _mixa_sample(x, gdec, n1, win, cosf, sinf, s0, h0, cb0, gn, cw, cbias,
                 wgate, ba, bx, lam, wout, *, bt):
    batch = x.shape[0]
    in_w = 6 * RET_WIDTH
    est = (D_MODEL * in_w * 4 + D_MODEL * D_MODEL * 4
           + 4 * bt * RET_HEADS * RET_DK * RET_DK * 4 + 16 * bt * in_w * 4)
    rows = lambda i: (i, 0)
    return pl.pallas_call(
        functools.partial(_mixa_sample_kernel, bt=bt),
        grid=(batch // bt,),
        in_specs=[
            pl.BlockSpec(memory_space=pltpu.SMEM),
            pl.BlockSpec((bt, D_MODEL), rows),
            _resident((1, D_MODEL)),
            _resident((in_w // PIECE_COLS, D_MODEL, PIECE_COLS)),
            _resident((1, RET_DK)),
            _resident((1, RET_DK)),
            pl.BlockSpec((bt, RET_HEADS, RET_DK, RET_DK), lambda i: (i, 0, 0, 0)),
            pl.BlockSpec((bt, LRU_WIDTH), rows),
            pl.BlockSpec((CONV_W - 1, bt, LRU_WIDTH), lambda i: (0, i, 0)),
            _resident((1, RET_WIDTH)),
            _resident((CONV_W, LRU_WIDTH)),
            _resident((1, LRU_WIDTH)),
            _resident((LRU_WIDTH // LANES, LANES, 2 * LANES)),
            _resident((1, LRU_WIDTH)),
            _resident((1, LRU_WIDTH)),
            _resident((1, LRU_WIDTH)),
            _resident((D_MODEL // PIECE_COLS, D_MODEL, PIECE_COLS)),
        ],
        out_specs=[
            pl.BlockSpec((bt, D_MODEL), rows),
            pl.BlockSpec((bt, RET_HEADS, RET_DK, RET_DK), lambda i: (i, 0, 0, 0)),
            pl.BlockSpec((bt, LRU_WIDTH), rows),
            pl.BlockSpec((CONV_W - 1, bt, LRU_WIDTH), lambda i: (0, i, 0)),
        ],
        out_shape=[
            jax.ShapeDtypeStruct((batch, D_MODEL), F32),
            jax.ShapeDtypeStruct((batch, RET_HEADS, RET_DK, RET_DK), F32),
            jax.ShapeDtypeStruct((batch, LRU_WIDTH), F32),
            jax.ShapeDtypeStruct((CONV_W - 1, batch, LRU_WIDTH), F32),
        ],
        scratch_shapes=[pltpu.VMEM((bt, D_MODEL), F32)],
        compiler_params=pltpu.CompilerParams(
            dimension_semantics=("arbitrary",), vmem_limit_bytes=_vmem_limit(est)),
        name="mixa_sample",
    )(gdec, x, n1, win, cosf, sinf, s0, h0, cb0, gn, cw, cbias, wgate,
      ba, bx, lam, wout)


def _rope_tables(pos):
    half = RET_DK // 2
    inv = ROPE_BASE ** (-jnp.arange(half, dtype=F32) / half)
    ang = pos[:, None] * inv[None, :]
    cos, sin = jnp.cos(ang), jnp.sin(ang)
    return jnp.concatenate([cos, cos], axis=1), jnp.concatenate([-sin, sin], axis=1)


def _decay_tables(c):
    lg = jnp.log1p(-jnp.exp2(-5.0 - jnp.arange(RET_HEADS, dtype=F32)))
    idx = jnp.arange(c, dtype=F32)
    diff = idx[:, None] - idx[None, :]
    causal = diff >= 0
    dmat = jnp.where(causal[None], jnp.exp(jnp.where(causal, diff, 0.0)[None] * lg[:, None, None]), 0.0)
    zeta = jnp.exp((c - 1.0 - idx)[None, :] * lg[:, None])
    xi = jnp.exp((idx + 1.0)[None, :] * lg[:, None])
    gdec = jnp.exp(c * lg)
    return lg, dmat, zeta, xi, gdec


def _gate_tiles(wa, wx):
    z = jnp.zeros((LRU_BW, LRU_BW), wa.dtype)
    bd = lambda w, g: jnp.block([[w[2 * g], z], [z, w[2 * g + 1]]])
    tiles = [jnp.concatenate([bd(wa, g), bd(wx, g)], axis=1) for g in range(LRU_BLOCKS // 2)]
    return jnp.stack(tiles).astype(BF16)


def kernel(x_prompt, x_sample, state_ret, state_lru, state_conv, norm1, norm2, norm_f, w_in_a, ret_gn,
           conv_w, conv_b, lru_wa, lru_ba, lru_wx, lru_bx, lru_lambda, w_out_a, w_in_c, sg_norm_g,
           sg_norm_b, sg_ws, sg_bs, w_out_c, ffn_wg, ffn_wu, ffn_wd):
    bp, lp, _ = x_prompt.shape
    bs = x_sample.shape[0]
    row = lambda v: v.reshape(1, -1)

    slabs = lambda wmat: jnp.stack(
        [wmat[:, i * PIECE_COLS:(i + 1) * PIECE_COLS].astype(BF16)
         for i in range(wmat.shape[1] // PIECE_COLS)])
    win_a = slabs(w_in_a[0])
    wout_a = slabs(w_out_a[0])
    wgate = _gate_tiles(lru_wa[0], lru_wx[0])
    mixa_w = (row(ret_gn[0]), conv_w[0], row(conv_b[0]), wgate, row(lru_ba[0]),
              row(lru_bx[0]), row(lru_lambda[0]), wout_a)

    tm = 512
    bt = 16
    cos_p, sin_p = _rope_tables(jnp.arange(lp, dtype=F32))
    cos_s, sin_s = _rope_tables(PAST_LEN + jnp.arange(1, dtype=F32))
    _, dmat, zeta, xi, gdec_p = _decay_tables(RET_CHUNK)
    _, _, _, _, gdec_s = _decay_tables(1)
    bc = lambda t: jnp.broadcast_to(t[:, :, None], (RET_HEADS, RET_CHUNK, RET_DK))
    zeros = lambda *s: jnp.zeros(s, F32)
    xp = x_prompt.reshape(bp * lp, D_MODEL)
    xs = x_sample.reshape(bs, D_MODEL)

    xp, ret_p, lru_p, conv_p, wg, wu, wd, win_c, wout_c = _mixa_prompt(
        xp, gdec_p, row(norm1[0]), win_a, cos_p, sin_p, dmat, bc(zeta), bc(xi),
        zeros(bp, RET_HEADS, RET_DK, RET_DK), zeros(bp, 1, LRU_WIDTH),
        zeros(bp, CONV_W - 1, LRU_WIDTH), *mixa_w, ffn_wg, ffn_wu, ffn_wd, w_in_c[0],
        w_out_c[0], batch=bp, seq=lp, tm=tm)
    ffn = lambda xp_, xs_, layer, final: _ffn(
        xp_, xs_, row(norm2[layer]), wg, wu, wd, row(norm_f), layer=layer,
        final_norm=final, tm=tm)
    xs, ret_s, lru_s, conv_s = _mixa_sample(
        xs, gdec_s, row(norm1[0]), win_a, cos_s, sin_s, state_ret[0],
        state_lru[0], jnp.transpose(state_conv[0], (1, 0, 2)), *mixa_w, bt=bt)
    xp, xs = ffn(xp, xs, 0, False)

    bsb = jnp.broadcast_to(sg_bs[0][:, :, None], (SG_GROUPS, SG_CHUNK, SG_GW))
    xp = _mixc_prompt(xp, row(norm1[1]), win_c, row(sg_norm_g[0]), row(sg_norm_b[0]), sg_ws[0],
                      bsb, wout_c, tm=2 * tm)
    sg_scale = row(jnp.repeat(sg_ws[0][:, 0, 0], SG_GW))
    sg_bias = row(jnp.repeat(sg_bs[0][:, 0], SG_GW))
    xs, v_s = _mixc_sample(xs, row(norm1[1]), win_c, row(sg_norm_g[0]), row(sg_norm_b[0]),
                           sg_scale, sg_bias, wout_c)
    y_prompt, y_sample = ffn(xp, xs, 1, True)

    return (y_prompt.reshape(bp, lp, D_MODEL),
            y_sample.reshape(bs, 1, D_MODEL),
            ret_p[None],
            ret_s[None],
            lru_p.reshape(1, bp, LRU_WIDTH),
            lru_s[None],
            conv_p[None],
            jnp.transpose(conv_s, (1, 0, 2))[None],
            v_s.reshape(1, bs, 1, SG_HALF))
```

```python
import functools
import math

import jax
import jax.numpy as jnp
from jax import lax
from jax.experimental import pallas as pl
from jax.experimental.pallas import tpu as pltpu

F32 = jnp.float32
BF16 = jnp.bfloat16

D_MODEL = 1024
RET_WIDTH = 512
RET_HEADS = 4
RET_DK = 128
RET_CHUNK = 128
ROPE_BASE = 10000.0
LRU_WIDTH = 512
LRU_BLOCKS = 8
LRU_BW = 64
LRU_C = 8.0
CONV_W = 4
SG_CHUNK = 128
SG_HALF = 1024
SG_GROUPS = 8
SG_GW = 128
D_FF = 2816
EPS = 1e-6
PAST_LEN = 16384

V7X_VMEM_BYTES = 64 * 1024 * 1024
SUBLANES = 8
LANES = 128
V7X_MXU_COLS = 256
PIECE_COLS = V7X_MXU_COLS
PIECES_PER_STAGE = 1

Q_SCALE = RET_DK ** -0.5
GELU_C = math.sqrt(2.0 / math.pi)
LOG2_E = 1.0 / math.log(2.0)


def _vmem_limit(estimate_bytes):
    return int(min(estimate_bytes * 3 // 2 + (8 << 20), V7X_VMEM_BYTES - (6 << 20)))


def _resident(shape):
    nd = len(shape)
    return pl.BlockSpec(shape, lambda *_: (0,) * nd, pipeline_mode=pl.Buffered(1))


def _dot(a, b):
    return lax.dot_general(a, b, (((1,), (0,)), ((), ())), preferred_element_type=F32)


def _rms(x, g):
    ms = jnp.mean(x * x, axis=-1, keepdims=True)
    return x * lax.rsqrt(ms + EPS) * g


def _gelu(x):
    k1 = -2.0 * GELU_C * LOG2_E
    k3 = k1 * 0.044715
    return x / (1.0 + jnp.exp2(x * (k3 * (x * x) + k1)))


def _silu(x):
    return x * jax.nn.sigmoid(x)


def _softplus(x):
    return jnp.maximum(x, 0.0) + jnp.log1p(jnp.exp(-jnp.abs(x)))


def _sqrt_nonneg(y):
    return jnp.where(y > 0.0, y * lax.rsqrt(y), 0.0)


def _group_norm(o):
    mu = jnp.mean(o, axis=-1, keepdims=True)
    oc = o - mu
    var = jnp.mean(oc * oc, axis=-1, keepdims=True)
    return oc * lax.rsqrt(var + EPS)


def _rotary(x, cosf, sinf):
    return x * cosf + pltpu.roll(x, RET_DK // 2, 1) * sinf


def _lru_group(xc, wgate, ba, bx, sp):
    both = _dot(xc.astype(BF16), wgate)
    r = jax.nn.sigmoid(both[:, :LANES] + ba)
    i = jax.nn.sigmoid(both[:, LANES:] + bx)
    log_a = (-LRU_C * r) * sp
    a = jnp.exp(log_a)
    mult = _sqrt_nonneg(-jnp.tanh(log_a) * (a * a + 1.0))
    return a, xc * i * mult


def _lru_coeffs(xc, wgate_ref, ba, bx, sp):
    xcb = xc.astype(BF16)
    ra, rx = [], []
    for g in range(LRU_WIDTH // LANES):
        cols = slice(g * LANES, (g + 1) * LANES)
        both = _dot(xcb[:, cols], wgate_ref[g])
        ra.append(both[:, :LANES])
        rx.append(both[:, LANES:])
    r = jax.nn.sigmoid(jnp.concatenate(ra, axis=1) + ba)
    i = jax.nn.sigmoid(jnp.concatenate(rx, axis=1) + bx)
    log_a = (-LRU_C * r) * sp
    a = jnp.exp(log_a)
    mult = _sqrt_nonneg(-jnp.tanh(log_a) * (a * a + 1.0))
    return a, xc * i * mult


def _ffn_rows(x, n2, wg_ref, wu_ref, wd_ref, nf, final_norm):
    h = _rms(x, n2).astype(BF16)
    act = (_silu(_dot(h, wg_ref[...])) * _dot(h, wu_ref[...])).astype(BF16)
    y = x + _dot(act, wd_ref[...])
    return _rms(y, nf) if final_norm else y


def _ffn_kernel(x_ref, xs_ref, n2_ref, wg_ref, wu_ref, wd_ref, nf_ref, o_ref, os_ref, *,
                final_norm, prompt_steps):
    step = pl.program_id(0)
    args = (n2_ref[...], wg_ref, wu_ref, wd_ref, nf_ref[...], final_norm)

    @pl.when(step < prompt_steps)
    def _():
        o_ref[...] = _ffn_rows(x_ref[...], *args)

    @pl.when(step == prompt_steps)
    def _():
        os_ref[...] = _ffn_rows(xs_ref[...], *args)


def _ffn(x, xs, n2, wg, wu, wd, nf, *, layer, final_norm, tm):
    t, ts = x.shape[0], xs.shape[0]
    steps = t // tm
    est = (3 * D_MODEL * D_FF * 2 + 4 * tm * D_MODEL * 4 + 4 * ts * D_MODEL * 4
           + 3 * tm * D_FF * 4)
    layer_blk = lambda shape: pl.BlockSpec((None,) + shape, lambda i: (layer, 0, 0),
                                           pipeline_mode=pl.Buffered(1))
    prompt_blk = pl.BlockSpec((tm, D_MODEL), lambda i: (jnp.minimum(i, steps - 1), 0))
    return pl.pallas_call(
        functools.partial(_ffn_kernel, final_norm=final_norm, prompt_steps=steps),
        grid=(steps + 1,),
        in_specs=[
            prompt_blk,
            _resident((ts, D_MODEL)),
            _resident((1, D_MODEL)),
            layer_blk((D_MODEL, D_FF)),
            layer_blk((D_MODEL, D_FF)),
            layer_blk((D_FF, D_MODEL)),
            _resident((1, D_MODEL)),
        ],
        out_specs=[prompt_blk, pl.BlockSpec((ts, D_MODEL), lambda i: (0, 0))],
        out_shape=[jax.ShapeDtypeStruct((t, D_MODEL), F32),
                   jax.ShapeDtypeStruct((ts, D_MODEL), F32)],
        compiler_params=pltpu.CompilerParams(
            dimension_semantics=("arbitrary",), vmem_limit_bytes=_vmem_limit(est)),
        name="ffn_final" if final_norm else "ffn",
    )(x, xs, n2, wg, wu, wd, nf)


def _sgu_u(hn, win_ref):
    return _gelu(_dot(hn, win_ref[0]))


def _sgu_v(hn, win_ref, vg, vb):
    zv = _gelu(_dot(hn, win_ref[1]))
    mu = jnp.mean(zv, axis=-1, keepdims=True)
    vc = zv - mu
    var = jnp.mean(vc * vc, axis=-1, keepdims=True)
    return vc * lax.rsqrt(var + EPS) * vg + vb


def _mixc_prompt_kernel(x_ref, n1_ref, win_ref, vg_ref, vb_ref, ws_ref, bsb_ref, wout_ref,
                        o_ref, u_s, v_s, gated_s, wm_s, *, tm):
    half = tm // 2
    nchunk = half // SG_CHUNK
    ri = lax.broadcasted_iota(jnp.int32, (SG_CHUNK, SG_CHUNK), 0)
    ci = lax.broadcasted_iota(jnp.int32, (SG_CHUNK, SG_CHUNK), 1)
    for g in range(SG_GROUPS):
        wm_s[g] = jnp.where(ri >= ci, ws_ref[g], 0.0).astype(BF16)

    def norm(r0):
        return _rms(x_ref[r0:r0 + half, :], n1_ref[...])

    def put_u(r0, hn):
        u_s[r0:r0 + half, :] = _sgu_u(hn, win_ref)

    def put_v(r0, hn):
        v_s[r0:r0 + half, :] = _sgu_v(hn, win_ref, vg_ref[...], vb_ref[...]).astype(BF16)

    def gate(r0):
        for g in range(SG_GROUPS):
            cols = slice(g * SG_GW, (g + 1) * SG_GW)
            vcat = jnp.concatenate(
                [v_s[r0 + c * SG_CHUNK:r0 + (c + 1) * SG_CHUNK, cols] for c in range(nchunk)],
                axis=1)
            sv_all = _dot(wm_s[g], vcat)
            for c in range(nchunk):
                rows = slice(r0 + c * SG_CHUNK, r0 + (c + 1) * SG_CHUNK)
                sv = sv_all[:, c * SG_GW:(c + 1) * SG_GW] + bsb_ref[g]
                gated_s[rows, cols] = u_s[rows, cols] * sv

    def out_proj(r0):
        rows = slice(r0, r0 + half)
        o_ref[rows, :] = x_ref[rows, :] + _dot(gated_s[rows, :], wout_ref[...])

    hn_a = norm(0)
    put_u(0, hn_a)
    put_v(0, hn_a)
    hn_b = norm(half)
    put_u(half, hn_b)
    gate(0)
    put_v(half, hn_b)
    out_proj(0)
    gate(half)
    out_proj(half)


def _mixc_prompt(x, n1, win, vg, vb, ws, bsb, wout, *, tm):
    t = x.shape[0]
    est = 3 * D_MODEL * SG_HALF * 2 + 4 * tm * D_MODEL * 4 + 5 * tm * 2 * SG_HALF * 4
    return pl.pallas_call(
        functools.partial(_mixc_prompt_kernel, tm=tm),
        grid=(t // tm,),
        in_specs=[
            pl.BlockSpec((tm, D_MODEL), lambda i: (i, 0)),
            _resident((1, D_MODEL)),
            _resident((2, D_MODEL, SG_HALF)),
            _resident((1, SG_HALF)),
            _resident((1, SG_HALF)),
            _resident((SG_GROUPS, SG_CHUNK, SG_CHUNK)),
            _resident((SG_GROUPS, SG_CHUNK, SG_GW)),
            _resident((SG_HALF, D_MODEL)),
        ],
        out_specs=pl.BlockSpec((tm, D_MODEL), lambda i: (i, 0)),
        out_shape=jax.ShapeDtypeStruct((t, D_MODEL), F32),
        scratch_shapes=[
            pltpu.VMEM((tm, SG_HALF), F32),
            pltpu.VMEM((tm, SG_HALF), BF16),
            pltpu.VMEM((tm, SG_HALF), F32),
            pltpu.VMEM((SG_GROUPS, SG_CHUNK, SG_CHUNK), BF16),
        ],
        compiler_params=pltpu.CompilerParams(
            dimension_semantics=("arbitrary",), vmem_limit_bytes=_vmem_limit(est)),
        name="mixc_prompt",
    )(x, n1, win, vg, vb, ws, bsb, wout)


def _mixc_sample_kernel(x_ref, n1_ref, win_ref, vg_ref, vb_ref, scale_ref, bias_ref, wout_ref,
                        o_ref, v_ref):
    x = x_ref[...]
    hn = _rms(x, n1_ref[...])
    u = _sgu_u(hn, win_ref)
    vn = _sgu_v(hn, win_ref, vg_ref[...], vb_ref[...])
    v_ref[...] = vn
    sv = vn * scale_ref[...] + bias_ref[...]
    o_ref[...] = x + _dot(u * sv, wout_ref[...])


def _mixc_sample(x, n1, win, vg, vb, scale, bias, wout):
    t = x.shape[0]
    est = 3 * D_MODEL * SG_HALF * 2 + 8 * t * 2 * SG_HALF * 4
    return pl.pallas_call(
        _mixc_sample_kernel,
        grid=(1,),
        in_specs=[
            _resident((t, D_MODEL)),
            _resident((1, D_MODEL)),
            _resident((2, D_MODEL, SG_HALF)),
            _resident((1, SG_HALF)),
            _resident((1, SG_HALF)),
            _resident((1, SG_HALF)),
            _resident((1, SG_HALF)),
            _resident((SG_HALF, D_MODEL)),
        ],
        out_specs=[
            pl.BlockSpec((t, D_MODEL), lambda i: (0, 0)),
            pl.BlockSpec((t, SG_HALF), lambda i: (0, 0)),
        ],
        out_shape=[
            jax.ShapeDtypeStruct((t, D_MODEL), F32),
            jax.ShapeDtypeStruct((t, SG_HALF), F32),
        ],
        compiler_params=pltpu.CompilerParams(
            dimension_semantics=("arbitrary",), vmem_limit_bytes=_vmem_limit(est)),
        name="mixc_sample",
    )(x, n1, win, vg, vb, scale, bias, wout)


def _lru_scan(a, b, h0):
    tm, w = a.shape
    groups = tm // SUBLANES
    a3 = a.reshape(groups, SUBLANES, w)
    b3 = b.reshape(groups, SUBLANES, w)
    sub = lax.broadcasted_iota(jnp.int32, (groups, SUBLANES, w), 1)
    for s in (1, 2, 4):
        a_sh = pltpu.roll(a3, s, 1)
        b_sh = pltpu.roll(b3, s, 1)
        keep = sub >= s
        b3 = jnp.where(keep, a3 * b_sh + b3, b3)
        a3 = jnp.where(keep, a3 * a_sh, a3)
    hprev = jnp.broadcast_to(h0, (SUBLANES, w))
    hs = []
    for r in range(groups):
        hr = a3[r] * hprev + b3[r]
        hs.append(hr)
        hprev = jnp.broadcast_to(hr[SUBLANES - 1:SUBLANES, :], (SUBLANES, w))
    return jnp.concatenate(hs, axis=0), hprev[0:1, :]


def _mixa_prompt_kernel(gdec_ref, x_ref, xn_ref, n1_ref, win_ref, cos_ref, sin_ref, dmat_ref,
                        zeta_ref, xi_ref, s0_ref, h0_ref, cb0_ref, gn_ref, cw_ref, cbias_ref,
                        wgate_ref, ba_ref, bx_ref, lam_ref, wout_ref,
                        wg_ref, wu_ref, wd_ref, winc_ref, woutc_ref,
                        o_ref, s_ref, hT_ref, cT_ref,
                        wg_o, wu_o, wd_o, winc_o, woutc_o,
                        z_s, xbn_s, xp_s, ymix_s, sb_s, *, tm, nt):
    half = tm // 2
    w = RET_WIDTH
    pad = SUBLANES
    tail = CONV_W - 1
    step = pl.program_id(0)
    ZQ, ZK, ZV, ZG, ZGB = range(5)

    pw = PIECE_COLS
    per_blk = w // pw

    def stash_xb(j, v):
        xbn_s[:, j * pw:(j + 1) * pw] = v

    def window_xb(j, v):
        xp_s[pad + half:pad + tm, j * pw:(j + 1) * pw] = v

    def projection(src_ref, src_row0, z_row0, put_xb):
        hn = []

        def xb_piece(j):
            def run():
                if j == 0:
                    hn.append(
                        _rms(src_ref[src_row0:src_row0 + half, :], n1_ref[...]).astype(BF16))
                put_xb(j, _dot(hn[0], win_ref[4 * per_blk + j]))
            return run

        def z_piece(win_blk, z_blk, j):
            def run():
                z_s[z_row0:z_row0 + half, z_blk * w + j * pw:z_blk * w + (j + 1) * pw] = _dot(
                    hn[0], win_ref[win_blk * per_blk + j])
            return run

        pieces = [xb_piece(j) for j in range(per_blk)]
        for win_blk, z_blk in ((5, ZGB), (0, ZQ), (1, ZK), (2, ZV), (3, ZG)):
            pieces += [z_piece(win_blk, z_blk, j) for j in range(per_blk)]
        return pieces

    @pl.when(step == 0)
    def _():
        for piece in projection(x_ref, 0, 0, stash_xb):
            piece()

    @pl.when(lax.rem(step, nt) == 0)
    def _():
        s_ref[...] = s0_ref[...]
        hT_ref[...] = h0_ref[...]
        cT_ref[...] = cb0_ref[...]

    xp_s[pad - tail:pad, :] = cT_ref[0]
    xp_s[pad:pad + half, :] = xbn_s[...]
    for h in range(RET_HEADS):
        sb_s[h] = s_ref[0, h].astype(BF16)
    hprev = [hT_ref[0]]
    sp = _softplus(-lam_ref[...])
    heads = range(RET_HEADS)

    def lru_unit(c, emit):
        rows = slice(c * RET_CHUNK, (c + 1) * RET_CHUNK)
        xc = cbias_ref[...]
        for j in reversed(range(CONV_W)):
            off = pad - tail + j + c * RET_CHUNK
            xc = xc + xp_s[off:off + RET_CHUNK, :] * cw_ref[j:j + 1, :]
        emit()
        a, b_in = _lru_coeffs(xc, wgate_ref, ba_ref[...], bx_ref[...], sp)
        emit()
        hl, hprev[0] = _lru_scan(a, b_in, hprev[0])
        ymix_s[rows, RET_WIDTH:] = (hl * _gelu(z_s[rows, ZGB * w:(ZGB + 1) * w])).astype(BF16)

    def ret_unit(c, emit):
        rows = slice(c * RET_CHUNK, (c + 1) * RET_CHUNK)
        zcol = lambda blk, h: slice(blk * w + h * LANES, blk * w + (h + 1) * LANES)
        cols = lambda h: slice(h * RET_DK, (h + 1) * RET_DK)
        cosf = cos_ref[rows, :]
        sinf = sin_ref[rows, :]
        qr = [_rotary(z_s[rows, zcol(ZQ, h)], cosf, sinf) * Q_SCALE for h in heads]
        kr = [_rotary(z_s[rows, zcol(ZK, h)], cosf, sinf) for h in heads]
        vb = [z_s[rows, zcol(ZV, h)].astype(BF16) for h in heads]
        emit()
        sc = [lax.dot_general(qr[h].astype(BF16), kr[h].astype(BF16), (((1,), (1,)), ((), ())),
                              preferred_element_type=F32) for h in heads]
        emit()
        lhs = [jnp.concatenate([(sc[h] * dmat_ref[h]).astype(BF16),
                                (qr[h] * xi_ref[h]).astype(BF16)], axis=1) for h in heads]
        o = [_dot(lhs[h], jnp.concatenate([vb[h], sb_s[h]], axis=0)) for h in heads]
        u = [lax.dot_general((kr[h] * zeta_ref[h]).astype(BF16), vb[h], (((0,), (0,)), ((), ())),
                             preferred_element_type=F32) for h in heads]
        emit()
        for h in heads:
            ya = _group_norm(o[h]) * gn_ref[:, cols(h)] * _silu(z_s[rows, zcol(ZG, h)])
            ymix_s[rows, cols(h)] = ya.astype(BF16)
        emit()
        for h in heads:
            s_new = gdec_ref[h] * s_ref[0, h] + u[h]
            s_ref[0, h] = s_new
            sb_s[h] = s_new.astype(BF16)

    def out_proj(row0):
        rows = slice(row0, row0 + half)

        def piece(j):
            def run():
                cols = slice(j * pw, (j + 1) * pw)
                o_ref[rows, cols] = x_ref[rows, cols] + _dot(ymix_s[rows, :], wout_ref[j])
            return run

        return [piece(j) for j in range(D_MODEL // pw)]

    def run_half(first_chunk, pieces):
        pending = iter(pieces)

        def emit():
            for _ in range(PIECES_PER_STAGE):
                next(pending, lambda: None)()

        for c in range(first_chunk, first_chunk + half // RET_CHUNK):
            emit()
            lru_unit(c, emit)
            emit()
            ret_unit(c, emit)
        for piece in pending:
            piece()

    run_half(0, projection(x_ref, half, half, window_xb))
    run_half(half // RET_CHUNK, out_proj(0) + projection(xn_ref, 0, 0, stash_xb))
    for piece in out_proj(half):
        piece()

    hT_ref[0] = hprev[0]
    cT_ref[0] = xp_s[pad + tm - tail:pad + tm, :]

    wg_o[...] = wg_ref[...].astype(BF16)
    wu_o[...] = wu_ref[...].astype(BF16)
    wd_o[...] = wd_ref[...].astype(BF16)
    for i in range(2):
        winc_o[i] = winc_ref[:, i * SG_HALF:(i + 1) * SG_HALF].astype(BF16)
    woutc_o[...] = woutc_ref[...].astype(BF16)


def _mixa_prompt(x, gdec, n1, win, cosf, sinf, dmat, zeta, xi, s0, h0, cb0, gn, cw, cbias,
                 wgate, ba, bx, lam, wout, wg, wu, wd, winc, woutc, *, batch, seq, tm):
    nt = seq // tm
    steps = batch * nt
    nlayer = wg.shape[0]
    per_layer = steps // nlayer
    ff_rows = D_MODEL // per_layer
    fd_rows = D_FF // per_layer
    c_rows = D_MODEL // steps
    band = lambda i: (i // per_layer, lax.rem(i, per_layer), 0)
    half = tm // 2
    in_w = 6 * RET_WIDTH
    est = (D_MODEL * in_w * 4 + D_MODEL * D_MODEL * 4 + 6 * tm * D_MODEL * 4
           + tm * 5 * RET_WIDTH * 4 + 4 * tm * LRU_WIDTH * 4)
    row_blk = lambda i: (i, 0)
    next_half = lambda i: (jnp.minimum(2 * (i + 1), 2 * steps - 2), 0)
    pos_blk = lambda i: (lax.rem(i, nt), 0)
    per_b4 = lambda i: (i // nt, 0, 0, 0)
    per_b3 = lambda i: (i // nt, 0, 0)
    hd = (RET_HEADS, RET_CHUNK, RET_CHUNK)
    return pl.pallas_call(
        functools.partial(_mixa_prompt_kernel, tm=tm, nt=nt),
        grid=(steps,),
        in_specs=[
            pl.BlockSpec(memory_space=pltpu.SMEM),
            pl.BlockSpec((tm, D_MODEL), row_blk),
            pl.BlockSpec((half, D_MODEL), next_half),
            _resident((1, D_MODEL)),
            _resident((in_w // PIECE_COLS, D_MODEL, PIECE_COLS)),
            pl.BlockSpec((tm, RET_DK), pos_blk),
            pl.BlockSpec((tm, RET_DK), pos_blk),
            _resident(hd),
            _resident(hd),
            _resident(hd),
            pl.BlockSpec((1, RET_HEADS, RET_DK, RET_DK), per_b4),
            pl.BlockSpec((1, 1, LRU_WIDTH), per_b3),
            pl.BlockSpec((1, CONV_W - 1, LRU_WIDTH), per_b3),
            _resident((1, RET_WIDTH)),
            _resident((CONV_W, LRU_WIDTH)),
            _resident((1, LRU_WIDTH)),
            _resident((LRU_WIDTH // LANES, LANES, 2 * LANES)),
            _resident((1, LRU_WIDTH)),
            _resident((1, LRU_WIDTH)),
            _resident((1, LRU_WIDTH)),
            _resident((D_MODEL // PIECE_COLS, D_MODEL, PIECE_COLS)),
            pl.BlockSpec((1, ff_rows, D_FF), band),
            pl.BlockSpec((1, ff_rows, D_FF), band),
            pl.BlockSpec((1, fd_rows, D_MODEL), band),
            pl.BlockSpec((c_rows, 2 * SG_HALF), row_blk),
            pl.BlockSpec((c_rows, D_MODEL), row_blk),
        ],
        out_specs=[
            pl.BlockSpec((tm, D_MODEL), row_blk),
            pl.BlockSpec((1, RET_HEADS, RET_DK, RET_DK), per_b4),
            pl.BlockSpec((1, 1, LRU_WIDTH), per_b3),
            pl.BlockSpec((1, CONV_W - 1, LRU_WIDTH), per_b3),
            pl.BlockSpec((1, ff_rows, D_FF), band),
            pl.BlockSpec((1, ff_rows, D_FF), band),
            pl.BlockSpec((1, fd_rows, D_MODEL), band),
            pl.BlockSpec((2, c_rows, SG_HALF), lambda i: (0, i, 0)),
            pl.BlockSpec((c_rows, D_MODEL), row_blk),
        ],
        out_shape=[
            jax.ShapeDtypeStruct((batch * seq, D_MODEL), F32),
            jax.ShapeDtypeStruct((batch, RET_HEADS, RET_DK, RET_DK), F32),
            jax.ShapeDtypeStruct((batch, 1, LRU_WIDTH), F32),
            jax.ShapeDtypeStruct((batch, CONV_W - 1, LRU_WIDTH), F32),
            jax.ShapeDtypeStruct(wg.shape, BF16),
            jax.ShapeDtypeStruct(wu.shape, BF16),
            jax.ShapeDtypeStruct(wd.shape, BF16),
            jax.ShapeDtypeStruct((2, D_MODEL, SG_HALF), BF16),
            jax.ShapeDtypeStruct(woutc.shape, BF16),
        ],
        scratch_shapes=[
            pltpu.VMEM((tm, 5 * RET_WIDTH), F32),
            pltpu.VMEM((half, LRU_WIDTH), F32),
            pltpu.VMEM((tm + 2 * SUBLANES, LRU_WIDTH), F32),
            pltpu.VMEM((tm, D_MODEL), BF16),
            pltpu.VMEM((RET_HEADS, RET_DK, RET_DK), BF16),
        ],
        compiler_params=pltpu.CompilerParams(
            dimension_semantics=("arbitrary",), vmem_limit_bytes=_vmem_limit(est)),
        name="mixa_prompt",
    )(gdec, x, x, n1, win, cosf, sinf, dmat, zeta, xi, s0, h0, cb0, gn, cw, cbias, wgate,
      ba, bx, lam, wout, wg, wu, wd, winc, woutc)


def _mixa_sample_kernel(gdec_ref, x_ref, n1_ref, win_ref, cos_ref, sin_ref,
                        s0_ref, h0_ref, cb0_ref, gn_ref, cw_ref, cbias_ref, wgate_ref,
                        ba_ref, bx_ref, lam_ref, wout_ref,
                        o_ref, s_ref, hT_ref, cT_ref, ymix_s, *, bt):
    x = x_ref[...]
    hn = _rms(x, n1_ref[...])
    cosf = cos_ref[...]
    sinf = sin_ref[...]
    w = RET_WIDTH
    per_blk = w // PIECE_COLS
    proj = lambda blk: jnp.concatenate(
        [_dot(hn, win_ref[blk * per_blk + j]) for j in range(per_blk)], axis=1)
    zq = proj(0)
    zk = proj(1)
    zv = proj(2)
    sg = _silu(proj(3))
    xb = proj(4)
    gg = _gelu(proj(5))

    rowi = lax.broadcasted_iota(jnp.int32, (bt, RET_DK), 0)
    for h in range(RET_HEADS):
        cols = slice(h * RET_DK, (h + 1) * RET_DK)
        qs = (_rotary(zq[:, cols], cosf, sinf) * Q_SCALE).astype(BF16)
        kb = _rotary(zk[:, cols], cosf, sinf).astype(BF16)
        vh = zv[:, cols]
        qk = jnp.sum(qs.astype(F32) * kb.astype(F32), axis=-1, keepdims=True)
        cross = jnp.zeros((bt, RET_DK), F32)
        for b in range(bt):
            s_old = s0_ref[b, h]
            cr = _dot(qs, s_old.astype(BF16))
            cross = cross + jnp.where(rowi == b, cr, 0.0)
            u = lax.dot_general(kb, jnp.where(rowi == b, vh, 0.0).astype(BF16),
                                (((0,), (0,)), ((), ())), preferred_element_type=F32)
            s_ref[b, h] = gdec_ref[h] * s_old + u
        o = qk * vh + cross * gdec_ref[h]
        ymix_s[:, cols] = _group_norm(o) * gn_ref[:, cols] * sg[:, cols]

    xc = cbias_ref[...]
    for j in range(CONV_W - 1):
        xc = xc + cb0_ref[j] * cw_ref[j:j + 1, :]
    xc = xc + xb * cw_ref[CONV_W - 1:CONV_W, :]
    for j in range(CONV_W - 2):
        cT_ref[j] = cb0_ref[j + 1]
    cT_ref[CONV_W - 2] = xb
    a, b_in = _lru_coeffs(xc, wgate_ref, ba_ref[...], bx_ref[...], _softplus(-lam_ref[...]))
    hnew = a * h0_ref[...] + b_in
    hT_ref[...] = hnew
    ymix_s[:, RET_WIDTH:] = hnew * gg
    ymix = ymix_s[...]
    o_ref[...] = x + jnp.concatenate(
        [_dot(ymix, wout_ref[j]) for j in range(D_MODEL // PIECE_COLS)], axis=1)


def _mixa_sample(x, gdec, n1, win, cosf, sinf, s0, h0, cb0, gn, cw, cbias,
                 wgate, ba, bx, lam, wout, *, bt):
    batch = x.shape[0]
    in_w = 6 * RET_WIDTH
    est = (D_MODEL * in_w * 4 + D_MODEL * D_MODEL * 4
           + 4 * bt * RET_HEADS * RET_DK * RET_DK * 4 + 16 * bt * in_w * 4)
    rows = lambda i: (i, 0)
    return pl.pallas_call(
        functools.partial(_mixa_sample_kernel, bt=bt),
        grid=(batch // bt,),
        in_specs=[
            pl.BlockSpec(memory_space=pltpu.SMEM),
            pl.BlockSpec((bt, D_MODEL), rows),
            _resident((1, D_MODEL)),
            _resident((in_w // PIECE_COLS, D_MODEL, PIECE_COLS)),
            _resident((1, RET_DK)),
            _resident((1, RET_DK)),
            pl.BlockSpec((bt, RET_HEADS, RET_DK, RET_DK), lambda i: (i, 0, 0, 0)),
            pl.BlockSpec((bt, LRU_WIDTH), rows),
            pl.BlockSpec((CONV_W - 1, bt, LRU_WIDTH), lambda i: (0, i, 0)),
            _resident((1, RET_WIDTH)),
            _resident((CONV_W, LRU_WIDTH)),
            _resident((1, LRU_WIDTH)),
            _resident((LRU_WIDTH // LANES, LANES, 2 * LANES)),
            _resident((1, LRU_WIDTH)),
            _resident((1, LRU_WIDTH)),
            _resident((1, LRU_WIDTH)),
            _resident((D_MODEL // PIECE_COLS, D_MODEL, PIECE_COLS)),
        ],
        out_specs=[
            pl.BlockSpec((bt, D_MODEL), rows),
            pl.BlockSpec((bt, RET_HEADS, RET_DK, RET_DK), lambda i: (i, 0, 0, 0)),
            pl.BlockSpec((bt, LRU_WIDTH), rows),
            pl.BlockSpec((CONV_W - 1, bt, LRU_WIDTH), lambda i: (0, i, 0)),
        ],
        out_shape=[
            jax.ShapeDtypeStruct((batch, D_MODEL), F32),
            jax.ShapeDtypeStruct((batch, RET_HEADS, RET_DK, RET_DK), F32),
            jax.ShapeDtypeStruct((batch, LRU_WIDTH), F32),
            jax.ShapeDtypeStruct((CONV_W - 1, batch, LRU_WIDTH), F32),
        ],
        scratch_shapes=[pltpu.VMEM((bt, D_MODEL), F32)],
        compiler_params=pltpu.CompilerParams(
            dimension_semantics=("arbitrary",), vmem_limit_bytes=_vmem_limit(est)),
        name="mixa_sample",
    )(gdec, x, n1, win, cosf, sinf, s0, h0, cb0, gn, cw, cbias, wgate,
      ba, bx, lam, wout)


def _rope_tables(pos):
    half = RET_DK // 2
    inv = ROPE_BASE ** (-jnp.arange(half, dtype=F32) / half)
    ang = pos[:, None] * inv[None, :]
    cos, sin = jnp.cos(ang), jnp.sin(ang)
    return jnp.concatenate([cos, cos], axis=1), jnp.concatenate([-sin, sin], axis=1)


def _decay_tables(c):
    lg = jnp.log1p(-jnp.exp2(-5.0 - jnp.arange(RET_HEADS, dtype=F32)))
    idx = jnp.arange(c, dtype=F32)
    diff = idx[:, None] - idx[None, :]
    causal = diff >= 0
    dmat = jnp.where(causal[None], jnp.exp(jnp.where(causal, diff, 0.0)[None] * lg[:, None, None]), 0.0)
    zeta = jnp.exp((c - 1.0 - idx)[None, :] * lg[:, None])
    xi = jnp.exp((idx + 1.0)[None, :] * lg[:, None])
    gdec = jnp.exp(c * lg)
    return lg, dmat, zeta, xi, gdec


def _gate_tiles(wa, wx):
    z = jnp.zeros((LRU_BW, LRU_BW), wa.dtype)
    bd = lambda w, g: jnp.block([[w[2 * g], z], [z, w[2 * g + 1]]])
    tiles = [jnp.concatenate([bd(wa, g), bd(wx, g)], axis=1) for g in range(LRU_BLOCKS // 2)]
    return jnp.stack(tiles).astype(BF16)


def kernel(x_prompt, x_sample, state_ret, state_lru, state_conv, norm1, norm2, norm_f, w_in_a, ret_gn,
           conv_w, conv_b, lru_wa, lru_ba, lru_wx, lru_bx, lru_lambda, w_out_a, w_in_c, sg_norm_g,
           sg_norm_b, sg_ws, sg_bs, w_out_c, ffn_wg, ffn_wu, ffn_wd):
    bp, lp, _ = x_prompt.shape
    bs = x_sample.shape[0]
    row = lambda v: v.reshape(1, -1)

    slabs = lambda wmat: jnp.stack(
        [wmat[:, i * PIECE_COLS:(i + 1) * PIECE_COLS].astype(BF16)
         for i in range(wmat.shape[1] // PIECE_COLS)])
    win_a = slabs(w_in_a[0])
    wout_a = slabs(w_out_a[0])
    wgate = _gate_tiles(lru_wa[0], lru_wx[0])
    mixa_w = (row(ret_gn[0]), conv_w[0], row(conv_b[0]), wgate, row(lru_ba[0]),
              row(lru_bx[0]), row(lru_lambda[0]), wout_a)

    tm = 512
    bt = 16
    cos_p, sin_p = _rope_tables(jnp.arange(lp, dtype=F32))
    cos_s, sin_s = _rope_tables(PAST_LEN + jnp.arange(1, dtype=F32))
    _, dmat, zeta, xi, gdec_p = _decay_tables(RET_CHUNK)
    _, _, _, _, gdec_s = _decay_tables(1)
    bc = lambda t: jnp.broadcast_to(t[:, :, None], (RET_HEADS, RET_CHUNK, RET_DK))
    zeros = lambda *s: jnp.zeros(s, F32)
    xp = x_prompt.reshape(bp * lp, D_MODEL)
    xs = x_sample.reshape(bs, D_MODEL)

    xp, ret_p, lru_p, conv_p, wg, wu, wd, win_c, wout_c = _mixa_prompt(
        xp, gdec_p, row(norm1[0]), win_a, cos_p, sin_p, dmat, bc(zeta), bc(xi),
        zeros(bp, RET_HEADS, RET_DK, RET_DK), zeros(bp, 1, LRU_WIDTH),
        zeros(bp, CONV_W - 1, LRU_WIDTH), *mixa_w, ffn_wg, ffn_wu, ffn_wd, w_in_c[0],
        w_out_c[0], batch=bp, seq=lp, tm=tm)
    ffn = lambda xp_, xs_, layer, final: _ffn(
        xp_, xs_, row(norm2[layer]), wg, wu, wd, row(norm_f), layer=layer,
        final_norm=final, tm=tm)
    xs, ret_s, lru_s, conv_s = _mixa_sample(
        xs, gdec_s, row(norm1[0]), win_a, cos_s, sin_s, state_ret[0],
        state_lru[0], jnp.transpose(state_conv[0], (1, 0, 2)), *mixa_w, bt=bt)
    xp, xs = ffn(xp, xs, 0, False)

    bsb = jnp.broadcast_to(sg_bs[0][:, :, None], (SG_GROUPS, SG_CHUNK, SG_GW))
    xp = _mixc_prompt(xp, row(norm1[1]), win_c, row(sg_norm_g[0]), row(sg_norm_b[0]), sg_ws[0],
                      bsb, wout_c, tm=2 * tm)
    sg_scale = row(jnp.repeat(sg_ws[0][:, 0, 0], SG_GW))
    sg_bias = row(jnp.repeat(sg_bs[0][:, 0], SG_GW))
    xs, v_s = _mixc_sample(xs, row(norm1[1]), win_c, row(sg_norm_g[0]), row(sg_norm_b[0]),
                           sg_scale, sg_bias, wout_c)
    y_prompt, y_sample = ffn(xp, xs, 1, True)

    return (y_prompt.reshape(bp, lp, D_MODEL),
            y_sample.reshape(bs, 1, D_MODEL),
            ret_p[None],
            ret_s[None],
            lru_p.reshape(1, bp, LRU_WIDTH),
            lru_s[None],
            conv_p[None],
            jnp.transpose(conv_s, (1, 0, 2))[None],
            v_s.reshape(1, bs, 1, SG_HALF))
```

```python
import functools
import math

import jax
import jax.numpy as jnp
from jax import lax
from jax.experimental import pallas as pl
from jax.experimental.pallas import tpu as pltpu

F32 = jnp.float32
BF16 = jnp.bfloat16

D_MODEL = 1024
RET_WIDTH = 512
RET_HEADS = 4
RET_DK = 128
RET_CHUNK = 128
ROPE_BASE = 10000.0
LRU_WIDTH = 512
LRU_BLOCKS = 8
LRU_BW = 64
LRU_C = 8.0
CONV_W = 4
SG_CHUNK = 128
SG_HALF = 1024
SG_GROUPS = 8
SG_GW = 128
D_FF = 2816
EPS = 1e-6
PAST_LEN = 16384

V7X_VMEM_BYTES = 64 * 1024 * 1024
SUBLANES = 8
LANES = 128
V7X_MXU_COLS = 256
PIECE_COLS = V7X_MXU_COLS
PIECES_PER_STAGE = 1

Q_SCALE = RET_DK ** -0.5
GELU_C = math.sqrt(2.0 / math.pi)
LOG2_E = 1.0 / math.log(2.0)


def _vmem_limit(estimate_bytes):
    return int(min(estimate_bytes * 3 // 2 + (8 << 20), V7X_VMEM_BYTES - (6 << 20)))


def _resident(shape):
    nd = len(shape)
    return pl.BlockSpec(shape, lambda *_: (0,) * nd, pipeline_mode=pl.Buffered(1))


def _dot(a, b):
    return lax.dot_general(a, b, (((1,), (0,)), ((), ())), preferred_element_type=F32)


def _rms(x, g):
    ms = jnp.mean(x * x, axis=-1, keepdims=True)
    return x * lax.rsqrt(ms + EPS) * g


def _gelu(x):
    k1 = -2.0 * GELU_C * LOG2_E
    k3 = k1 * 0.044715
    return x / (1.0 + jnp.exp2(x * (k3 * (x * x) + k1)))


def _silu(x):
    return x * jax.nn.sigmoid(x)


def _softplus(x):
    return jnp.maximum(x, 0.0) + jnp.log1p(jnp.exp(-jnp.abs(x)))


def _sqrt_nonneg(y):
    return jnp.where(y > 0.0, y * lax.rsqrt(y), 0.0)


def _group_norm(o):
    mu = jnp.mean(o, axis=-1, keepdims=True)
    oc = o - mu
    var = jnp.mean(oc * oc, axis=-1, keepdims=True)
    return oc * lax.rsqrt(var + EPS)


def _rotary(x, cosf, sinf):
    return x * cosf + pltpu.roll(x, RET_DK // 2, 1) * sinf


def _lru_coeffs(xc, wgate_ref, ba, bx, spc):
    xcb = xc.astype(BF16)
    ra, rx = [], []
    for g in range(LRU_WIDTH // LANES):
        cols = slice(g * LANES, (g + 1) * LANES)
        both = _dot(xcb[:, cols], wgate_ref[g])
        ra.append(both[:, :LANES])
        rx.append(both[:, LANES:])
    r = jax.nn.sigmoid(jnp.concatenate(ra, axis=1) + ba)
    i = jax.nn.sigmoid(jnp.concatenate(rx, axis=1) + bx)
    nl = r * spc
    a = jnp.exp2(nl * (-LOG2_E))
    mult = _sqrt_nonneg(jnp.tanh(nl) * (a * a + 1.0))
    return a, xc * i * mult


def _ffn_rows(x, n2, wg_ref, wu_ref, wd_ref, nf, final_norm):
    h = _rms(x, n2).astype(BF16)
    act = (_silu(_dot(h, wg_ref[...])) * _dot(h, wu_ref[...])).astype(BF16)
    y = x + _dot(act, wd_ref[...])
    return _rms(y, nf) if final_norm else y


def _exact_zero_from(v, width):
    bits = pltpu.bitcast(v, jnp.uint32)
    z = lax.shift_right_logical(lax.shift_right_logical(bits, jnp.uint32(16)), jnp.uint32(16))
    folded = z[:, :width]
    for c0 in range(width, v.shape[1], width):
        folded = folded | z[:, c0:c0 + width]
    return pltpu.bitcast(folded, F32)


def _ffn_kernel(x_ref, xn_ref, xs_ref, n2_ref, wg_ref, wu_ref, wd_ref, nf_ref, o_ref, os_ref,
                h_s, *, final_norm, prompt_steps):
    step = pl.program_id(0)
    args = (n2_ref[...], wg_ref, wu_ref, wd_ref, nf_ref[...], final_norm)
    slab = V7X_MXU_COLS

    @pl.when(step == 0)
    def _():
        h_s[...] = _rms(x_ref[...], n2_ref[...]).astype(BF16)

    @pl.when(step < prompt_steps)
    def _():
        h = h_s[...]
        act = _silu(_dot(h, wg_ref[...])) * _dot(h, wu_ref[...])
        nxt = _rms(xn_ref[...], n2_ref[...])
        h_s[...] = nxt.astype(BF16)
        head = act[:, :D_FF - slab].astype(BF16)
        last = (act[:, D_FF - slab:] + _exact_zero_from(nxt, slab)).astype(BF16)
        y = (x_ref[...] + _dot(head, wd_ref[:D_FF - slab, :])) + _dot(last, wd_ref[D_FF - slab:, :])
        o_ref[...] = _rms(y, nf_ref[...]) if final_norm else y

    @pl.when(step == prompt_steps)
    def _():
        os_ref[...] = _ffn_rows(xs_ref[...], *args)


def _ffn(x, xs, n2, wg, wu, wd, nf, *, layer, final_norm, tm):
    t, ts = x.shape[0], xs.shape[0]
    steps = t // tm
    est = (3 * D_MODEL * D_FF * 2 + 7 * tm * D_MODEL * 4 + 4 * ts * D_MODEL * 4
           + 3 * tm * D_FF * 4)
    layer_blk = lambda shape: pl.BlockSpec((None,) + shape, lambda i: (layer, 0, 0),
                                           pipeline_mode=pl.Buffered(1))
    prompt_blk = pl.BlockSpec((tm, D_MODEL), lambda i: (jnp.minimum(i, steps - 1), 0))
    next_blk = pl.BlockSpec((tm, D_MODEL), lambda i: (jnp.minimum(i + 1, steps - 1), 0))
    return pl.pallas_call(
        functools.partial(_ffn_kernel, final_norm=final_norm, prompt_steps=steps),
        grid=(steps + 1,),
        in_specs=[
            prompt_blk,
            next_blk,
            _resident((ts, D_MODEL)),
            _resident((1, D_MODEL)),
            layer_blk((D_MODEL, D_FF)),
            layer_blk((D_MODEL, D_FF)),
            layer_blk((D_FF, D_MODEL)),
            _resident((1, D_MODEL)),
        ],
        out_specs=[prompt_blk, pl.BlockSpec((ts, D_MODEL), lambda i: (0, 0))],
        out_shape=[jax.ShapeDtypeStruct((t, D_MODEL), F32),
                   jax.ShapeDtypeStruct((ts, D_MODEL), F32)],
        scratch_shapes=[pltpu.VMEM((tm, D_MODEL), BF16)],
        compiler_params=pltpu.CompilerParams(
            dimension_semantics=("arbitrary",), vmem_limit_bytes=_vmem_limit(est)),
        name="ffn_final" if final_norm else "ffn",
    )(x, x, xs, n2, wg, wu, wd, nf)


def _sgu_u(hn, win_ref):
    return _gelu(_dot(hn, win_ref[0]))


def _sgu_v(hn, win_ref, vg, vb):
    zv = _gelu(_dot(hn, win_ref[1]))
    mu = jnp.mean(zv, axis=-1, keepdims=True)
    vc = zv - mu
    var = jnp.mean(vc * vc, axis=-1, keepdims=True)
    return vc * lax.rsqrt(var + EPS) * vg + vb


def _mixc_prompt_kernel(x_ref, n1_ref, win_ref, vg_ref, vb_ref, ws_ref, bsb_ref, wout_ref,
                        o_ref, u_s, v_s, gated_s, wm_s, *, tm):
    half = tm // 2
    nchunk = half // SG_CHUNK
    ri = lax.broadcasted_iota(jnp.int32, (SG_CHUNK, SG_CHUNK), 0)
    ci = lax.broadcasted_iota(jnp.int32, (SG_CHUNK, SG_CHUNK), 1)
    for g in range(SG_GROUPS):
        wm_s[g] = jnp.where(ri >= ci, ws_ref[g], 0.0).astype(BF16)

    def norm(r0):
        return _rms(x_ref[r0:r0 + half, :], n1_ref[...])

    def put_u(r0, hn):
        u_s[r0:r0 + half, :] = _sgu_u(hn, win_ref)

    def put_v(r0, hn):
        v_s[r0:r0 + half, :] = _sgu_v(hn, win_ref, vg_ref[...], vb_ref[...]).astype(BF16)

    def gate(r0):
        for g in range(SG_GROUPS):
            cols = slice(g * SG_GW, (g + 1) * SG_GW)
            vcat = jnp.concatenate(
                [v_s[r0 + c * SG_CHUNK:r0 + (c + 1) * SG_CHUNK, cols] for c in range(nchunk)],
                axis=1)
            sv_all = _dot(wm_s[g], vcat)
            for c in range(nchunk):
                rows = slice(r0 + c * SG_CHUNK, r0 + (c + 1) * SG_CHUNK)
                sv = sv_all[:, c * SG_GW:(c + 1) * SG_GW] + bsb_ref[g]
                gated_s[rows, cols] = u_s[rows, cols] * sv

    def out_proj(r0):
        rows = slice(r0, r0 + half)
        o_ref[rows, :] = x_ref[rows, :] + _dot(gated_s[rows, :], wout_ref[...])

    hn_a = norm(0)
    put_u(0, hn_a)
    put_v(0, hn_a)
    hn_b = norm(half)
    put_u(half, hn_b)
    gate(0)
    put_v(half, hn_b)
    out_proj(0)
    gate(half)
    out_proj(half)


def _mixc_prompt(x, n1, win, vg, vb, ws, bsb, wout, *, tm):
    t = x.shape[0]
    est = 3 * D_MODEL * SG_HALF * 2 + 4 * tm * D_MODEL * 4 + 5 * tm * 2 * SG_HALF * 4
    return pl.pallas_call(
        functools.partial(_mixc_prompt_kernel, tm=tm),
        grid=(t // tm,),
        in_specs=[
            pl.BlockSpec((tm, D_MODEL), lambda i: (i, 0)),
            _resident((1, D_MODEL)),
            _resident((2, D_MODEL, SG_HALF)),
            _resident((1, SG_HALF)),
            _resident((1, SG_HALF)),
            _resident((SG_GROUPS, SG_CHUNK, SG_CHUNK)),
            _resident((SG_GROUPS, SG_CHUNK, SG_GW)),
            _resident((SG_HALF, D_MODEL)),
        ],
        out_specs=pl.BlockSpec((tm, D_MODEL), lambda i: (i, 0)),
        out_shape=jax.ShapeDtypeStruct((t, D_MODEL), F32),
        scratch_shapes=[
            pltpu.VMEM((tm, SG_HALF), F32),
            pltpu.VMEM((tm, SG_HALF), BF16),
            pltpu.VMEM((tm, SG_HALF), F32),
            pltpu.VMEM((SG_GROUPS, SG_CHUNK, SG_CHUNK), BF16),
        ],
        compiler_params=pltpu.CompilerParams(
            dimension_semantics=("arbitrary",), vmem_limit_bytes=_vmem_limit(est)),
        name="mixc_prompt",
    )(x, n1, win, vg, vb, ws, bsb, wout)


def _mixc_sample_kernel(x_ref, n1_ref, win_ref, vg_ref, vb_ref, scale_ref, bias_ref, wout_ref,
                        o_ref, v_ref):
    x = x_ref[...]
    hn = _rms(x, n1_ref[...])
    u = _sgu_u(hn, win_ref)
    vn = _sgu_v(hn, win_ref, vg_ref[...], vb_ref[...])
    v_ref[...] = vn
    sv = vn * scale_ref[...] + bias_ref[...]
    o_ref[...] = x + _dot(u * sv, wout_ref[...])


def _mixc_sample(x, n1, win, vg, vb, scale, bias, wout):
    t = x.shape[0]
    est = 3 * D_MODEL * SG_HALF * 2 + 8 * t * 2 * SG_HALF * 4
    return pl.pallas_call(
        _mixc_sample_kernel,
        grid=(1,),
        in_specs=[
            _resident((t, D_MODEL)),
            _resident((1, D_MODEL)),
            _resident((2, D_MODEL, SG_HALF)),
            _resident((1, SG_HALF)),
            _resident((1, SG_HALF)),
            _resident((1, SG_HALF)),
            _resident((1, SG_HALF)),
            _resident((SG_HALF, D_MODEL)),
        ],
        out_specs=[
            pl.BlockSpec((t, D_MODEL), lambda i: (0, 0)),
            pl.BlockSpec((t, SG_HALF), lambda i: (0, 0)),
        ],
        out_shape=[
            jax.ShapeDtypeStruct((t, D_MODEL), F32),
            jax.ShapeDtypeStruct((t, SG_HALF), F32),
        ],
        compiler_params=pltpu.CompilerParams(
            dimension_semantics=("arbitrary",), vmem_limit_bytes=_vmem_limit(est)),
        name="mixc_sample",
    )(x, n1, win, vg, vb, scale, bias, wout)


def _lru_scan(a, b, h0):
    tm, w = a.shape
    groups = tm // SUBLANES
    a3 = a.reshape(groups, SUBLANES, w)
    b3 = b.reshape(groups, SUBLANES, w)
    sub = lax.broadcasted_iota(jnp.int32, (groups, SUBLANES, w), 1)
    for s in (1, 2, 4):
        a_sh = pltpu.roll(a3, s, 1)
        b_sh = pltpu.roll(b3, s, 1)
        keep = sub >= s
        b3 = jnp.where(keep, a3 * b_sh + b3, b3)
        a3 = jnp.where(keep, a3 * a_sh, a3)
    hprev = jnp.broadcast_to(h0, (SUBLANES, w))
    hs = []
    for r in range(groups):
        hr = a3[r] * hprev + b3[r]
        hs.append(hr)
        hprev = jnp.broadcast_to(hr[SUBLANES - 1:SUBLANES, :], (SUBLANES, w))
    return jnp.concatenate(hs, axis=0), hprev[0:1, :]


def _mixa_prompt_kernel(gdec_ref, x_ref, xn_ref, n1_ref, win_ref, cos_ref, sin_ref, dmat_ref,
                        zeta_ref, xi_ref, gn_ref, cw_ref, cbias_ref,
                        wgate_ref, ba_ref, bx_ref, lam_ref, wout_ref,
                        wg_ref, wu_ref, wd_ref, winc_ref, woutc_ref,
                        o_ref, s_ref, hT_ref, cT_ref,
                        wg_o, wu_o, wd_o, winc_o, woutc_o,
                        z_s, xbn_s, xp_s, ymix_s, sb_s, *, tm, nt):
    half = tm // 2
    w = RET_WIDTH
    pad = SUBLANES
    tail = CONV_W - 1
    step = pl.program_id(0)
    ZQ, ZK, ZV, ZG, ZGB = range(5)

    pw = PIECE_COLS
    per_blk = w // pw

    def stash_xb(j, v):
        xbn_s[:, j * pw:(j + 1) * pw] = v

    def window_xb(j, v):
        xp_s[pad + half:pad + tm, j * pw:(j + 1) * pw] = v

    def projection(src_ref, src_row0, z_row0, put_xb):
        hn = []

        def xb_piece(j):
            def run():
                if j == 0:
                    hn.append(
                        _rms(src_ref[src_row0:src_row0 + half, :], n1_ref[...]).astype(BF16))
                put_xb(j, _dot(hn[0], win_ref[4 * per_blk + j]))
            return run

        def z_piece(win_blk, z_blk, j):
            def run():
                z_s[z_row0:z_row0 + half, z_blk * w + j * pw:z_blk * w + (j + 1) * pw] = _dot(
                    hn[0], win_ref[win_blk * per_blk + j])
            return run

        pieces = [xb_piece(j) for j in range(per_blk)]
        for win_blk, z_blk in ((5, ZGB), (0, ZQ), (1, ZK), (2, ZV), (3, ZG)):
            pieces += [z_piece(win_blk, z_blk, j) for j in range(per_blk)]
        return pieces

    @pl.when(step == 0)
    def _():
        for piece in projection(x_ref, 0, 0, stash_xb):
            piece()

    @pl.when(lax.rem(step, nt) == 0)
    def _():
        s_ref[...] = jnp.zeros_like(s_ref)
        hT_ref[...] = jnp.zeros_like(hT_ref)
        cT_ref[...] = jnp.zeros_like(cT_ref)

    xp_s[pad - tail:pad, :] = cT_ref[0]
    xp_s[pad:pad + half, :] = xbn_s[...]
    for h in range(RET_HEADS):
        sb_s[h] = s_ref[0, h].astype(BF16)
    hprev = [hT_ref[0]]
    sp = LRU_C * _softplus(-lam_ref[...])
    heads = range(RET_HEADS)

    def lru_unit(c, emit):
        rows = slice(c * RET_CHUNK, (c + 1) * RET_CHUNK)
        xc = cbias_ref[...]
        for j in reversed(range(CONV_W)):
            off = pad - tail + j + c * RET_CHUNK
            xc = xc + xp_s[off:off + RET_CHUNK, :] * cw_ref[j:j + 1, :]
        emit()
        a, b_in = _lru_coeffs(xc, wgate_ref, ba_ref[...], bx_ref[...], sp)
        emit()
        hl, hprev[0] = _lru_scan(a, b_in, hprev[0])
        ymix_s[rows, RET_WIDTH:] = (hl * _gelu(z_s[rows, ZGB * w:(ZGB + 1) * w])).astype(BF16)

    def ret_unit(c, emit):
        rows = slice(c * RET_CHUNK, (c + 1) * RET_CHUNK)
        zcol = lambda blk, h: slice(blk * w + h * LANES, blk * w + (h + 1) * LANES)
        cols = lambda h: slice(h * RET_DK, (h + 1) * RET_DK)
        cosf = cos_ref[rows, :]
        sinf = sin_ref[rows, :]
        qr = [_rotary(z_s[rows, zcol(ZQ, h)], cosf, sinf) * Q_SCALE for h in heads]
        kr = [_rotary(z_s[rows, zcol(ZK, h)], cosf, sinf) for h in heads]
        vb = [z_s[rows, zcol(ZV, h)].astype(BF16) for h in heads]
        emit()
        sc = [lax.dot_general(qr[h].astype(BF16), kr[h].astype(BF16), (((1,), (1,)), ((), ())),
                              preferred_element_type=F32) for h in heads]
        emit()
        lhs = [jnp.concatenate([(sc[h] * dmat_ref[h]).astype(BF16),
                                (qr[h] * xi_ref[h]).astype(BF16)], axis=1) for h in heads]
        o = [_dot(lhs[h], jnp.concatenate([vb[h], sb_s[h]], axis=0)) for h in heads]
        u = [lax.dot_general((kr[h] * zeta_ref[h]).astype(BF16), vb[h], (((0,), (0,)), ((), ())),
                             preferred_element_type=F32) for h in heads]
        emit()
        for h in heads:
            ya = _group_norm(o[h]) * gn_ref[:, cols(h)] * _silu(z_s[rows, zcol(ZG, h)])
            ymix_s[rows, cols(h)] = ya.astype(BF16)
        emit()
        for h in heads:
            s_new = gdec_ref[h] * s_ref[0, h] + u[h]
            s_ref[0, h] = s_new
            sb_s[h] = s_new.astype(BF16)

    def out_proj(row0):
        rows = slice(row0, row0 + half)

        def piece(j):
            def run():
                cols = slice(j * pw, (j + 1) * pw)
                o_ref[rows, cols] = x_ref[rows, cols] + _dot(ymix_s[rows, :], wout_ref[j])
            return run

        return [piece(j) for j in range(D_MODEL // pw)]

    def run_half(first_chunk, pieces):
        pending = iter(pieces)

        def emit():
            for _ in range(PIECES_PER_STAGE):
                next(pending, lambda: None)()

        for c in range(first_chunk, first_chunk + half // RET_CHUNK):
            emit()
            lru_unit(c, emit)
            emit()
            ret_unit(c, emit)
        for piece in pending:
            piece()

    run_half(0, projection(x_ref, half, half, window_xb))
    run_half(half // RET_CHUNK, out_proj(0) + projection(xn_ref, 0, 0, stash_xb))
    for piece in out_proj(half):
        piece()

    hT_ref[0] = hprev[0]
    cT_ref[0] = xp_s[pad + tm - tail:pad + tm, :]

    wg_o[...] = wg_ref[...].astype(BF16)
    wu_o[...] = wu_ref[...].astype(BF16)
    wd_o[...] = wd_ref[...].astype(BF16)
    for i in range(2):
        winc_o[i] = winc_ref[:, i * SG_HALF:(i + 1) * SG_HALF].astype(BF16)
    woutc_o[...] = woutc_ref[...].astype(BF16)


def _mixa_prompt(x, gdec, n1, win, cosf, sinf, dmat, zeta, xi, gn, cw, cbias,
                 wgate, ba, bx, lam, wout, wg, wu, wd, winc, woutc, *, batch, seq, tm):
    nt = seq // tm
    steps = batch * nt
    nlayer = wg.shape[0]
    per_layer = steps // nlayer
    ff_rows = D_MODEL // per_layer
    fd_rows = D_FF // per_layer
    c_rows = D_MODEL // steps
    band = lambda i: (i // per_layer, lax.rem(i, per_layer), 0)
    half = tm // 2
    in_w = 6 * RET_WIDTH
    est = (D_MODEL * in_w * 4 + D_MODEL * D_MODEL * 4 + 6 * tm * D_MODEL * 4
           + tm * 5 * RET_WIDTH * 4 + 4 * tm * LRU_WIDTH * 4)
    row_blk = lambda i: (i, 0)
    next_half = lambda i: (jnp.minimum(2 * (i + 1), 2 * steps - 2), 0)
    pos_blk = lambda i: (lax.rem(i, nt), 0)
    per_b4 = lambda i: (i // nt, 0, 0, 0)
    per_b3 = lambda i: (i // nt, 0, 0)
    hd = (RET_HEADS, RET_CHUNK, RET_CHUNK)
    return pl.pallas_call(
        functools.partial(_mixa_prompt_kernel, tm=tm, nt=nt),
        grid=(steps,),
        in_specs=[
            pl.BlockSpec(memory_space=pltpu.SMEM),
            pl.BlockSpec((tm, D_MODEL), row_blk),
            pl.BlockSpec((half, D_MODEL), next_half),
            _resident((1, D_MODEL)),
            _resident((in_w // PIECE_COLS, D_MODEL, PIECE_COLS)),
            pl.BlockSpec((tm, RET_DK), pos_blk),
            pl.BlockSpec((tm, RET_DK), pos_blk),
            _resident(hd),
            _resident(hd),
            _resident(hd),
            _resident((1, RET_WIDTH)),
            _resident((CONV_W, LRU_WIDTH)),
            _resident((1, LRU_WIDTH)),
            _resident((LRU_WIDTH // LANES, LANES, 2 * LANES)),
            _resident((1, LRU_WIDTH)),
            _resident((1, LRU_WIDTH)),
            _resident((1, LRU_WIDTH)),
            _resident((D_MODEL // PIECE_COLS, D_MODEL, PIECE_COLS)),
            pl.BlockSpec((1, ff_rows, D_FF), band),
            pl.BlockSpec((1, ff_rows, D_FF), band),
            pl.BlockSpec((1, fd_rows, D_MODEL), band),
            pl.BlockSpec((c_rows, 2 * SG_HALF), row_blk),
            pl.BlockSpec((c_rows, D_MODEL), row_blk),
        ],
        out_specs=[
            pl.BlockSpec((tm, D_MODEL), row_blk),
            pl.BlockSpec((1, RET_HEADS, RET_DK, RET_DK), per_b4),
            pl.BlockSpec((1, 1, LRU_WIDTH), per_b3),
            pl.BlockSpec((1, CONV_W - 1, LRU_WIDTH), per_b3),
            pl.BlockSpec((1, ff_rows, D_FF), band),
            pl.BlockSpec((1, ff_rows, D_FF), band),
            pl.BlockSpec((1, fd_rows, D_MODEL), band),
            pl.BlockSpec((2, c_rows, SG_HALF), lambda i: (0, i, 0)),
            pl.BlockSpec((c_rows, D_MODEL), row_blk),
        ],
        out_shape=[
            jax.ShapeDtypeStruct((batch * seq, D_MODEL), F32),
            jax.ShapeDtypeStruct((batch, RET_HEADS, RET_DK, RET_DK), F32),
            jax.ShapeDtypeStruct((batch, 1, LRU_WIDTH), F32),
            jax.ShapeDtypeStruct((batch, CONV_W - 1, LRU_WIDTH), F32),
            jax.ShapeDtypeStruct(wg.shape, BF16),
            jax.ShapeDtypeStruct(wu.shape, BF16),
            jax.ShapeDtypeStruct(wd.shape, BF16),
            jax.ShapeDtypeStruct((2, D_MODEL, SG_HALF), BF16),
            jax.ShapeDtypeStruct(woutc.shape, BF16),
        ],
        scratch_shapes=[
            pltpu.VMEM((tm, 5 * RET_WIDTH), F32),
            pltpu.VMEM((half, LRU_WIDTH), F32),
            pltpu.VMEM((tm + 2 * SUBLANES, LRU_WIDTH), F32),
            pltpu.VMEM((tm, D_MODEL), BF16),
            pltpu.VMEM((RET_HEADS, RET_DK, RET_DK), BF16),
        ],
        compiler_params=pltpu.CompilerParams(
            dimension_semantics=("arbitrary",), vmem_limit_bytes=_vmem_limit(est)),
        name="mixa_prompt",
    )(gdec, x, x, n1, win, cosf, sinf, dmat, zeta, xi, gn, cw, cbias, wgate,
      ba, bx, lam, wout, wg, wu, wd, winc, woutc)


def _mixa_sample_kernel(gdec_ref, x_ref, n1_ref, win_ref, cos_ref, sin_ref,
                        s0_ref, h0_ref, cb0_ref, gn_ref, cw_ref, cbias_ref, wgate_ref,
                        ba_ref, bx_ref, lam_ref, wout_ref,
                        o_ref, s_ref, hT_ref, cT_ref,
                        q_s, k_s, v_s, sg_s, ymix_s, *, bt):
    step = pl.program_id(0)
    w = RET_WIDTH

    @pl.when(step == 0)
    def _():
        hn = _rms(x_ref[...], n1_ref[...])
        per_blk = w // PIECE_COLS
        proj = lambda blk: jnp.concatenate(
            [_dot(hn, win_ref[blk * per_blk + j]) for j in range(per_blk)], axis=1)
        cosf = cos_ref[...]
        sinf = sin_ref[...]
        zq = proj(0)
        zk = proj(1)
        for h in range(RET_HEADS):
            cols = slice(h * RET_DK, (h + 1) * RET_DK)
            q_s[:, cols] = (_rotary(zq[:, cols], cosf, sinf) * Q_SCALE).astype(BF16)
            k_s[:, cols] = _rotary(zk[:, cols], cosf, sinf).astype(BF16)
        v_s[...] = proj(2)
        sg_s[...] = _silu(proj(3))
        xb = proj(4)
        gg = _gelu(proj(5))
        xc = cbias_ref[...]
        for j in range(CONV_W - 1):
            xc = xc + cb0_ref[j] * cw_ref[j:j + 1, :]
        xc = xc + xb * cw_ref[CONV_W - 1:CONV_W, :]
        for j in range(CONV_W - 2):
            cT_ref[j] = cb0_ref[j + 1]
        cT_ref[CONV_W - 2] = xb
        a, b_in = _lru_coeffs(xc, wgate_ref, ba_ref[...], bx_ref[...],
                              LRU_C * _softplus(-lam_ref[...]))
        hnew = a * h0_ref[...] + b_in
        hT_ref[...] = hnew
        ymix_s[:, RET_WIDTH:] = (hnew * gg).astype(BF16)

    r0 = pl.multiple_of(step * bt, bt)
    rows = pl.ds(r0, bt)
    rowi = lax.broadcasted_iota(jnp.int32, (bt, RET_DK), 0)
    for h in range(RET_HEADS):
        cols = slice(h * RET_DK, (h + 1) * RET_DK)
        qs = q_s[rows, cols]
        kb = k_s[rows, cols]
        vh = v_s[rows, cols]
        qk = jnp.sum(qs.astype(F32) * kb.astype(F32), axis=-1, keepdims=True)
        cross = jnp.zeros((bt, RET_DK), F32)
        for b in range(bt):
            s_old = s0_ref[b, h]
            cr = _dot(qs, s_old.astype(BF16))
            cross = cross + jnp.where(rowi == b, cr, 0.0)
            u = lax.dot_general(kb, jnp.where(rowi == b, vh, 0.0).astype(BF16),
                                (((0,), (0,)), ((), ())), preferred_element_type=F32)
            s_ref[b, h] = gdec_ref[h] * s_old + u
        o = qk * vh + cross * gdec_ref[h]
        ymix_s[rows, cols] = (_group_norm(o) * gn_ref[:, cols] * sg_s[rows, cols]).astype(BF16)

    @pl.when(step == pl.num_programs(0) - 1)
    def _():
        ymix = ymix_s[...]
        o_ref[...] = x_ref[...] + jnp.concatenate(
            [_dot(ymix, wout_ref[j]) for j in range(D_MODEL // PIECE_COLS)], axis=1)


def _mixa_sample(x, gdec, n1, win, cosf, sinf, s0, h0, cb0, gn, cw, cbias,
                 wgate, ba, bx, lam, wout, *, bt):
    batch = x.shape[0]
    in_w = 6 * RET_WIDTH
    est = (D_MODEL * in_w * 2 + D_MODEL * D_MODEL * 2
           + 4 * bt * RET_HEADS * RET_DK * RET_DK * 4 + 12 * batch * in_w * 4)
    return pl.pallas_call(
        functools.partial(_mixa_sample_kernel, bt=bt),
        grid=(batch // bt,),
        in_specs=[
            pl.BlockSpec(memory_space=pltpu.SMEM),
            _resident((batch, D_MODEL)),
            _resident((1, D_MODEL)),
            _resident((in_w // PIECE_COLS, D_MODEL, PIECE_COLS)),
            _resident((1, RET_DK)),
            _resident((1, RET_DK)),
            pl.BlockSpec((bt, RET_HEADS, RET_DK, RET_DK), lambda i: (i, 0, 0, 0)),
            _resident((batch, LRU_WIDTH)),
            _resident((CONV_W - 1, batch, LRU_WIDTH)),
            _resident((1, RET_WIDTH)),
            _resident((CONV_W, LRU_WIDTH)),
            _resident((1, LRU_WIDTH)),
            _resident((LRU_WIDTH // LANES, LANES, 2 * LANES)),
            _resident((1, LRU_WIDTH)),
            _resident((1, LRU_WIDTH)),
            _resident((1, LRU_WIDTH)),
            _resident((D_MODEL // PIECE_COLS, D_MODEL, PIECE_COLS)),
        ],
        out_specs=[
            pl.BlockSpec((batch, D_MODEL), lambda i: (0, 0)),
            pl.BlockSpec((bt, RET_HEADS, RET_DK, RET_DK), lambda i: (i, 0, 0, 0)),
            pl.BlockSpec((batch, LRU_WIDTH), lambda i: (0, 0)),
            pl.BlockSpec((CONV_W - 1, batch, LRU_WIDTH), lambda i: (0, 0, 0)),
        ],
        out_shape=[
            jax.ShapeDtypeStruct((batch, D_MODEL), F32),
            jax.ShapeDtypeStruct((batch, RET_HEADS, RET_DK, RET_DK), F32),
            jax.ShapeDtypeStruct((batch, LRU_WIDTH), F32),
            jax.ShapeDtypeStruct((CONV_W - 1, batch, LRU_WIDTH), F32),
        ],
        scratch_shapes=[
            pltpu.VMEM((batch, RET_WIDTH), BF16),
            pltpu.VMEM((batch, RET_WIDTH), BF16),
            pltpu.VMEM((batch, RET_WIDTH), F32),
            pltpu.VMEM((batch, RET_WIDTH), F32),
            pltpu.VMEM((batch, D_MODEL), BF16),
        ],
        compiler_params=pltpu.CompilerParams(
            dimension_semantics=("arbitrary",), vmem_limit_bytes=_vmem_limit(est)),
        name="mixa_sample",
    )(gdec, x, n1, win, cosf, sinf, s0, h0, cb0, gn, cw, cbias, wgate,
      ba, bx, lam, wout)


def _rope_tables(pos):
    half = RET_DK // 2
    inv = ROPE_BASE ** (-jnp.arange(half, dtype=F32) / half)
    ang = pos[:, None] * inv[None, :]
    cos, sin = jnp.cos(ang), jnp.sin(ang)
    return jnp.concatenate([cos, cos], axis=1), jnp.concatenate([-sin, sin], axis=1)


def _decay_tables(c):
    lg = jnp.log1p(-jnp.exp2(-5.0 - jnp.arange(RET_HEADS, dtype=F32)))
    idx = jnp.arange(c, dtype=F32)
    diff = idx[:, None] - idx[None, :]
    causal = diff >= 0
    dmat = jnp.where(causal[None], jnp.exp(jnp.where(causal, diff, 0.0)[None] * lg[:, None, None]), 0.0)
    zeta = jnp.exp((c - 1.0 - idx)[None, :] * lg[:, None])
    xi = jnp.exp((idx + 1.0)[None, :] * lg[:, None])
    gdec = jnp.exp(c * lg)
    return lg, dmat, zeta, xi, gdec


def _gate_tiles(wa, wx):
    z = jnp.zeros((LRU_BW, LRU_BW), wa.dtype)
    bd = lambda w, g: jnp.block([[w[2 * g], z], [z, w[2 * g + 1]]])
    tiles = [jnp.concatenate([bd(wa, g), bd(wx, g)], axis=1) for g in range(LRU_BLOCKS // 2)]
    return jnp.stack(tiles).astype(BF16)


def kernel(x_prompt, x_sample, state_ret, state_lru, state_conv, norm1, norm2, norm_f, w_in_a, ret_gn,
           conv_w, conv_b, lru_wa, lru_ba, lru_wx, lru_bx, lru_lambda, w_out_a, w_in_c, sg_norm_g,
           sg_norm_b, sg_ws, sg_bs, w_out_c, ffn_wg, ffn_wu, ffn_wd):
    bp, lp, _ = x_prompt.shape
    bs = x_sample.shape[0]
    row = lambda v: v.reshape(1, -1)

    slabs = lambda wmat: jnp.stack(
        [wmat[:, i * PIECE_COLS:(i + 1) * PIECE_COLS].astype(BF16)
         for i in range(wmat.shape[1] // PIECE_COLS)])
    win_a = slabs(w_in_a[0])
    wout_a = slabs(w_out_a[0])
    wgate = _gate_tiles(lru_wa[0], lru_wx[0])
    mixa_w = (row(ret_gn[0]), conv_w[0], row(conv_b[0]), wgate, row(lru_ba[0]),
              row(lru_bx[0]), row(lru_lambda[0]), wout_a)

    tm = 512
    bt = 16
    cos_p, sin_p = _rope_tables(jnp.arange(lp, dtype=F32))
    cos_s, sin_s = _rope_tables(PAST_LEN + jnp.arange(1, dtype=F32))
    _, dmat, zeta, xi, gdec_p = _decay_tables(RET_CHUNK)
    _, _, _, _, gdec_s = _decay_tables(1)
    bc = lambda t: jnp.broadcast_to(t[:, :, None], (RET_HEADS, RET_CHUNK, RET_DK))
    xp = x_prompt.reshape(bp * lp, D_MODEL)
    xs = x_sample.reshape(bs, D_MODEL)

    xp, ret_p, lru_p, conv_p, wg, wu, wd, win_c, wout_c = _mixa_prompt(
        xp, gdec_p, row(norm1[0]), win_a, cos_p, sin_p, dmat, bc(zeta), bc(xi),
        *mixa_w, ffn_wg, ffn_wu, ffn_wd, w_in_c[0], w_out_c[0], batch=bp, seq=lp, tm=tm)
    ffn = lambda xp_, xs_, layer, final: _ffn(
        xp_, xs_, row(norm2[layer]), wg, wu, wd, row(norm_f), layer=layer,
        final_norm=final, tm=tm)
    xs, ret_s, lru_s, conv_s = _mixa_sample(
        xs, gdec_s, row(norm1[0]), win_a, cos_s, sin_s, state_ret[0],
        state_lru[0], jnp.transpose(state_conv[0], (1, 0, 2)), *mixa_w, bt=bt)
    xp, xs = ffn(xp, xs, 0, False)

    bsb = jnp.broadcast_to(sg_bs[0][:, :, None], (SG_GROUPS, SG_CHUNK, SG_GW))
    xp = _mixc_prompt(xp, row(norm1[1]), win_c, row(sg_norm_g[0]), row(sg_norm_b[0]), sg_ws[0],
                      bsb, wout_c, tm=2 * tm)
    sg_scale = row(jnp.repeat(sg_ws[0][:, 0, 0], SG_GW))
    sg_bias = row(jnp.repeat(sg_bs[0][:, 0], SG_GW))
    xs, v_s = _mixc_sample(xs, row(norm1[1]), win_c, row(sg_norm_g[0]), row(sg_norm_b[0]),
                           sg_scale, sg_bias, wout_c)
    y_prompt, y_sample = ffn(xp, xs, 1, True)

    return (y_prompt.reshape(bp, lp, D_MODEL),
            y_sample.reshape(bs, 1, D_MODEL),
            ret_p[None],
            ret_s[None],
            lru_p.reshape(1, bp, LRU_WIDTH),
            lru_s[None],
            conv_p[None],
            jnp.transpose(conv_s, (1, 0, 2))[None],
            v_s.reshape(1, bs, 1, SG_HALF))
```

```python
import functools
import math

import jax
import jax.numpy as jnp
from jax import lax
from jax.experimental import pallas as pl
from jax.experimental.pallas import tpu as pltpu

F32 = jnp.float32
BF16 = jnp.bfloat16

D_MODEL = 1024
RET_WIDTH = 512
RET_HEADS = 4
RET_DK = 128
RET_CHUNK = 128
ROPE_BASE = 10000.0
LRU_WIDTH = 512
LRU_BLOCKS = 8
LRU_BW = 64
LRU_C = 8.0
CONV_W = 4
SG_CHUNK = 128
SG_HALF = 1024
SG_GROUPS = 8
SG_GW = 128
D_FF = 2816
EPS = 1e-6
PAST_LEN = 16384

V7X_VMEM_BYTES = 64 * 1024 * 1024
SUBLANES = 8
LANES = 128
V7X_MXU_COLS = 256
PIECE_COLS = V7X_MXU_COLS
PIECES_PER_STAGE = 1

Q_SCALE = RET_DK ** -0.5
GELU_C = math.sqrt(2.0 / math.pi)
LOG2_E = 1.0 / math.log(2.0)


def _vmem_limit(estimate_bytes):
    return int(min(estimate_bytes * 3 // 2 + (8 << 20), V7X_VMEM_BYTES - (6 << 20)))


def _resident(shape):
    nd = len(shape)
    return pl.BlockSpec(shape, lambda *_: (0,) * nd, pipeline_mode=pl.Buffered(1))


def _dot(a, b):
    return lax.dot_general(a, b, (((1,), (0,)), ((), ())), preferred_element_type=F32)


def _rms(x, g):
    ms = jnp.mean(x * x, axis=-1, keepdims=True)
    return x * lax.rsqrt(ms + EPS) * g


def _gelu(x):
    k1 = -2.0 * GELU_C * LOG2_E
    k3 = k1 * 0.044715
    return x / (1.0 + jnp.exp2(x * (k3 * (x * x) + k1)))


def _silu(x):
    return x * jax.nn.sigmoid(x)


def _softplus(x):
    return jnp.maximum(x, 0.0) + jnp.log1p(jnp.exp(-jnp.abs(x)))


def _sqrt_nonneg(y):
    return jnp.where(y > 0.0, y * lax.rsqrt(y), 0.0)


def _group_norm(o):
    mu = jnp.mean(o, axis=-1, keepdims=True)
    oc = o - mu
    var = jnp.mean(oc * oc, axis=-1, keepdims=True)
    return oc * lax.rsqrt(var + EPS)


def _rotary(x, cosf, sinf):
    return x * cosf + pltpu.roll(x, RET_DK // 2, 1) * sinf


def _lru_coeffs(xc, wgate_ref, ba, bx, spc):
    ra, rx = [], []
    for g in range(LRU_WIDTH // LANES):
        cols = slice(g * LANES, (g + 1) * LANES)
        both = _dot(xc[:, cols], wgate_ref[g])
        ra.append(both[:, :LANES])
        rx.append(both[:, LANES:])
    r = jax.nn.sigmoid(jnp.concatenate(ra, axis=1) + ba)
    i = jax.nn.sigmoid(jnp.concatenate(rx, axis=1) + bx)
    nl = r * spc
    a = jnp.exp2(nl * (-LOG2_E))
    mult = _sqrt_nonneg(jnp.tanh(nl) * (a * a + 1.0))
    return a, xc * i * mult


def _ffn_rows(x, n2, wg_ref, wu_ref, wd_ref, nf, final_norm):
    h = _rms(x, n2).astype(BF16)
    act = (_silu(_dot(h, wg_ref[...])) * _dot(h, wu_ref[...])).astype(BF16)
    y = x + _dot(act, wd_ref[...])
    return _rms(y, nf) if final_norm else y


def _exact_zero_from(v, width):
    bits = pltpu.bitcast(v, jnp.uint32)
    z = lax.shift_right_logical(lax.shift_right_logical(bits, jnp.uint32(16)), jnp.uint32(16))
    folded = z[:, :width]
    for c0 in range(width, v.shape[1], width):
        folded = folded | z[:, c0:c0 + width]
    return pltpu.bitcast(folded, F32)


def _ffn_kernel(x_ref, xn_ref, xs_ref, n2_ref, wg_ref, wu_ref, wd_ref, nf_ref, o_ref, os_ref,
                h_s, *, final_norm, prompt_steps):
    step = pl.program_id(0)
    args = (n2_ref[...], wg_ref, wu_ref, wd_ref, nf_ref[...], final_norm)
    slab = V7X_MXU_COLS

    @pl.when(step == 0)
    def _():
        h_s[...] = _rms(x_ref[...], n2_ref[...]).astype(BF16)

    @pl.when(step < prompt_steps)
    def _():
        h = h_s[...]
        act = _silu(_dot(h, wg_ref[...])) * _dot(h, wu_ref[...])
        nxt = _rms(xn_ref[...], n2_ref[...])
        h_s[...] = nxt.astype(BF16)
        head = act[:, :D_FF - slab].astype(BF16)
        last = (act[:, D_FF - slab:] + _exact_zero_from(nxt, slab)).astype(BF16)
        y = (x_ref[...] + _dot(head, wd_ref[:D_FF - slab, :])) + _dot(last, wd_ref[D_FF - slab:, :])
        o_ref[...] = _rms(y, nf_ref[...]) if final_norm else y

    @pl.when(step == prompt_steps)
    def _():
        os_ref[...] = _ffn_rows(xs_ref[...], *args)


def _ffn(x, xs, n2, wg, wu, wd, nf, *, layer, final_norm, tm):
    t, ts = x.shape[0], xs.shape[0]
    steps = t // tm
    est = (3 * D_MODEL * D_FF * 2 + 7 * tm * D_MODEL * 4 + 4 * ts * D_MODEL * 4
           + 3 * tm * D_FF * 4)
    layer_blk = lambda shape: pl.BlockSpec((None,) + shape, lambda i: (layer, 0, 0),
                                           pipeline_mode=pl.Buffered(1))
    prompt_blk = pl.BlockSpec((tm, D_MODEL), lambda i: (jnp.minimum(i, steps - 1), 0))
    next_blk = pl.BlockSpec((tm, D_MODEL), lambda i: (jnp.minimum(i + 1, steps - 1), 0))
    return pl.pallas_call(
        functools.partial(_ffn_kernel, final_norm=final_norm, prompt_steps=steps),
        grid=(steps + 1,),
        in_specs=[
            prompt_blk,
            next_blk,
            _resident((ts, D_MODEL)),
            _resident((1, D_MODEL)),
            layer_blk((D_MODEL, D_FF)),
            layer_blk((D_MODEL, D_FF)),
            layer_blk((D_FF, D_MODEL)),
            _resident((1, D_MODEL)),
        ],
        out_specs=[prompt_blk, pl.BlockSpec((ts, D_MODEL), lambda i: (0, 0))],
        out_shape=[jax.ShapeDtypeStruct((t, D_MODEL), F32),
                   jax.ShapeDtypeStruct((ts, D_MODEL), F32)],
        scratch_shapes=[pltpu.VMEM((tm, D_MODEL), BF16)],
        compiler_params=pltpu.CompilerParams(
            dimension_semantics=("arbitrary",), vmem_limit_bytes=_vmem_limit(est)),
        name="ffn_final" if final_norm else "ffn",
    )(x, x, xs, n2, wg, wu, wd, nf)


def _sgu_u(hn, win_ref):
    return _gelu(_dot(hn, win_ref[0]))


def _sgu_v(hn, win_ref, vg, vb):
    zv = _gelu(_dot(hn, win_ref[1]))
    mu = jnp.mean(zv, axis=-1, keepdims=True)
    vc = zv - mu
    var = jnp.mean(vc * vc, axis=-1, keepdims=True)
    return vc * lax.rsqrt(var + EPS) * vg + vb


def _mixc_prompt_kernel(x_ref, n1_ref, win_ref, vg_ref, vb_ref, ws_ref, bsb_ref, wout_ref,
                        o_ref, u_s, v_s, gated_s, wm_s, *, tm):
    half = tm // 2
    nchunk = half // SG_CHUNK
    ri = lax.broadcasted_iota(jnp.int32, (SG_CHUNK, SG_CHUNK), 0)
    ci = lax.broadcasted_iota(jnp.int32, (SG_CHUNK, SG_CHUNK), 1)
    for g in range(SG_GROUPS):
        wm_s[g] = jnp.where(ri >= ci, ws_ref[g], 0.0).astype(BF16)

    def norm(r0):
        return _rms(x_ref[r0:r0 + half, :], n1_ref[...])

    def put_u(r0, hn):
        u_s[r0:r0 + half, :] = _sgu_u(hn, win_ref)

    def put_v(r0, hn):
        v_s[r0:r0 + half, :] = _sgu_v(hn, win_ref, vg_ref[...], vb_ref[...]).astype(BF16)

    def gate(r0):
        for g in range(SG_GROUPS):
            cols = slice(g * SG_GW, (g + 1) * SG_GW)
            vcat = jnp.concatenate(
                [v_s[r0 + c * SG_CHUNK:r0 + (c + 1) * SG_CHUNK, cols] for c in range(nchunk)],
                axis=1)
            sv_all = _dot(wm_s[g], vcat)
            for c in range(nchunk):
                rows = slice(r0 + c * SG_CHUNK, r0 + (c + 1) * SG_CHUNK)
                sv = sv_all[:, c * SG_GW:(c + 1) * SG_GW] + bsb_ref[g]
                gated_s[rows, cols] = u_s[rows, cols] * sv

    def out_proj(r0):
        rows = slice(r0, r0 + half)
        o_ref[rows, :] = x_ref[rows, :] + _dot(gated_s[rows, :], wout_ref[...])

    hn_a = norm(0)
    put_u(0, hn_a)
    put_v(0, hn_a)
    hn_b = norm(half)
    put_u(half, hn_b)
    gate(0)
    put_v(half, hn_b)
    out_proj(0)
    gate(half)
    out_proj(half)


def _mixc_prompt(x, n1, win, vg, vb, ws, bsb, wout, *, tm):
    t = x.shape[0]
    est = 3 * D_MODEL * SG_HALF * 2 + 4 * tm * D_MODEL * 4 + 5 * tm * 2 * SG_HALF * 4
    return pl.pallas_call(
        functools.partial(_mixc_prompt_kernel, tm=tm),
        grid=(t // tm,),
        in_specs=[
            pl.BlockSpec((tm, D_MODEL), lambda i: (i, 0)),
            _resident((1, D_MODEL)),
            _resident((2, D_MODEL, SG_HALF)),
            _resident((1, SG_HALF)),
            _resident((1, SG_HALF)),
            _resident((SG_GROUPS, SG_CHUNK, SG_CHUNK)),
            _resident((SG_GROUPS, SG_CHUNK, SG_GW)),
            _resident((SG_HALF, D_MODEL)),
        ],
        out_specs=pl.BlockSpec((tm, D_MODEL), lambda i: (i, 0)),
        out_shape=jax.ShapeDtypeStruct((t, D_MODEL), F32),
        scratch_shapes=[
            pltpu.VMEM((tm, SG_HALF), F32),
            pltpu.VMEM((tm, SG_HALF), BF16),
            pltpu.VMEM((tm, SG_HALF), F32),
            pltpu.VMEM((SG_GROUPS, SG_CHUNK, SG_CHUNK), BF16),
        ],
        compiler_params=pltpu.CompilerParams(
            dimension_semantics=("arbitrary",), vmem_limit_bytes=_vmem_limit(est)),
        name="mixc_prompt",
    )(x, n1, win, vg, vb, ws, bsb, wout)


def _mixc_ffn_kernel(x_ref, xs_ref, n1_ref, win_ref, vg_ref, vb_ref, ws_ref, bsb_ref, wout_ref,
                     n2_ref, wg_ref, wu_ref, wd_ref, nf_ref,
                     o_ref, os_ref,
                     x3_s, u_s, v_s, gated_s, wm_s, *, tm, prompt_steps):
    step = pl.program_id(0)
    nchunk = tm // SG_CHUNK

    @pl.when(step == 0)
    def _():
        x3_s[...] = jnp.zeros_like(x3_s)
        ri = lax.broadcasted_iota(jnp.int32, (SG_CHUNK, SG_CHUNK), 0)
        ci = lax.broadcasted_iota(jnp.int32, (SG_CHUNK, SG_CHUNK), 1)
        for g in range(SG_GROUPS):
            wm_s[g] = jnp.where(ri >= ci, ws_ref[g], 0.0).astype(BF16)

    x3 = x3_s[...]
    h_f = _rms(x3, n2_ref[...]).astype(BF16)
    gate_f = _dot(h_f, wg_ref[...])
    h_m = _rms(x_ref[...], n1_ref[...])
    u_s[...] = _sgu_u(h_m, win_ref)
    up_f = _dot(h_f, wu_ref[...])
    v_s[...] = _sgu_v(h_m, win_ref, vg_ref[...], vb_ref[...]).astype(BF16)
    act_f = (_silu(gate_f) * up_f).astype(BF16)
    y_f = x3 + _dot(act_f, wd_ref[...])
    for g in range(SG_GROUPS):
        cols = slice(g * SG_GW, (g + 1) * SG_GW)
        vcat = jnp.concatenate(
            [v_s[c * SG_CHUNK:(c + 1) * SG_CHUNK, cols] for c in range(nchunk)], axis=1)
        sv_all = _dot(wm_s[g], vcat)
        for c in range(nchunk):
            rows = slice(c * SG_CHUNK, (c + 1) * SG_CHUNK)
            sv = sv_all[:, c * SG_GW:(c + 1) * SG_GW] + bsb_ref[g]
            gated_s[rows, cols] = u_s[rows, cols] * sv
    o_ref[...] = _rms(y_f, nf_ref[...])
    x3_s[...] = x_ref[...] + _dot(gated_s[...], wout_ref[...])

    @pl.when(step == prompt_steps)
    def _():
        os_ref[...] = _ffn_rows(xs_ref[...], n2_ref[...], wg_ref, wu_ref, wd_ref, nf_ref[...], True)


def _mixc_ffn(x, xs, n1, win, vg, vb, ws, bsb, wout, n2, wg, wu, wd, nf, *, layer, tm):
    t, ts = x.shape[0], xs.shape[0]
    steps = t // tm
    est = (3 * D_MODEL * D_FF * 2 + 3 * D_MODEL * SG_HALF * 2 + 8 * tm * D_MODEL * 4
           + 3 * tm * D_FF * 4 + 4 * ts * D_MODEL * 4)
    layer_blk = lambda shape: pl.BlockSpec((None,) + shape, lambda i: (layer, 0, 0),
                                           pipeline_mode=pl.Buffered(1))
    return pl.pallas_call(
        functools.partial(_mixc_ffn_kernel, tm=tm, prompt_steps=steps),
        grid=(steps + 1,),
        in_specs=[
            pl.BlockSpec((tm, D_MODEL), lambda i: (jnp.minimum(i, steps - 1), 0)),
            _resident((ts, D_MODEL)),
            _resident((1, D_MODEL)),
            _resident((2, D_MODEL, SG_HALF)),
            _resident((1, SG_HALF)),
            _resident((1, SG_HALF)),
            _resident((SG_GROUPS, SG_CHUNK, SG_CHUNK)),
            _resident((SG_GROUPS, SG_CHUNK, SG_GW)),
            _resident((SG_HALF, D_MODEL)),
            _resident((1, D_MODEL)),
            layer_blk((D_MODEL, D_FF)),
            layer_blk((D_MODEL, D_FF)),
            layer_blk((D_FF, D_MODEL)),
            _resident((1, D_MODEL)),
        ],
        out_specs=[
            pl.BlockSpec((tm, D_MODEL), lambda i: (jnp.maximum(i - 1, 0), 0)),
            pl.BlockSpec((ts, D_MODEL), lambda i: (0, 0)),
        ],
        out_shape=[jax.ShapeDtypeStruct((t, D_MODEL), F32),
                   jax.ShapeDtypeStruct((ts, D_MODEL), F32)],
        scratch_shapes=[
            pltpu.VMEM((tm, D_MODEL), F32),
            pltpu.VMEM((tm, SG_HALF), F32),
            pltpu.VMEM((tm, SG_HALF), BF16),
            pltpu.VMEM((tm, SG_HALF), F32),
            pltpu.VMEM((SG_GROUPS, SG_CHUNK, SG_CHUNK), BF16),
        ],
        compiler_params=pltpu.CompilerParams(
            dimension_semantics=("arbitrary",), vmem_limit_bytes=_vmem_limit(est)),
        name="mixc_ffn_final",
    )(x, xs, n1, win, vg, vb, ws, bsb, wout, n2, wg, wu, wd, nf)


def _mixc_sample_kernel(x_ref, n1_ref, win_ref, vg_ref, vb_ref, scale_ref, bias_ref, wout_ref,
                        o_ref, v_ref):
    x = x_ref[...]
    hn = _rms(x, n1_ref[...])
    u = _sgu_u(hn, win_ref)
    vn = _sgu_v(hn, win_ref, vg_ref[...], vb_ref[...])
    v_ref[...] = vn
    sv = vn * scale_ref[...] + bias_ref[...]
    o_ref[...] = x + _dot(u * sv, wout_ref[...])


def _mixc_sample(x, n1, win, vg, vb, scale, bias, wout):
    t = x.shape[0]
    est = 3 * D_MODEL * SG_HALF * 2 + 8 * t * 2 * SG_HALF * 4
    return pl.pallas_call(
        _mixc_sample_kernel,
        grid=(1,),
        in_specs=[
            _resident((t, D_MODEL)),
            _resident((1, D_MODEL)),
            _resident((2, D_MODEL, SG_HALF)),
            _resident((1, SG_HALF)),
            _resident((1, SG_HALF)),
            _resident((1, SG_HALF)),
            _resident((1, SG_HALF)),
            _resident((SG_HALF, D_MODEL)),
        ],
        out_specs=[
            pl.BlockSpec((t, D_MODEL), lambda i: (0, 0)),
            pl.BlockSpec((t, SG_HALF), lambda i: (0, 0)),
        ],
        out_shape=[
            jax.ShapeDtypeStruct((t, D_MODEL), F32),
            jax.ShapeDtypeStruct((t, SG_HALF), F32),
        ],
        compiler_params=pltpu.CompilerParams(
            dimension_semantics=("arbitrary",), vmem_limit_bytes=_vmem_limit(est)),
        name="mixc_sample",
    )(x, n1, win, vg, vb, scale, bias, wout)


def _lru_scan(a, b, h0):
    tm, w = a.shape
    groups = tm // SUBLANES
    a3 = a.reshape(groups, SUBLANES, w)
    b3 = b.reshape(groups, SUBLANES, w)
    sub = lax.broadcasted_iota(jnp.int32, (groups, SUBLANES, w), 1)
    for s in (1, 2, 4):
        a_sh = pltpu.roll(a3, s, 1)
        b_sh = pltpu.roll(b3, s, 1)
        keep = sub >= s
        b3 = jnp.where(keep, a3 * b_sh + b3, b3)
        a3 = jnp.where(keep, a3 * a_sh, a3)
    hprev = jnp.broadcast_to(h0, (SUBLANES, w))
    hs = []
    for r in range(groups):
        hr = a3[r] * hprev + b3[r]
        hs.append(hr)
        hprev = jnp.broadcast_to(hr[SUBLANES - 1:SUBLANES, :], (SUBLANES, w))
    return jnp.concatenate(hs, axis=0), hprev[0:1, :]


def _mixa_prompt_kernel(gdec_ref, x_ref, xn_ref, n1_ref, win_ref, cos_ref, sin_ref, dmat_ref,
                        zeta_ref, xi_ref, gn_ref, cw_ref, cbias_ref,
                        wgate_ref, ba_ref, bx_ref, lam_ref, wout_ref,
                        wg_ref, wu_ref, wd_ref, winc_ref, woutc_ref,
                        o_ref, s_ref, hT_ref, cT_ref,
                        wg_o, wu_o, wd_o, winc_o, woutc_o,
                        z_s, xbn_s, xp_s, ymix_s, sb_s, *, tm, nt):
    half = tm // 2
    w = RET_WIDTH
    pad = SUBLANES
    tail = CONV_W - 1
    step = pl.program_id(0)
    ZQ, ZK, ZV, ZG, ZGB = range(5)

    pw = PIECE_COLS
    per_blk = w // pw

    def stash_xb(j, v):
        xbn_s[:, j * pw:(j + 1) * pw] = v

    def window_xb(j, v):
        xp_s[pad + half:pad + tm, j * pw:(j + 1) * pw] = v

    def projection(src_ref, src_row0, z_row0, put_xb):
        hn = []

        def xb_piece(j):
            def run():
                if j == 0:
                    hn.append(
                        _rms(src_ref[src_row0:src_row0 + half, :], n1_ref[...]).astype(BF16))
                put_xb(j, _dot(hn[0], win_ref[4 * per_blk + j]))
            return run

        def z_piece(win_blk, z_blk, j):
            def run():
                z_s[z_row0:z_row0 + half, z_blk * w + j * pw:z_blk * w + (j + 1) * pw] = _dot(
                    hn[0], win_ref[win_blk * per_blk + j])
            return run

        pieces = [xb_piece(j) for j in range(per_blk)]
        for win_blk, z_blk in ((5, ZGB), (0, ZQ), (1, ZK), (2, ZV), (3, ZG)):
            pieces += [z_piece(win_blk, z_blk, j) for j in range(per_blk)]
        return pieces

    @pl.when(step == 0)
    def _():
        for piece in projection(x_ref, 0, 0, stash_xb):
            piece()

    @pl.when(lax.rem(step, nt) == 0)
    def _():
        s_ref[...] = jnp.zeros_like(s_ref)
        hT_ref[...] = jnp.zeros_like(hT_ref)
        cT_ref[...] = jnp.zeros_like(cT_ref)

    xp_s[pad - tail:pad, :] = cT_ref[0]
    xp_s[pad:pad + half, :] = xbn_s[...]
    for h in range(RET_HEADS):
        sb_s[h] = s_ref[0, h].astype(BF16)
    hprev = [hT_ref[0]]
    sp = LRU_C * _softplus(-lam_ref[...])
    heads = range(RET_HEADS)

    def lru_unit(c, emit):
        rows = slice(c * RET_CHUNK, (c + 1) * RET_CHUNK)
        xc = cbias_ref[...]
        for j in reversed(range(CONV_W)):
            off = pad - tail + j + c * RET_CHUNK
            xc = xc + xp_s[off:off + RET_CHUNK, :] * cw_ref[j:j + 1, :]
        emit()
        a, b_in = _lru_coeffs(xc, wgate_ref, ba_ref[...], bx_ref[...], sp)
        emit()
        hl, hprev[0] = _lru_scan(a, b_in, hprev[0])
        ymix_s[rows, RET_WIDTH:] = (hl * _gelu(z_s[rows, ZGB * w:(ZGB + 1) * w])).astype(BF16)

    def ret_unit(c, emit):
        rows = slice(c * RET_CHUNK, (c + 1) * RET_CHUNK)
        zcol = lambda blk, h: slice(blk * w + h * LANES, blk * w + (h + 1) * LANES)
        cols = lambda h: slice(h * RET_DK, (h + 1) * RET_DK)
        cosf = cos_ref[rows, :]
        sinf = sin_ref[rows, :]
        qr = [_rotary(z_s[rows, zcol(ZQ, h)], cosf, sinf) * Q_SCALE for h in heads]
        kr = [_rotary(z_s[rows, zcol(ZK, h)], cosf, sinf) for h in heads]
        vb = [z_s[rows, zcol(ZV, h)].astype(BF16) for h in heads]
        emit()
        sc = [lax.dot_general(qr[h], kr[h].astype(BF16), (((1,), (1,)), ((), ())),
                              preferred_element_type=F32) for h in heads]
        emit()
        lhs = [jnp.concatenate([sc[h] * dmat_ref[h], qr[h] * xi_ref[h]], axis=1) for h in heads]
        o = [_dot(lhs[h], jnp.concatenate([vb[h], sb_s[h]], axis=0)) for h in heads]
        u = [lax.dot_general((kr[h] * zeta_ref[h]).astype(BF16), vb[h], (((0,), (0,)), ((), ())),
                             preferred_element_type=F32) for h in heads]
        emit()
        for h in heads:
            ya = _group_norm(o[h]) * gn_ref[:, cols(h)] * _silu(z_s[rows, zcol(ZG, h)])
            ymix_s[rows, cols(h)] = ya.astype(BF16)
        emit()
        for h in heads:
            s_new = gdec_ref[h] * s_ref[0, h] + u[h]
            s_ref[0, h] = s_new
            sb_s[h] = s_new.astype(BF16)

    def out_proj(row0):
        rows = slice(row0, row0 + half)

        def piece(j):
            def run():
                cols = slice(j * pw, (j + 1) * pw)
                o_ref[rows, cols] = x_ref[rows, cols] + _dot(ymix_s[rows, :], wout_ref[j])
            return run

        return [piece(j) for j in range(D_MODEL // pw)]

    def run_half(first_chunk, pieces):
        pending = iter(pieces)

        def emit():
            for _ in range(PIECES_PER_STAGE):
                next(pending, lambda: None)()

        for c in range(first_chunk, first_chunk + half // RET_CHUNK):
            emit()
            lru_unit(c, emit)
            emit()
            ret_unit(c, emit)
        for piece in pending:
            piece()

    run_half(0, projection(x_ref, half, half, window_xb))
    run_half(half // RET_CHUNK, out_proj(0) + projection(xn_ref, 0, 0, stash_xb))
    for piece in out_proj(half):
        piece()

    hT_ref[0] = hprev[0]
    cT_ref[0] = xp_s[pad + tm - tail:pad + tm, :]

    wg_o[...] = wg_ref[...].astype(BF16)
    wu_o[...] = wu_ref[...].astype(BF16)
    wd_o[...] = wd_ref[...].astype(BF16)
    for i in range(2):
        winc_o[i] = winc_ref[:, i * SG_HALF:(i + 1) * SG_HALF].astype(BF16)
    woutc_o[...] = woutc_ref[...].astype(BF16)


def _mixa_prompt(x, gdec, n1, win, cosf, sinf, dmat, zeta, xi, gn, cw, cbias,
                 wgate, ba, bx, lam, wout, wg, wu, wd, winc, woutc, *, batch, seq, tm):
    nt = seq // tm
    steps = batch * nt
    nlayer = wg.shape[0]
    per_layer = steps // nlayer
    ff_rows = D_MODEL // per_layer
    fd_rows = D_FF // per_layer
    c_rows = D_MODEL // steps
    band = lambda i: (i // per_layer, lax.rem(i, per_layer), 0)
    half = tm // 2
    in_w = 6 * RET_WIDTH
    est = (D_MODEL * in_w * 4 + D_MODEL * D_MODEL * 4 + 6 * tm * D_MODEL * 4
           + tm * 5 * RET_WIDTH * 4 + 4 * tm * LRU_WIDTH * 4)
    row_blk = lambda i: (i, 0)
    next_half = lambda i: (jnp.minimum(2 * (i + 1), 2 * steps - 2), 0)
    pos_blk = lambda i: (lax.rem(i, nt), 0)
    per_b4 = lambda i: (i // nt, 0, 0, 0)
    per_b3 = lambda i: (i // nt, 0, 0)
    hd = (RET_HEADS, RET_CHUNK, RET_CHUNK)
    return pl.pallas_call(
        functools.partial(_mixa_prompt_kernel, tm=tm, nt=nt),
        grid=(steps,),
        in_specs=[
            pl.BlockSpec(memory_space=pltpu.SMEM),
            pl.BlockSpec((tm, D_MODEL), row_blk),
            pl.BlockSpec((half, D_MODEL), next_half),
            _resident((1, D_MODEL)),
            _resident((in_w // PIECE_COLS, D_MODEL, PIECE_COLS)),
            pl.BlockSpec((tm, RET_DK), pos_blk),
            pl.BlockSpec((tm, RET_DK), pos_blk),
            _resident(hd),
            _resident(hd),
            _resident(hd),
            _resident((1, RET_WIDTH)),
            _resident((CONV_W, LRU_WIDTH)),
            _resident((1, LRU_WIDTH)),
            _resident((LRU_WIDTH // LANES, LANES, 2 * LANES)),
            _resident((1, LRU_WIDTH)),
            _resident((1, LRU_WIDTH)),
            _resident((1, LRU_WIDTH)),
            _resident((D_MODEL // PIECE_COLS, D_MODEL, PIECE_COLS)),
            pl.BlockSpec((1, ff_rows, D_FF), band),
            pl.BlockSpec((1, ff_rows, D_FF), band),
            pl.BlockSpec((1, fd_rows, D_MODEL), band),
            pl.BlockSpec((c_rows, 2 * SG_HALF), row_blk),
            pl.BlockSpec((c_rows, D_MODEL), row_blk),
        ],
        out_specs=[
            pl.BlockSpec((tm, D_MODEL), row_blk),
            pl.BlockSpec((1, RET_HEADS, RET_DK, RET_DK), per_b4),
            pl.BlockSpec((1, 1, LRU_WIDTH), per_b3),
            pl.BlockSpec((1, CONV_W - 1, LRU_WIDTH), per_b3),
            pl.BlockSpec((1, ff_rows, D_FF), band),
            pl.BlockSpec((1, ff_rows, D_FF), band),
            pl.BlockSpec((1, fd_rows, D_MODEL), band),
            pl.BlockSpec((2, c_rows, SG_HALF), lambda i: (0, i, 0)),
            pl.BlockSpec((c_rows, D_MODEL), row_blk),
        ],
        out_shape=[
            jax.ShapeDtypeStruct((batch * seq, D_MODEL), F32),
            jax.ShapeDtypeStruct((batch, RET_HEADS, RET_DK, RET_DK), F32),
            jax.ShapeDtypeStruct((batch, 1, LRU_WIDTH), F32),
            jax.ShapeDtypeStruct((batch, CONV_W - 1, LRU_WIDTH), F32),
            jax.ShapeDtypeStruct(wg.shape, BF16),
            jax.ShapeDtypeStruct(wu.shape, BF16),
            jax.ShapeDtypeStruct(wd.shape, BF16),
            jax.ShapeDtypeStruct((2, D_MODEL, SG_HALF), BF16),
            jax.ShapeDtypeStruct(woutc.shape, BF16),
        ],
        scratch_shapes=[
            pltpu.VMEM((tm, 5 * RET_WIDTH), F32),
            pltpu.VMEM((half, LRU_WIDTH), F32),
            pltpu.VMEM((tm + 2 * SUBLANES, LRU_WIDTH), F32),
            pltpu.VMEM((tm, D_MODEL), BF16),
            pltpu.VMEM((RET_HEADS, RET_DK, RET_DK), BF16),
        ],
        compiler_params=pltpu.CompilerParams(
            dimension_semantics=("arbitrary",), vmem_limit_bytes=_vmem_limit(est)),
        name="mixa_prompt",
    )(gdec, x, x, n1, win, cosf, sinf, dmat, zeta, xi, gn, cw, cbias, wgate,
      ba, bx, lam, wout, wg, wu, wd, winc, woutc)


def _mixa_sample_kernel(gdec_ref, x_ref, n1_ref, win_ref, cos_ref, sin_ref,
                        s0_ref, h0_ref, cb0_ref, gn_ref, cw_ref, cbias_ref, wgate_ref,
                        ba_ref, bx_ref, lam_ref, wout_ref,
                        o_ref, s_ref, hT_ref, cT_ref,
                        q_s, k_s, v_s, sg_s, ymix_s, *, bt):
    step = pl.program_id(0)
    w = RET_WIDTH

    @pl.when(step == 0)
    def _():
        hn = _rms(x_ref[...], n1_ref[...])
        per_blk = w // PIECE_COLS
        proj = lambda blk: jnp.concatenate(
            [_dot(hn, win_ref[blk * per_blk + j]) for j in range(per_blk)], axis=1)
        cosf = cos_ref[...]
        sinf = sin_ref[...]
        zq = proj(0)
        zk = proj(1)
        for h in range(RET_HEADS):
            cols = slice(h * RET_DK, (h + 1) * RET_DK)
            q_s[:, cols] = (_rotary(zq[:, cols], cosf, sinf) * Q_SCALE).astype(BF16)
            k_s[:, cols] = _rotary(zk[:, cols], cosf, sinf).astype(BF16)
        v_s[...] = proj(2)
        sg_s[...] = _silu(proj(3))
        xb = proj(4)
        gg = _gelu(proj(5))
        xc = cbias_ref[...]
        for j in range(CONV_W - 1):
            xc = xc + cb0_ref[j] * cw_ref[j:j + 1, :]
        xc = xc + xb * cw_ref[CONV_W - 1:CONV_W, :]
        for j in range(CONV_W - 2):
            cT_ref[j] = cb0_ref[j + 1]
        cT_ref[CONV_W - 2] = xb
        a, b_in = _lru_coeffs(xc, wgate_ref, ba_ref[...], bx_ref[...],
                              LRU_C * _softplus(-lam_ref[...]))
        hnew = a * h0_ref[...] + b_in
        hT_ref[...] = hnew
        ymix_s[:, RET_WIDTH:] = (hnew * gg).astype(BF16)

    r0 = pl.multiple_of(step * bt, bt)
    rows = pl.ds(r0, bt)
    rowi = lax.broadcasted_iota(jnp.int32, (bt, RET_DK), 0)
    for h in range(RET_HEADS):
        cols = slice(h * RET_DK, (h + 1) * RET_DK)
        qs = q_s[rows, cols]
        kb = k_s[rows, cols]
        vh = v_s[rows, cols]
        qk = jnp.sum(qs.astype(F32) * kb.astype(F32), axis=-1, keepdims=True)
        cross = jnp.zeros((bt, RET_DK), F32)
        for b in range(bt):
            s_old = s0_ref[b, h]
            cr = _dot(qs, s_old.astype(BF16))
            cross = cross + jnp.where(rowi == b, cr, 0.0)
            u = lax.dot_general(kb, jnp.where(rowi == b, vh, 0.0).astype(BF16),
                                (((0,), (0,)), ((), ())), preferred_element_type=F32)
            s_ref[b, h] = gdec_ref[h] * s_old + u
        o = qk * vh + cross * gdec_ref[h]
        ymix_s[rows, cols] = (_group_norm(o) * gn_ref[:, cols] * sg_s[rows, cols]).astype(BF16)

    @pl.when(step == pl.num_programs(0) - 1)
    def _():
        ymix = ymix_s[...]
        o_ref[...] = x_ref[...] + jnp.concatenate(
            [_dot(ymix, wout_ref[j]) for j in range(D_MODEL // PIECE_COLS)], axis=1)


def _mixa_sample(x, gdec, n1, win, cosf, sinf, s0, h0, cb0, gn, cw, cbias,
                 wgate, ba, bx, lam, wout, *, bt):
    batch = x.shape[0]
    in_w = 6 * RET_WIDTH
    est = (D_MODEL * in_w * 2 + D_MODEL * D_MODEL * 2
           + 4 * bt * RET_HEADS * RET_DK * RET_DK * 4 + 12 * batch * in_w * 4)
    return pl.pallas_call(
        functools.partial(_mixa_sample_kernel, bt=bt),
        grid=(batch // bt,),
        in_specs=[
            pl.BlockSpec(memory_space=pltpu.SMEM),
            _resident((batch, D_MODEL)),
            _resident((1, D_MODEL)),
            _resident((in_w // PIECE_COLS, D_MODEL, PIECE_COLS)),
            _resident((1, RET_DK)),
            _resident((1, RET_DK)),
            pl.BlockSpec((bt, RET_HEADS, RET_DK, RET_DK), lambda i: (i, 0, 0, 0)),
            _resident((batch, LRU_WIDTH)),
            _resident((CONV_W - 1, batch, LRU_WIDTH)),
            _resident((1, RET_WIDTH)),
            _resident((CONV_W, LRU_WIDTH)),
            _resident((1, LRU_WIDTH)),
            _resident((LRU_WIDTH // LANES, LANES, 2 * LANES)),
            _resident((1, LRU_WIDTH)),
            _resident((1, LRU_WIDTH)),
            _resident((1, LRU_WIDTH)),
            _resident((D_MODEL // PIECE_COLS, D_MODEL, PIECE_COLS)),
        ],
        out_specs=[
            pl.BlockSpec((batch, D_MODEL), lambda i: (0, 0)),
            pl.BlockSpec((bt, RET_HEADS, RET_DK, RET_DK), lambda i: (i, 0, 0, 0)),
            pl.BlockSpec((batch, LRU_WIDTH), lambda i: (0, 0)),
            pl.BlockSpec((CONV_W - 1, batch, LRU_WIDTH), lambda i: (0, 0, 0)),
        ],
        out_shape=[
            jax.ShapeDtypeStruct((batch, D_MODEL), F32),
            jax.ShapeDtypeStruct((batch, RET_HEADS, RET_DK, RET_DK), F32),
            jax.ShapeDtypeStruct((batch, LRU_WIDTH), F32),
            jax.ShapeDtypeStruct((CONV_W - 1, batch, LRU_WIDTH), F32),
        ],
        scratch_shapes=[
            pltpu.VMEM((batch, RET_WIDTH), BF16),
            pltpu.VMEM((batch, RET_WIDTH), BF16),
            pltpu.VMEM((batch, RET_WIDTH), F32),
            pltpu.VMEM((batch, RET_WIDTH), F32),
            pltpu.VMEM((batch, D_MODEL), BF16),
        ],
        compiler_params=pltpu.CompilerParams(
            dimension_semantics=("arbitrary",), vmem_limit_bytes=_vmem_limit(est)),
        name="mixa_sample",
    )(gdec, x, n1, win, cosf, sinf, s0, h0, cb0, gn, cw, cbias, wgate,
      ba, bx, lam, wout)


def _rope_tables(pos):
    half = RET_DK // 2
    inv = ROPE_BASE ** (-jnp.arange(half, dtype=F32) / half)
    ang = pos[:, None] * inv[None, :]
    cos, sin = jnp.cos(ang), jnp.sin(ang)
    return jnp.concatenate([cos, cos], axis=1), jnp.concatenate([-sin, sin], axis=1)


def _decay_tables(c):
    lg = jnp.log1p(-jnp.exp2(-5.0 - jnp.arange(RET_HEADS, dtype=F32)))
    idx = jnp.arange(c, dtype=F32)
    diff = idx[:, None] - idx[None, :]
    causal = diff >= 0
    dmat = jnp.where(causal[None], jnp.exp(jnp.where(causal, diff, 0.0)[None] * lg[:, None, None]), 0.0)
    zeta = jnp.exp((c - 1.0 - idx)[None, :] * lg[:, None])
    xi = jnp.exp((idx + 1.0)[None, :] * lg[:, None])
    gdec = jnp.exp(c * lg)
    return lg, dmat, zeta, xi, gdec


def _gate_tiles(wa, wx):
    z = jnp.zeros((LRU_BW, LRU_BW), wa.dtype)
    bd = lambda w, g: jnp.block([[w[2 * g], z], [z, w[2 * g + 1]]])
    tiles = [jnp.concatenate([bd(wa, g), bd(wx, g)], axis=1) for g in range(LRU_BLOCKS // 2)]
    return jnp.stack(tiles).astype(BF16)


def kernel(x_prompt, x_sample, state_ret, state_lru, state_conv, norm1, norm2, norm_f, w_in_a, ret_gn,
           conv_w, conv_b, lru_wa, lru_ba, lru_wx, lru_bx, lru_lambda, w_out_a, w_in_c, sg_norm_g,
           sg_norm_b, sg_ws, sg_bs, w_out_c, ffn_wg, ffn_wu, ffn_wd):
    bp, lp, _ = x_prompt.shape
    bs = x_sample.shape[0]
    row = lambda v: v.reshape(1, -1)

    slabs = lambda wmat: jnp.stack(
        [wmat[:, i * PIECE_COLS:(i + 1) * PIECE_COLS].astype(BF16)
         for i in range(wmat.shape[1] // PIECE_COLS)])
    win_a = slabs(w_in_a[0])
    wout_a = slabs(w_out_a[0])
    wgate = _gate_tiles(lru_wa[0], lru_wx[0])
    mixa_w = (row(ret_gn[0]), conv_w[0], row(conv_b[0]), wgate, row(lru_ba[0]),
              row(lru_bx[0]), row(lru_lambda[0]), wout_a)

    tm = 512
    bt = 16
    cos_p, sin_p = _rope_tables(jnp.arange(lp, dtype=F32))
    cos_s, sin_s = _rope_tables(PAST_LEN + jnp.arange(1, dtype=F32))
    _, dmat, zeta, xi, gdec_p = _decay_tables(RET_CHUNK)
    _, _, _, _, gdec_s = _decay_tables(1)
    bc = lambda t: jnp.broadcast_to(t[:, :, None], (RET_HEADS, RET_CHUNK, RET_DK))
    xp = x_prompt.reshape(bp * lp, D_MODEL)
    xs = x_sample.reshape(bs, D_MODEL)

    xp, ret_p, lru_p, conv_p, wg, wu, wd, win_c, wout_c = _mixa_prompt(
        xp, gdec_p, row(norm1[0]), win_a, cos_p, sin_p, dmat, bc(zeta), bc(xi),
        *mixa_w, ffn_wg, ffn_wu, ffn_wd, w_in_c[0], w_out_c[0], batch=bp, seq=lp, tm=tm)
    ffn = lambda xp_, xs_, layer, final: _ffn(
        xp_, xs_, row(norm2[layer]), wg, wu, wd, row(norm_f), layer=layer,
        final_norm=final, tm=tm)
    xs, ret_s, lru_s, conv_s = _mixa_sample(
        xs, gdec_s, row(norm1[0]), win_a, cos_s, sin_s, state_ret[0],
        state_lru[0], jnp.transpose(state_conv[0], (1, 0, 2)), *mixa_w, bt=bt)
    xp, xs = ffn(xp, xs, 0, False)

    bsb = jnp.broadcast_to(sg_bs[0][:, :, None], (SG_GROUPS, SG_CHUNK, SG_GW))
    sg_scale = row(jnp.repeat(sg_ws[0][:, 0, 0], SG_GW))
    sg_bias = row(jnp.repeat(sg_bs[0][:, 0], SG_GW))
    xs, v_s = _mixc_sample(xs, row(norm1[1]), win_c, row(sg_norm_g[0]), row(sg_norm_b[0]),
                           sg_scale, sg_bias, wout_c)
    y_prompt, y_sample = _mixc_ffn(
        xp, xs, row(norm1[1]), win_c, row(sg_norm_g[0]), row(sg_norm_b[0]), sg_ws[0], bsb, wout_c,
        row(norm2[1]), wg, wu, wd, row(norm_f), layer=1, tm=tm)

    return (y_prompt.reshape(bp, lp, D_MODEL),
            y_sample.reshape(bs, 1, D_MODEL),
            ret_p[None],
            ret_s[None],
            lru_p.reshape(1, bp, LRU_WIDTH),
            lru_s[None],
            conv_p[None],
            jnp.transpose(conv_s, (1, 0, 2))[None],
            v_s.reshape(1, bs, 1, SG_HALF))
```

```python
import functools
import math

import jax
import jax.numpy as jnp
from jax import lax
from jax.experimental import pallas as pl
from jax.experimental.pallas import tpu as pltpu

F32 = jnp.float32
BF16 = jnp.bfloat16

D_MODEL = 1024
RET_WIDTH = 512
RET_HEADS = 4
RET_DK = 128
RET_CHUNK = 128
ROPE_BASE = 10000.0
LRU_WIDTH = 512
LRU_BLOCKS = 8
LRU_BW = 64
LRU_C = 8.0
CONV_W = 4
SG_CHUNK = 128
SG_HALF = 1024
SG_GROUPS = 8
SG_GW = 128
D_FF = 2816
EPS = 1e-6
PAST_LEN = 16384

V7X_VMEM_BYTES = 64 * 1024 * 1024
SUBLANES = 8
LANES = 128
V7X_MXU_COLS = 256
PIECE_COLS = V7X_MXU_COLS
PIECES_PER_STAGE = 1

Q_SCALE = RET_DK ** -0.5
GELU_C = math.sqrt(2.0 / math.pi)
LOG2_E = 1.0 / math.log(2.0)


def _vmem_limit(estimate_bytes):
    return int(min(estimate_bytes * 3 // 2 + (8 << 20), V7X_VMEM_BYTES - (6 << 20)))


def _resident(shape):
    nd = len(shape)
    return pl.BlockSpec(shape, lambda *_: (0,) * nd, pipeline_mode=pl.Buffered(1))


def _dot(a, b):
    return lax.dot_general(a, b, (((1,), (0,)), ((), ())), preferred_element_type=F32)


def _rms(x, g):
    ms = jnp.mean(x * x, axis=-1, keepdims=True)
    return x * lax.rsqrt(ms + EPS) * g


def _gelu(x):
    k1 = -2.0 * GELU_C * LOG2_E
    k3 = k1 * 0.044715
    return x / (1.0 + jnp.exp2(x * (k3 * (x * x) + k1)))


def _silu(x):
    return x * jax.nn.sigmoid(x)


def _softplus(x):
    return jnp.maximum(x, 0.0) + jnp.log1p(jnp.exp(-jnp.abs(x)))


def _sqrt_nonneg(y):
    return jnp.where(y > 0.0, y * lax.rsqrt(y), 0.0)


def _group_norm(o):
    mu = jnp.mean(o, axis=-1, keepdims=True)
    oc = o - mu
    var = jnp.mean(oc * oc, axis=-1, keepdims=True)
    return oc * lax.rsqrt(var + EPS)


def _rotary(x, cosf, sinf):
    return x * cosf + pltpu.roll(x, RET_DK // 2, 1) * sinf


def _lru_coeffs(xc, wgate_ref, ba, bx, spc):
    ra, rx = [], []
    for g in range(LRU_WIDTH // LANES):
        cols = slice(g * LANES, (g + 1) * LANES)
        both = _dot(xc[:, cols], wgate_ref[g])
        ra.append(both[:, :LANES])
        rx.append(both[:, LANES:])
    r = jax.nn.sigmoid(jnp.concatenate(ra, axis=1) + ba)
    i = jax.nn.sigmoid(jnp.concatenate(rx, axis=1) + bx)
    nl = r * spc
    a = jnp.exp2(nl * (-LOG2_E))
    mult = _sqrt_nonneg(jnp.tanh(nl) * (a * a + 1.0))
    return a, xc * i * mult


def _ffn_rows(x, n2, wg_ref, wu_ref, wd_ref, nf, final_norm):
    h = _rms(x, n2).astype(BF16)
    act = (_silu(_dot(h, wg_ref[...])) * _dot(h, wu_ref[...])).astype(BF16)
    y = x + _dot(act, wd_ref[...])
    return _rms(y, nf) if final_norm else y


def _exact_zero_from(v, width):
    bits = pltpu.bitcast(v, jnp.uint32)
    z = lax.shift_right_logical(lax.shift_right_logical(bits, jnp.uint32(16)), jnp.uint32(16))
    folded = z[:, :width]
    for c0 in range(width, v.shape[1], width):
        folded = folded | z[:, c0:c0 + width]
    return pltpu.bitcast(folded, F32)


def _ffn_kernel(x_ref, xn_ref, xs_ref, n2_ref, wg_ref, wu_ref, wd_ref, nf_ref, o_ref, os_ref,
                h_s, *y_s, final_norm, prompt_steps):
    step = pl.program_id(0)
    args = (n2_ref[...], wg_ref, wu_ref, wd_ref, nf_ref[...], final_norm)
    slab = V7X_MXU_COLS

    @pl.when(step == 0)
    def _():
        h_s[...] = _rms(x_ref[...], n2_ref[...]).astype(BF16)
        if final_norm:
            y_s[0][...] = jnp.zeros_like(y_s[0])

    @pl.when(step < prompt_steps)
    def _():
        h = h_s[...]
        act = _silu(_dot(h, wg_ref[...])) * _dot(h, wu_ref[...])
        nxt = _rms(xn_ref[...], n2_ref[...])
        h_s[...] = nxt.astype(BF16)
        tie = _exact_zero_from(nxt, slab)
        if final_norm:
            prev = _rms(y_s[0][...], nf_ref[...])
            o_ref[...] = prev
            tie = tie + _exact_zero_from(prev, slab)
        head = act[:, :D_FF - slab].astype(BF16)
        last = (act[:, D_FF - slab:] + tie).astype(BF16)
        y = (x_ref[...] + _dot(head, wd_ref[:D_FF - slab, :])) + _dot(last, wd_ref[D_FF - slab:, :])
        if final_norm:
            y_s[0][...] = y
        else:
            o_ref[...] = y

    @pl.when(step == prompt_steps)
    def _():
        if final_norm:
            o_ref[...] = _rms(y_s[0][...], nf_ref[...])
        os_ref[...] = _ffn_rows(xs_ref[...], *args)


def _ffn(x, xs, n2, wg, wu, wd, nf, *, layer, final_norm, tm):
    t, ts = x.shape[0], xs.shape[0]
    steps = t // tm
    est = (3 * D_MODEL * D_FF * 2 + 7 * tm * D_MODEL * 4 + 4 * ts * D_MODEL * 4
           + 3 * tm * D_FF * 4)
    layer_blk = lambda shape: pl.BlockSpec((None,) + shape, lambda i: (layer, 0, 0),
                                           pipeline_mode=pl.Buffered(1))
    prompt_blk = pl.BlockSpec((tm, D_MODEL), lambda i: (jnp.minimum(i, steps - 1), 0))
    next_blk = pl.BlockSpec((tm, D_MODEL), lambda i: (jnp.minimum(i + 1, steps - 1), 0))
    out_blk = (pl.BlockSpec((tm, D_MODEL), lambda i: (jnp.maximum(i - 1, 0), 0))
               if final_norm else prompt_blk)
    scratch = [pltpu.VMEM((tm, D_MODEL), BF16)]
    if final_norm:
        scratch.append(pltpu.VMEM((tm, D_MODEL), F32))
    return pl.pallas_call(
        functools.partial(_ffn_kernel, final_norm=final_norm, prompt_steps=steps),
        grid=(steps + 1,),
        in_specs=[
            prompt_blk,
            next_blk,
            _resident((ts, D_MODEL)),
            _resident((1, D_MODEL)),
            layer_blk((D_MODEL, D_FF)),
            layer_blk((D_MODEL, D_FF)),
            layer_blk((D_FF, D_MODEL)),
            _resident((1, D_MODEL)),
        ],
        out_specs=[out_blk, pl.BlockSpec((ts, D_MODEL), lambda i: (0, 0))],
        out_shape=[jax.ShapeDtypeStruct((t, D_MODEL), F32),
                   jax.ShapeDtypeStruct((ts, D_MODEL), F32)],
        scratch_shapes=scratch,
        compiler_params=pltpu.CompilerParams(
            dimension_semantics=("arbitrary",), vmem_limit_bytes=_vmem_limit(est)),
        name="ffn_final" if final_norm else "ffn",
    )(x, x, xs, n2, wg, wu, wd, nf)


def _sgu_u(hn, win_ref):
    return _gelu(_dot(hn, win_ref[0]))


def _sgu_v(hn, win_ref, vg, vb):
    zv = _gelu(_dot(hn, win_ref[1]))
    mu = jnp.mean(zv, axis=-1, keepdims=True)
    vc = zv - mu
    var = jnp.mean(vc * vc, axis=-1, keepdims=True)
    return vc * lax.rsqrt(var + EPS) * vg + vb


def _mixc_prompt_kernel(x_ref, n1_ref, win_ref, vg_ref, vb_ref, ws_ref, bsb_ref, wout_ref,
                        o_ref, u_s, v_s, gated_s, wm_s, *, tm):
    half = tm // 2
    nchunk = half // SG_CHUNK
    ri = lax.broadcasted_iota(jnp.int32, (SG_CHUNK, SG_CHUNK), 0)
    ci = lax.broadcasted_iota(jnp.int32, (SG_CHUNK, SG_CHUNK), 1)
    for g in range(SG_GROUPS):
        wm_s[g] = jnp.where(ri >= ci, ws_ref[g], 0.0).astype(BF16)

    def norm(r0):
        return _rms(x_ref[r0:r0 + half, :], n1_ref[...])

    def put_u(r0, hn):
        u_s[r0:r0 + half, :] = _sgu_u(hn, win_ref)

    def put_v(r0, hn):
        v_s[r0:r0 + half, :] = _sgu_v(hn, win_ref, vg_ref[...], vb_ref[...]).astype(BF16)

    def gate(r0):
        for g in range(SG_GROUPS):
            cols = slice(g * SG_GW, (g + 1) * SG_GW)
            vcat = jnp.concatenate(
                [v_s[r0 + c * SG_CHUNK:r0 + (c + 1) * SG_CHUNK, cols] for c in range(nchunk)],
                axis=1)
            sv_all = _dot(wm_s[g], vcat)
            for c in range(nchunk):
                rows = slice(r0 + c * SG_CHUNK, r0 + (c + 1) * SG_CHUNK)
                sv = sv_all[:, c * SG_GW:(c + 1) * SG_GW] + bsb_ref[g]
                gated_s[rows, cols] = u_s[rows, cols] * sv

    def out_proj(r0):
        rows = slice(r0, r0 + half)
        o_ref[rows, :] = x_ref[rows, :] + _dot(gated_s[rows, :], wout_ref[...])

    hn_a = norm(0)
    put_u(0, hn_a)
    put_v(0, hn_a)
    hn_b = norm(half)
    put_u(half, hn_b)
    gate(0)
    put_v(half, hn_b)
    out_proj(0)
    gate(half)
    out_proj(half)


def _mixc_prompt(x, n1, win, vg, vb, ws, bsb, wout, *, tm):
    t = x.shape[0]
    est = 3 * D_MODEL * SG_HALF * 2 + 4 * tm * D_MODEL * 4 + 5 * tm * 2 * SG_HALF * 4
    return pl.pallas_call(
        functools.partial(_mixc_prompt_kernel, tm=tm),
        grid=(t // tm,),
        in_specs=[
            pl.BlockSpec((tm, D_MODEL), lambda i: (i, 0)),
            _resident((1, D_MODEL)),
            _resident((2, D_MODEL, SG_HALF)),
            _resident((1, SG_HALF)),
            _resident((1, SG_HALF)),
            _resident((SG_GROUPS, SG_CHUNK, SG_CHUNK)),
            _resident((SG_GROUPS, SG_CHUNK, SG_GW)),
            _resident((SG_HALF, D_MODEL)),
        ],
        out_specs=pl.BlockSpec((tm, D_MODEL), lambda i: (i, 0)),
        out_shape=jax.ShapeDtypeStruct((t, D_MODEL), F32),
        scratch_shapes=[
            pltpu.VMEM((tm, SG_HALF), F32),
            pltpu.VMEM((tm, SG_HALF), BF16),
            pltpu.VMEM((tm, SG_HALF), F32),
            pltpu.VMEM((SG_GROUPS, SG_CHUNK, SG_CHUNK), BF16),
        ],
        compiler_params=pltpu.CompilerParams(
            dimension_semantics=("arbitrary",), vmem_limit_bytes=_vmem_limit(est)),
        name="mixc_prompt",
    )(x, n1, win, vg, vb, ws, bsb, wout)


def _mixc_sample_kernel(x_ref, n1_ref, win_ref, vg_ref, vb_ref, scale_ref, bias_ref, wout_ref,
                        o_ref, v_ref):
    x = x_ref[...]
    hn = _rms(x, n1_ref[...])
    u = _sgu_u(hn, win_ref)
    vn = _sgu_v(hn, win_ref, vg_ref[...], vb_ref[...])
    v_ref[...] = vn
    sv = vn * scale_ref[...] + bias_ref[...]
    o_ref[...] = x + _dot(u * sv, wout_ref[...])


def _mixc_sample(x, n1, win, vg, vb, scale, bias, wout):
    t = x.shape[0]
    est = 3 * D_MODEL * SG_HALF * 2 + 8 * t * 2 * SG_HALF * 4
    return pl.pallas_call(
        _mixc_sample_kernel,
        grid=(1,),
        in_specs=[
            _resident((t, D_MODEL)),
            _resident((1, D_MODEL)),
            _resident((2, D_MODEL, SG_HALF)),
            _resident((1, SG_HALF)),
            _resident((1, SG_HALF)),
            _resident((1, SG_HALF)),
            _resident((1, SG_HALF)),
            _resident((SG_HALF, D_MODEL)),
        ],
        out_specs=[
            pl.BlockSpec((t, D_MODEL), lambda i: (0, 0)),
            pl.BlockSpec((t, SG_HALF), lambda i: (0, 0)),
        ],
        out_shape=[
            jax.ShapeDtypeStruct((t, D_MODEL), F32),
            jax.ShapeDtypeStruct((t, SG_HALF), F32),
        ],
        compiler_params=pltpu.CompilerParams(
            dimension_semantics=("arbitrary",), vmem_limit_bytes=_vmem_limit(est)),
        name="mixc_sample",
    )(x, n1, win, vg, vb, scale, bias, wout)


def _lru_scan(a, b, h0):
    tm, w = a.shape
    groups = tm // SUBLANES
    a3 = a.reshape(groups, SUBLANES, w)
    b3 = b.reshape(groups, SUBLANES, w)
    sub = lax.broadcasted_iota(jnp.int32, (groups, SUBLANES, w), 1)
    for s in (1, 2, 4):
        a_sh = pltpu.roll(a3, s, 1)
        b_sh = pltpu.roll(b3, s, 1)
        keep = sub >= s
        b3 = jnp.where(keep, a3 * b_sh + b3, b3)
        a3 = jnp.where(keep, a3 * a_sh, a3)
    hprev = jnp.broadcast_to(h0, (SUBLANES, w))
    hs = []
    for r in range(groups):
        hr = a3[r] * hprev + b3[r]
        hs.append(hr)
        hprev = jnp.broadcast_to(hr[SUBLANES - 1:SUBLANES, :], (SUBLANES, w))
    return jnp.concatenate(hs, axis=0), hprev[0:1, :]


def _mixa_prompt_kernel(gdec_ref, x_ref, xn_ref, n1_ref, win_ref, cos_ref, sin_ref, dmat_ref,
                        zeta_ref, xi_ref, gn_ref, cw_ref, cbias_ref,
                        wgate_ref, ba_ref, bx_ref, lam_ref, wout_ref,
                        wg_ref, wu_ref, wd_ref, winc_ref, woutc_ref,
                        o_ref, s_ref, hT_ref, cT_ref,
                        wg_o, wu_o, wd_o, winc_o, woutc_o,
                        z_s, xbn_s, xp_s, ymix_s, sb_s, *, tm, nt):
    half = tm // 2
    w = RET_WIDTH
    pad = SUBLANES
    tail = CONV_W - 1
    step = pl.program_id(0)
    ZQ, ZK, ZV, ZG, ZGB = range(5)

    pw = PIECE_COLS
    per_blk = w // pw

    def stash_xb(j, v):
        xbn_s[:, j * pw:(j + 1) * pw] = v

    def window_xb(j, v):
        xp_s[pad + half:pad + tm, j * pw:(j + 1) * pw] = v

    def projection(src_ref, src_row0, z_row0, put_xb):
        hn = []

        def xb_piece(j):
            def run():
                if j == 0:
                    hn.append(
                        _rms(src_ref[src_row0:src_row0 + half, :], n1_ref[...]).astype(BF16))
                put_xb(j, _dot(hn[0], win_ref[4 * per_blk + j]))
            return run

        def z_piece(win_blk, z_blk, j):
            def run():
                z_s[z_row0:z_row0 + half, z_blk * w + j * pw:z_blk * w + (j + 1) * pw] = _dot(
                    hn[0], win_ref[win_blk * per_blk + j])
            return run

        pieces = [xb_piece(j) for j in range(per_blk)]
        for win_blk, z_blk in ((5, ZGB), (0, ZQ), (1, ZK), (2, ZV), (3, ZG)):
            pieces += [z_piece(win_blk, z_blk, j) for j in range(per_blk)]
        return pieces

    @pl.when(step == 0)
    def _():
        for piece in projection(x_ref, 0, 0, stash_xb):
            piece()

    @pl.when(lax.rem(step, nt) == 0)
    def _():
        s_ref[...] = jnp.zeros_like(s_ref)
        hT_ref[...] = jnp.zeros_like(hT_ref)
        cT_ref[...] = jnp.zeros_like(cT_ref)

    xp_s[pad - tail:pad, :] = cT_ref[0]
    xp_s[pad:pad + half, :] = xbn_s[...]
    for h in range(RET_HEADS):
        sb_s[h] = s_ref[0, h].astype(BF16)
    hprev = [hT_ref[0]]
    sp = LRU_C * _softplus(-lam_ref[...])
    heads = range(RET_HEADS)

    def lru_unit(c, emit):
        rows = slice(c * RET_CHUNK, (c + 1) * RET_CHUNK)
        xc = cbias_ref[...]
        for j in reversed(range(CONV_W)):
            off = pad - tail + j + c * RET_CHUNK
            xc = xc + xp_s[off:off + RET_CHUNK, :] * cw_ref[j:j + 1, :]
        emit()
        a, b_in = _lru_coeffs(xc, wgate_ref, ba_ref[...], bx_ref[...], sp)
        emit()
        hl, hprev[0] = _lru_scan(a, b_in, hprev[0])
        ymix_s[rows, RET_WIDTH:] = (hl * _gelu(z_s[rows, ZGB * w:(ZGB + 1) * w])).astype(BF16)

    def ret_unit(c, emit):
        rows = slice(c * RET_CHUNK, (c + 1) * RET_CHUNK)
        zcol = lambda blk, h: slice(blk * w + h * LANES, blk * w + (h + 1) * LANES)
        cols = lambda h: slice(h * RET_DK, (h + 1) * RET_DK)
        cosf = cos_ref[rows, :]
        sinf = sin_ref[rows, :]
        qr = [_rotary(z_s[rows, zcol(ZQ, h)], cosf, sinf) * Q_SCALE for h in heads]
        kr = [_rotary(z_s[rows, zcol(ZK, h)], cosf, sinf) for h in heads]
        vb = [z_s[rows, zcol(ZV, h)].astype(BF16) for h in heads]
        emit()
        sc = [lax.dot_general(qr[h], kr[h].astype(BF16), (((1,), (1,)), ((), ())),
                              preferred_element_type=F32) for h in heads]
        emit()
        lhs = [jnp.concatenate([sc[h] * dmat_ref[h], qr[h] * xi_ref[h]], axis=1) for h in heads]
        o = [_dot(lhs[h], jnp.concatenate([vb[h], sb_s[h]], axis=0)) for h in heads]
        u = [lax.dot_general((kr[h] * zeta_ref[h]).astype(BF16), vb[h], (((0,), (0,)), ((), ())),
                             preferred_element_type=F32) for h in heads]
        emit()
        for h in heads:
            ya = _group_norm(o[h]) * gn_ref[:, cols(h)] * _silu(z_s[rows, zcol(ZG, h)])
            ymix_s[rows, cols(h)] = ya.astype(BF16)
        emit()
        for h in heads:
            s_new = gdec_ref[h] * s_ref[0, h] + u[h]
            s_ref[0, h] = s_new
            sb_s[h] = s_new.astype(BF16)

    def out_proj(row0):
        rows = slice(row0, row0 + half)

        def piece(j):
            def run():
                cols = slice(j * pw, (j + 1) * pw)
                o_ref[rows, cols] = x_ref[rows, cols] + _dot(ymix_s[rows, :], wout_ref[j])
            return run

        return [piece(j) for j in range(D_MODEL // pw)]

    def run_half(first_chunk, pieces):
        pending = iter(pieces)

        def emit():
            for _ in range(PIECES_PER_STAGE):
                next(pending, lambda: None)()

        for c in range(first_chunk, first_chunk + half // RET_CHUNK):
            emit()
            lru_unit(c, emit)
            emit()
            ret_unit(c, emit)
        for piece in pending:
            piece()

    run_half(0, projection(x_ref, half, half, window_xb))
    run_half(half // RET_CHUNK, out_proj(0) + projection(xn_ref, 0, 0, stash_xb))
    for piece in out_proj(half):
        piece()

    hT_ref[0] = hprev[0]
    cT_ref[0] = xp_s[pad + tm - tail:pad + tm, :]

    wg_o[...] = wg_ref[...].astype(BF16)
    wu_o[...] = wu_ref[...].astype(BF16)
    wd_o[...] = wd_ref[...].astype(BF16)
    for i in range(2):
        winc_o[i] = winc_ref[:, i * SG_HALF:(i + 1) * SG_HALF].astype(BF16)
    woutc_o[...] = woutc_ref[...].astype(BF16)


def _mixa_prompt(x, gdec, n1, win, cosf, sinf, dmat, zeta, xi, gn, cw, cbias,
                 wgate, ba, bx, lam, wout, wg, wu, wd, winc, woutc, *, batch, seq, tm):
    nt = seq // tm
    steps = batch * nt
    nlayer = wg.shape[0]
    per_layer = steps // nlayer
    ff_rows = D_MODEL // per_layer
    fd_rows = D_FF // per_layer
    c_rows = D_MODEL // steps
    band = lambda i: (i // per_layer, lax.rem(i, per_layer), 0)
    half = tm // 2
    in_w = 6 * RET_WIDTH
    est = (D_MODEL * in_w * 4 + D_MODEL * D_MODEL * 4 + 6 * tm * D_MODEL * 4
           + tm * 5 * RET_WIDTH * 4 + 4 * tm * LRU_WIDTH * 4)
    row_blk = lambda i: (i, 0)
    next_half = lambda i: (jnp.minimum(2 * (i + 1), 2 * steps - 2), 0)
    pos_blk = lambda i: (lax.rem(i, nt), 0)
    per_b4 = lambda i: (i // nt, 0, 0, 0)
    per_b3 = lambda i: (i // nt, 0, 0)
    hd = (RET_HEADS, RET_CHUNK, RET_CHUNK)
    return pl.pallas_call(
        functools.partial(_mixa_prompt_kernel, tm=tm, nt=nt),
        grid=(steps,),
        in_specs=[
            pl.BlockSpec(memory_space=pltpu.SMEM),
            pl.BlockSpec((tm, D_MODEL), row_blk),
            pl.BlockSpec((half, D_MODEL), next_half),
            _resident((1, D_MODEL)),
            _resident((in_w // PIECE_COLS, D_MODEL, PIECE_COLS)),
            pl.BlockSpec((tm, RET_DK), pos_blk),
            pl.BlockSpec((tm, RET_DK), pos_blk),
            _resident(hd),
            _resident(hd),
            _resident(hd),
            _resident((1, RET_WIDTH)),
            _resident((CONV_W, LRU_WIDTH)),
            _resident((1, LRU_WIDTH)),
            _resident((LRU_WIDTH // LANES, LANES, 2 * LANES)),
            _resident((1, LRU_WIDTH)),
            _resident((1, LRU_WIDTH)),
            _resident((1, LRU_WIDTH)),
            _resident((D_MODEL // PIECE_COLS, D_MODEL, PIECE_COLS)),
            pl.BlockSpec((1, ff_rows, D_FF), band),
            pl.BlockSpec((1, ff_rows, D_FF), band),
            pl.BlockSpec((1, fd_rows, D_MODEL), band),
            pl.BlockSpec((c_rows, 2 * SG_HALF), row_blk),
            pl.BlockSpec((c_rows, D_MODEL), row_blk),
        ],
        out_specs=[
            pl.BlockSpec((tm, D_MODEL), row_blk),
            pl.BlockSpec((1, RET_HEADS, RET_DK, RET_DK), per_b4),
            pl.BlockSpec((1, 1, LRU_WIDTH), per_b3),
            pl.BlockSpec((1, CONV_W - 1, LRU_WIDTH), per_b3),
            pl.BlockSpec((1, ff_rows, D_FF), band),
            pl.BlockSpec((1, ff_rows, D_FF), band),
            pl.BlockSpec((1, fd_rows, D_MODEL), band),
            pl.BlockSpec((2, c_rows, SG_HALF), lambda i: (0, i, 0)),
            pl.BlockSpec((c_rows, D_MODEL), row_blk),
        ],
        out_shape=[
            jax.ShapeDtypeStruct((batch * seq, D_MODEL), F32),
            jax.ShapeDtypeStruct((batch, RET_HEADS, RET_DK, RET_DK), F32),
            jax.ShapeDtypeStruct((batch, 1, LRU_WIDTH), F32),
            jax.ShapeDtypeStruct((batch, CONV_W - 1, LRU_WIDTH), F32),
            jax.ShapeDtypeStruct(wg.shape, BF16),
            jax.ShapeDtypeStruct(wu.shape, BF16),
            jax.ShapeDtypeStruct(wd.shape, BF16),
            jax.ShapeDtypeStruct((2, D_MODEL, SG_HALF), BF16),
            jax.ShapeDtypeStruct(woutc.shape, BF16),
        ],
        scratch_shapes=[
            pltpu.VMEM((tm, 5 * RET_WIDTH), F32),
            pltpu.VMEM((half, LRU_WIDTH), F32),
            pltpu.VMEM((tm + 2 * SUBLANES, LRU_WIDTH), F32),
            pltpu.VMEM((tm, D_MODEL), BF16),
            pltpu.VMEM((RET_HEADS, RET_DK, RET_DK), BF16),
        ],
        compiler_params=pltpu.CompilerParams(
            dimension_semantics=("arbitrary",), vmem_limit_bytes=_vmem_limit(est)),
        name="mixa_prompt",
    )(gdec, x, x, n1, win, cosf, sinf, dmat, zeta, xi, gn, cw, cbias, wgate,
      ba, bx, lam, wout, wg, wu, wd, winc, woutc)


def _mixa_sample_kernel(gdec_ref, x_ref, n1_ref, win_ref, cos_ref, sin_ref,
                        s0_ref, h0_ref, cb0_ref, gn_ref, cw_ref, cbias_ref, wgate_ref,
                        ba_ref, bx_ref, lam_ref, wout_ref,
                        o_ref, s_ref, hT_ref, cT_ref,
                        q_s, k_s, v_s, sg_s, ymix_s, *, bt):
    step = pl.program_id(0)
    w = RET_WIDTH

    @pl.when(step == 0)
    def _():
        hn = _rms(x_ref[...], n1_ref[...])
        per_blk = w // PIECE_COLS
        proj = lambda blk: jnp.concatenate(
            [_dot(hn, win_ref[blk * per_blk + j]) for j in range(per_blk)], axis=1)
        cosf = cos_ref[...]
        sinf = sin_ref[...]
        zq = proj(0)
        zk = proj(1)
        for h in range(RET_HEADS):
            cols = slice(h * RET_DK, (h + 1) * RET_DK)
            q_s[:, cols] = (_rotary(zq[:, cols], cosf, sinf) * Q_SCALE).astype(BF16)
            k_s[:, cols] = _rotary(zk[:, cols], cosf, sinf).astype(BF16)
        v_s[...] = proj(2)
        sg_s[...] = _silu(proj(3))
        xb = proj(4)
        gg = _gelu(proj(5))
        xc = cbias_ref[...]
        for j in range(CONV_W - 1):
            xc = xc + cb0_ref[j] * cw_ref[j:j + 1, :]
        xc = xc + xb * cw_ref[CONV_W - 1:CONV_W, :]
        for j in range(CONV_W - 2):
            cT_ref[j] = cb0_ref[j + 1]
        cT_ref[CONV_W - 2] = xb
        a, b_in = _lru_coeffs(xc, wgate_ref, ba_ref[...], bx_ref[...],
                              LRU_C * _softplus(-lam_ref[...]))
        hnew = a * h0_ref[...] + b_in
        hT_ref[...] = hnew
        ymix_s[:, RET_WIDTH:] = (hnew * gg).astype(BF16)

    r0 = pl.multiple_of(step * bt, bt)
    rows = pl.ds(r0, bt)
    rowi = lax.broadcasted_iota(jnp.int32, (bt, RET_DK), 0)
    for h in range(RET_HEADS):
        cols = slice(h * RET_DK, (h + 1) * RET_DK)
        qs = q_s[rows, cols]
        kb = k_s[rows, cols]
        vh = v_s[rows, cols]
        qk = jnp.sum(qs.astype(F32) * kb.astype(F32), axis=-1, keepdims=True)
        cross = jnp.zeros((bt, RET_DK), F32)
        for b in range(bt):
            s_old = s0_ref[b, h]
            cr = _dot(qs, s_old.astype(BF16))
            cross = cross + jnp.where(rowi == b, cr, 0.0)
            u = lax.dot_general(kb, jnp.where(rowi == b, vh, 0.0).astype(BF16),
                                (((0,), (0,)), ((), ())), preferred_element_type=F32)
            s_ref[b, h] = gdec_ref[h] * s_old + u
        o = qk * vh + cross * gdec_ref[h]
        ymix_s[rows, cols] = (_group_norm(o) * gn_ref[:, cols] * sg_s[rows, cols]).astype(BF16)

    @pl.when(step == pl.num_programs(0) - 1)
    def _():
        ymix = ymix_s[...]
        o_ref[...] = x_ref[...] + jnp.concatenate(
            [_dot(ymix, wout_ref[j]) for j in range(D_MODEL // PIECE_COLS)], axis=1)


def _mixa_sample(x, gdec, n1, win, cosf, sinf, s0, h0, cb0, gn, cw, cbias,
                 wgate, ba, bx, lam, wout, *, bt):
    batch = x.shape[0]
    in_w = 6 * RET_WIDTH
    est = (D_MODEL * in_w * 2 + D_MODEL * D_MODEL * 2
           + 4 * bt * RET_HEADS * RET_DK * RET_DK * 4 + 12 * batch * in_w * 4)
    return pl.pallas_call(
        functools.partial(_mixa_sample_kernel, bt=bt),
        grid=(batch // bt,),
        in_specs=[
            pl.BlockSpec(memory_space=pltpu.SMEM),
            _resident((batch, D_MODEL)),
            _resident((1, D_MODEL)),
            _resident((in_w // PIECE_COLS, D_MODEL, PIECE_COLS)),
            _resident((1, RET_DK)),
            _resident((1, RET_DK)),
            pl.BlockSpec((bt, RET_HEADS, RET_DK, RET_DK), lambda i: (i, 0, 0, 0)),
            _resident((batch, LRU_WIDTH)),
            _resident((CONV_W - 1, batch, LRU_WIDTH)),
            _resident((1, RET_WIDTH)),
            _resident((CONV_W, LRU_WIDTH)),
            _resident((1, LRU_WIDTH)),
            _resident((LRU_WIDTH // LANES, LANES, 2 * LANES)),
            _resident((1, LRU_WIDTH)),
            _resident((1, LRU_WIDTH)),
            _resident((1, LRU_WIDTH)),
            _resident((D_MODEL // PIECE_COLS, D_MODEL, PIECE_COLS)),
        ],
        out_specs=[
            pl.BlockSpec((batch, D_MODEL), lambda i: (0, 0)),
            pl.BlockSpec((bt, RET_HEADS, RET_DK, RET_DK), lambda i: (i, 0, 0, 0)),
            pl.BlockSpec((batch, LRU_WIDTH), lambda i: (0, 0)),
            pl.BlockSpec((CONV_W - 1, batch, LRU_WIDTH), lambda i: (0, 0, 0)),
        ],
        out_shape=[
            jax.ShapeDtypeStruct((batch, D_MODEL), F32),
            jax.ShapeDtypeStruct((batch, RET_HEADS, RET_DK, RET_DK), F32),
            jax.ShapeDtypeStruct((batch, LRU_WIDTH), F32),
            jax.ShapeDtypeStruct((CONV_W - 1, batch, LRU_WIDTH), F32),
        ],
        scratch_shapes=[
            pltpu.VMEM((batch, RET_WIDTH), BF16),
            pltpu.VMEM((batch, RET_WIDTH), BF16),
            pltpu.VMEM((batch, RET_WIDTH), F32),
            pltpu.VMEM((batch, RET_WIDTH), F32),
            pltpu.VMEM((batch, D_MODEL), BF16),
        ],
        compiler_params=pltpu.CompilerParams(
            dimension_semantics=("arbitrary",), vmem_limit_bytes=_vmem_limit(est)),
        name="mixa_sample",
    )(gdec, x, n1, win, cosf, sinf, s0, h0, cb0, gn, cw, cbias, wgate,
      ba, bx, lam, wout)


def _rope_tables(pos):
    half = RET_DK // 2
    inv = ROPE_BASE ** (-jnp.arange(half, dtype=F32) / half)
    ang = pos[:, None] * inv[None, :]
    cos, sin = jnp.cos(ang), jnp.sin(ang)
    return jnp.concatenate([cos, cos], axis=1), jnp.concatenate([-sin, sin], axis=1)


def _decay_tables(c):
    lg = jnp.log1p(-jnp.exp2(-5.0 - jnp.arange(RET_HEADS, dtype=F32)))
    idx = jnp.arange(c, dtype=F32)
    diff = idx[:, None] - idx[None, :]
    causal = diff >= 0
    dmat = jnp.where(causal[None], jnp.exp(jnp.where(causal, diff, 0.0)[None] * lg[:, None, None]), 0.0)
    zeta = jnp.exp((c - 1.0 - idx)[None, :] * lg[:, None])
    xi = jnp.exp((idx + 1.0)[None, :] * lg[:, None])
    gdec = jnp.exp(c * lg)
    return lg, dmat, zeta, xi, gdec


def _gate_tiles(wa, wx):
    z = jnp.zeros((LRU_BW, LRU_BW), wa.dtype)
    bd = lambda w, g: jnp.block([[w[2 * g], z], [z, w[2 * g + 1]]])
    tiles = [jnp.concatenate([bd(wa, g), bd(wx, g)], axis=1) for g in range(LRU_BLOCKS // 2)]
    return jnp.stack(tiles).astype(BF16)


def kernel(x_prompt, x_sample, state_ret, state_lru, state_conv, norm1, norm2, norm_f, w_in_a, ret_gn,
           conv_w, conv_b, lru_wa, lru_ba, lru_wx, lru_bx, lru_lambda, w_out_a, w_in_c, sg_norm_g,
           sg_norm_b, sg_ws, sg_bs, w_out_c, ffn_wg, ffn_wu, ffn_wd):
    bp, lp, _ = x_prompt.shape
    bs = x_sample.shape[0]
    row = lambda v: v.reshape(1, -1)

    slabs = lambda wmat: jnp.stack(
        [wmat[:, i * PIECE_COLS:(i + 1) * PIECE_COLS].astype(BF16)
         for i in range(wmat.shape[1] // PIECE_COLS)])
    win_a = slabs(w_in_a[0])
    wout_a = slabs(w_out_a[0])
    wgate = _gate_tiles(lru_wa[0], lru_wx[0])
    mixa_w = (row(ret_gn[0]), conv_w[0], row(conv_b[0]), wgate, row(lru_ba[0]),
              row(lru_bx[0]), row(lru_lambda[0]), wout_a)

    tm = 512
    bt = 16
    cos_p, sin_p = _rope_tables(jnp.arange(lp, dtype=F32))
    cos_s, sin_s = _rope_tables(PAST_LEN + jnp.arange(1, dtype=F32))
    _, dmat, zeta, xi, gdec_p = _decay_tables(RET_CHUNK)
    _, _, _, _, gdec_s = _decay_tables(1)
    bc = lambda t: jnp.broadcast_to(t[:, :, None], (RET_HEADS, RET_CHUNK, RET_DK))
    xp = x_prompt.reshape(bp * lp, D_MODEL)
    xs = x_sample.reshape(bs, D_MODEL)

    xp, ret_p, lru_p, conv_p, wg, wu, wd, win_c, wout_c = _mixa_prompt(
        xp, gdec_p, row(norm1[0]), win_a, cos_p, sin_p, dmat, bc(zeta), bc(xi),
        *mixa_w, ffn_wg, ffn_wu, ffn_wd, w_in_c[0], w_out_c[0], batch=bp, seq=lp, tm=tm)
    ffn = lambda xp_, xs_, layer, final: _ffn(
        xp_, xs_, row(norm2[layer]), wg, wu, wd, row(norm_f), layer=layer,
        final_norm=final, tm=tm)
    xs, ret_s, lru_s, conv_s = _mixa_sample(
        xs, gdec_s, row(norm1[0]), win_a, cos_s, sin_s, state_ret[0],
        state_lru[0], jnp.transpose(state_conv[0], (1, 0, 2)), *mixa_w, bt=bt)
    xp, xs = ffn(xp, xs, 0, False)

    bsb = jnp.broadcast_to(sg_bs[0][:, :, None], (SG_GROUPS, SG_CHUNK, SG_GW))
    xp = _mixc_prompt(xp, row(norm1[1]), win_c, row(sg_norm_g[0]), row(sg_norm_b[0]), sg_ws[0],
                      bsb, wout_c, tm=2 * tm)
    sg_scale = row(jnp.repeat(sg_ws[0][:, 0, 0], SG_GW))
    sg_bias = row(jnp.repeat(sg_bs[0][:, 0], SG_GW))
    xs, v_s = _mixc_sample(xs, row(norm1[1]), win_c, row(sg_norm_g[0]), row(sg_norm_b[0]),
                           sg_scale, sg_bias, wout_c)
    y_prompt, y_sample = ffn(xp, xs, 1, True)

    return (y_prompt.reshape(bp, lp, D_MODEL),
            y_sample.reshape(bs, 1, D_MODEL),
            ret_p[None],
            ret_s[None],
            lru_p.reshape(1, bp, LRU_WIDTH),
            lru_s[None],
            conv_p[None],
            jnp.transpose(conv_s, (1, 0, 2))[None],
            v_s.reshape(1, bs, 1, SG_HALF))
```

```python
import functools
import math

import jax
import jax.numpy as jnp
from jax import lax
from jax.experimental import pallas as pl
from jax.experimental.pallas import tpu as pltpu

F32 = jnp.float32
BF16 = jnp.bfloat16

D_MODEL = 1024
RET_WIDTH = 512
RET_HEADS = 4
RET_DK = 128
RET_CHUNK = 128
ROPE_BASE = 10000.0
LRU_WIDTH = 512
LRU_BLOCKS = 8
LRU_BW = 64
LRU_C = 8.0
CONV_W = 4
SG_CHUNK = 128
SG_HALF = 1024
SG_GROUPS = 8
SG_GW = 128
D_FF = 2816
EPS = 1e-6
PAST_LEN = 16384

V7X_VMEM_BYTES = 64 * 1024 * 1024
SUBLANES = 8
LANES = 128
V7X_MXU_COLS = 256
PIECE_COLS = V7X_MXU_COLS
PIECES_PER_STAGE = 1

Q_SCALE = RET_DK ** -0.5
GELU_C = math.sqrt(2.0 / math.pi)
LOG2_E = 1.0 / math.log(2.0)


def _vmem_limit(estimate_bytes):
    return int(min(estimate_bytes * 3 // 2 + (8 << 20), V7X_VMEM_BYTES - (6 << 20)))


def _resident(shape):
    nd = len(shape)
    return pl.BlockSpec(shape, lambda *_: (0,) * nd, pipeline_mode=pl.Buffered(1))


def _dot(a, b):
    return lax.dot_general(a, b, (((1,), (0,)), ((), ())), preferred_element_type=F32)


def _rms(x, g):
    ms = jnp.mean(x * x, axis=-1, keepdims=True)
    return x * lax.rsqrt(ms + EPS) * g


def _gelu(x):
    k1 = -2.0 * GELU_C * LOG2_E
    k3 = k1 * 0.044715
    return x / (1.0 + jnp.exp2(x * (k3 * (x * x) + k1)))


def _silu(x):
    return x * jax.nn.sigmoid(x)


def _softplus(x):
    return jnp.maximum(x, 0.0) + jnp.log1p(jnp.exp(-jnp.abs(x)))


def _sqrt_nonneg(y):
    return jnp.where(y > 0.0, y * lax.rsqrt(y), 0.0)


def _group_norm(o):
    mu = jnp.mean(o, axis=-1, keepdims=True)
    oc = o - mu
    var = jnp.mean(oc * oc, axis=-1, keepdims=True)
    return oc * lax.rsqrt(var + EPS)


def _rotary(x, cosf, sinf):
    return x * cosf + pltpu.roll(x, RET_DK // 2, 1) * sinf


def _lru_coeffs(xc, wgate_ref, ba, bx, spc):
    ra, rx = [], []
    for g in range(LRU_WIDTH // LANES):
        cols = slice(g * LANES, (g + 1) * LANES)
        both = _dot(xc[:, cols], wgate_ref[g])
        ra.append(both[:, :LANES])
        rx.append(both[:, LANES:])
    r = jax.nn.sigmoid(jnp.concatenate(ra, axis=1) + ba)
    i = jax.nn.sigmoid(jnp.concatenate(rx, axis=1) + bx)
    nl = r * spc
    a = jnp.exp2(nl * (-LOG2_E))
    mult = _sqrt_nonneg(jnp.tanh(nl) * (a * a + 1.0))
    return a, xc * i * mult


def _ffn_rows(x, n2, wg_ref, wu_ref, wd_ref, nf, final_norm):
    h = _rms(x, n2).astype(BF16)
    act = (_silu(_dot(h, wg_ref[...])) * _dot(h, wu_ref[...])).astype(BF16)
    y = x + _dot(act, wd_ref[...])
    return _rms(y, nf) if final_norm else y


def _exact_zero_from(v, width):
    bits = pltpu.bitcast(v, jnp.uint32)
    z = lax.shift_right_logical(lax.shift_right_logical(bits, jnp.uint32(16)), jnp.uint32(16))
    folded = z[:, :width]
    for c0 in range(width, v.shape[1], width):
        folded = folded | z[:, c0:c0 + width]
    return pltpu.bitcast(folded, F32)


def _ffn_kernel(x_ref, xn_ref, xs_ref, n2_ref, wg_ref, wu_ref, wd_ref, nf_ref, o_ref, os_ref,
                h_s, *, final_norm, prompt_steps):
    step = pl.program_id(0)
    args = (n2_ref[...], wg_ref, wu_ref, wd_ref, nf_ref[...], final_norm)
    slab = V7X_MXU_COLS

    @pl.when(step == 0)
    def _():
        h_s[...] = _rms(x_ref[...], n2_ref[...]).astype(BF16)

    @pl.when(step < prompt_steps)
    def _():
        h = h_s[...]
        act = _silu(_dot(h, wg_ref[...])) * _dot(h, wu_ref[...])
        nxt = _rms(xn_ref[...], n2_ref[...])
        h_s[...] = nxt.astype(BF16)
        head = act[:, :D_FF - slab].astype(BF16)
        last = (act[:, D_FF - slab:] + _exact_zero_from(nxt, slab)).astype(BF16)
        y = (x_ref[...] + _dot(head, wd_ref[:D_FF - slab, :])) + _dot(last, wd_ref[D_FF - slab:, :])
        o_ref[...] = _rms(y, nf_ref[...]) if final_norm else y

    @pl.when(step == prompt_steps)
    def _():
        os_ref[...] = _ffn_rows(xs_ref[...], *args)


def _ffn(x, xs, n2, wg, wu, wd, nf, *, layer, final_norm, tm):
    t, ts = x.shape[0], xs.shape[0]
    steps = t // tm
    est = (3 * D_MODEL * D_FF * 2 + 7 * tm * D_MODEL * 4 + 4 * ts * D_MODEL * 4
           + 3 * tm * D_FF * 4)
    layer_blk = lambda shape: pl.BlockSpec((None,) + shape, lambda i: (layer, 0, 0),
                                           pipeline_mode=pl.Buffered(1))
    prompt_blk = pl.BlockSpec((tm, D_MODEL), lambda i: (jnp.minimum(i, steps - 1), 0))
    next_blk = pl.BlockSpec((tm, D_MODEL), lambda i: (jnp.minimum(i + 1, steps - 1), 0))
    return pl.pallas_call(
        functools.partial(_ffn_kernel, final_norm=final_norm, prompt_steps=steps),
        grid=(steps + 1,),
        in_specs=[
            prompt_blk,
            next_blk,
            _resident((ts, D_MODEL)),
            _resident((1, D_MODEL)),
            layer_blk((D_MODEL, D_FF)),
            layer_blk((D_MODEL, D_FF)),
            layer_blk((D_FF, D_MODEL)),
            _resident((1, D_MODEL)),
        ],
        out_specs=[prompt_blk, pl.BlockSpec((ts, D_MODEL), lambda i: (0, 0))],
        out_shape=[jax.ShapeDtypeStruct((t, D_MODEL), F32),
                   jax.ShapeDtypeStruct((ts, D_MODEL), F32)],
        scratch_shapes=[pltpu.VMEM((tm, D_MODEL), BF16)],
        compiler_params=pltpu.CompilerParams(
            dimension_semantics=("arbitrary",), vmem_limit_bytes=_vmem_limit(est)),
        name="ffn_final" if final_norm else "ffn",
    )(x, x, xs, n2, wg, wu, wd, nf)


SG_PIECES = SG_HALF // PIECE_COLS


def _sgu_half(hn, win_ref, first):
    return jnp.concatenate(
        [_gelu(_dot(hn, win_ref[first + j])) for j in range(SG_PIECES)], axis=1)


def _sgu_u(hn, win_ref):
    return _sgu_half(hn, win_ref, 0)


def _sgu_v(hn, win_ref, vg, vb):
    zv = _sgu_half(hn, win_ref, SG_PIECES)
    mu = jnp.mean(zv, axis=-1, keepdims=True)
    vc = zv - mu
    var = jnp.mean(vc * vc, axis=-1, keepdims=True)
    return vc * lax.rsqrt(var + EPS) * vg + vb


def _mixc_prompt_kernel(x_ref, n1_ref, win_ref, vg_ref, vb_ref, ws_ref, bsb_ref, wout_ref,
                        o_ref, u_s, v_s, gated_s, wm_s, *, tm):
    half = tm // 2
    nchunk = half // SG_CHUNK
    ri = lax.broadcasted_iota(jnp.int32, (SG_CHUNK, SG_CHUNK), 0)
    ci = lax.broadcasted_iota(jnp.int32, (SG_CHUNK, SG_CHUNK), 1)
    for g in range(SG_GROUPS):
        wm_s[g] = jnp.where(ri >= ci, ws_ref[g], 0.0).astype(BF16)

    def norm(r0):
        return _rms(x_ref[r0:r0 + half, :], n1_ref[...])

    def put_u(r0, hn):
        u_s[r0:r0 + half, :] = _sgu_u(hn, win_ref)

    def put_v(r0, hn):
        v_s[r0:r0 + half, :] = _sgu_v(hn, win_ref, vg_ref[...], vb_ref[...]).astype(BF16)

    def gate(r0):
        for g in range(SG_GROUPS):
            cols = slice(g * SG_GW, (g + 1) * SG_GW)
            vcat = jnp.concatenate(
                [v_s[r0 + c * SG_CHUNK:r0 + (c + 1) * SG_CHUNK, cols] for c in range(nchunk)],
                axis=1)
            sv_all = _dot(wm_s[g], vcat)
            for c in range(nchunk):
                rows = slice(r0 + c * SG_CHUNK, r0 + (c + 1) * SG_CHUNK)
                sv = sv_all[:, c * SG_GW:(c + 1) * SG_GW] + bsb_ref[g]
                gated_s[rows, cols] = u_s[rows, cols] * sv

    def out_proj(r0):
        rows = slice(r0, r0 + half)
        for j in range(D_MODEL // PIECE_COLS):
            cols = slice(j * PIECE_COLS, (j + 1) * PIECE_COLS)
            o_ref[rows, cols] = x_ref[rows, cols] + _dot(gated_s[rows, :], wout_ref[j])

    hn_a = norm(0)
    put_u(0, hn_a)
    put_v(0, hn_a)
    hn_b = norm(half)
    put_u(half, hn_b)
    gate(0)
    put_v(half, hn_b)
    out_proj(0)
    gate(half)
    out_proj(half)


def _mixc_prompt(x, n1, win, vg, vb, ws, bsb, wout, *, tm):
    t = x.shape[0]
    est = 3 * D_MODEL * SG_HALF * 2 + 4 * tm * D_MODEL * 4 + 5 * tm * 2 * SG_HALF * 4
    return pl.pallas_call(
        functools.partial(_mixc_prompt_kernel, tm=tm),
        grid=(t // tm,),
        in_specs=[
            pl.BlockSpec((tm, D_MODEL), lambda i: (i, 0)),
            _resident((1, D_MODEL)),
            _resident((2 * SG_PIECES, D_MODEL, PIECE_COLS)),
            _resident((1, SG_HALF)),
            _resident((1, SG_HALF)),
            _resident((SG_GROUPS, SG_CHUNK, SG_CHUNK)),
            _resident((SG_GROUPS, SG_CHUNK, SG_GW)),
            _resident((D_MODEL // PIECE_COLS, SG_HALF, PIECE_COLS)),
        ],
        out_specs=pl.BlockSpec((tm, D_MODEL), lambda i: (i, 0)),
        out_shape=jax.ShapeDtypeStruct((t, D_MODEL), F32),
        scratch_shapes=[
            pltpu.VMEM((tm, SG_HALF), F32),
            pltpu.VMEM((tm, SG_HALF), BF16),
            pltpu.VMEM((tm, SG_HALF), F32),
            pltpu.VMEM((SG_GROUPS, SG_CHUNK, SG_CHUNK), BF16),
        ],
        compiler_params=pltpu.CompilerParams(
            dimension_semantics=("arbitrary",), vmem_limit_bytes=_vmem_limit(est)),
        name="mixc_prompt",
    )(x, n1, win, vg, vb, ws, bsb, wout)


def _mixc_sample_kernel(x_ref, n1_ref, win_ref, vg_ref, vb_ref, scale_ref, bias_ref, wout_ref,
                        o_ref, v_ref):
    x = x_ref[...]
    hn = _rms(x, n1_ref[...])
    u = _sgu_u(hn, win_ref)
    vn = _sgu_v(hn, win_ref, vg_ref[...], vb_ref[...])
    v_ref[...] = vn
    sv = vn * scale_ref[...] + bias_ref[...]
    gated = u * sv
    o_ref[...] = x + jnp.concatenate(
        [_dot(gated, wout_ref[j]) for j in range(D_MODEL // PIECE_COLS)], axis=1)


def _mixc_sample(x, n1, win, vg, vb, scale, bias, wout):
    t = x.shape[0]
    est = 3 * D_MODEL * SG_HALF * 2 + 8 * t * 2 * SG_HALF * 4
    return pl.pallas_call(
        _mixc_sample_kernel,
        grid=(1,),
        in_specs=[
            _resident((t, D_MODEL)),
            _resident((1, D_MODEL)),
            _resident((2 * SG_PIECES, D_MODEL, PIECE_COLS)),
            _resident((1, SG_HALF)),
            _resident((1, SG_HALF)),
            _resident((1, SG_HALF)),
            _resident((1, SG_HALF)),
            _resident((D_MODEL // PIECE_COLS, SG_HALF, PIECE_COLS)),
        ],
        out_specs=[
            pl.BlockSpec((t, D_MODEL), lambda i: (0, 0)),
            pl.BlockSpec((t, SG_HALF), lambda i: (0, 0)),
        ],
        out_shape=[
            jax.ShapeDtypeStruct((t, D_MODEL), F32),
            jax.ShapeDtypeStruct((t, SG_HALF), F32),
        ],
        compiler_params=pltpu.CompilerParams(
            dimension_semantics=("arbitrary",), vmem_limit_bytes=_vmem_limit(est)),
        name="mixc_sample",
    )(x, n1, win, vg, vb, scale, bias, wout)


def _lru_scan(a, b, h0):
    tm, w = a.shape
    groups = tm // SUBLANES
    a3 = a.reshape(groups, SUBLANES, w)
    b3 = b.reshape(groups, SUBLANES, w)
    sub = lax.broadcasted_iota(jnp.int32, (groups, SUBLANES, w), 1)
    for s in (1, 2, 4):
        a_sh = pltpu.roll(a3, s, 1)
        b_sh = pltpu.roll(b3, s, 1)
        keep = sub >= s
        b3 = jnp.where(keep, a3 * b_sh + b3, b3)
        a3 = jnp.where(keep, a3 * a_sh, a3)
    hprev = jnp.broadcast_to(h0, (SUBLANES, w))
    hs = []
    for r in range(groups):
        hr = a3[r] * hprev + b3[r]
        hs.append(hr)
        hprev = jnp.broadcast_to(hr[SUBLANES - 1:SUBLANES, :], (SUBLANES, w))
    return jnp.concatenate(hs, axis=0), hprev[0:1, :]


def _mixa_prompt_kernel(gdec_ref, x_ref, xn_ref, n1_ref, win_ref, cos_ref, sin_ref, dmat_ref,
                        zeta_ref, xi_ref, gn_ref, cw_ref, cbias_ref,
                        wgate_ref, ba_ref, bx_ref, lam_ref, wout_ref,
                        wg_ref, wu_ref, wd_ref, winc_ref, woutc_ref,
                        o_ref, s_ref, hT_ref, cT_ref,
                        wg_o, wu_o, wd_o, winc_o, woutc_o,
                        z_s, xbn_s, xp_s, ymix_s, sb_s, *, tm, nt):
    half = tm // 2
    w = RET_WIDTH
    pad = SUBLANES
    tail = CONV_W - 1
    step = pl.program_id(0)
    ZQ, ZK, ZV, ZG, ZGB = range(5)

    pw = PIECE_COLS
    per_blk = w // pw

    def stash_xb(j, v):
        xbn_s[:, j * pw:(j + 1) * pw] = v

    def window_xb(j, v):
        xp_s[pad + half:pad + tm, j * pw:(j + 1) * pw] = v

    def projection(src_ref, src_row0, z_row0, put_xb):
        hn = []

        def xb_piece(j):
            def run():
                if j == 0:
                    hn.append(
                        _rms(src_ref[src_row0:src_row0 + half, :], n1_ref[...]).astype(BF16))
                put_xb(j, _dot(hn[0], win_ref[4 * per_blk + j]))
            return run

        def z_piece(win_blk, z_blk, j):
            def run():
                z_s[z_row0:z_row0 + half, z_blk * w + j * pw:z_blk * w + (j + 1) * pw] = _dot(
                    hn[0], win_ref[win_blk * per_blk + j])
            return run

        pieces = [xb_piece(j) for j in range(per_blk)]
        for win_blk, z_blk in ((5, ZGB), (0, ZQ), (1, ZK), (2, ZV), (3, ZG)):
            pieces += [z_piece(win_blk, z_blk, j) for j in range(per_blk)]
        return pieces

    @pl.when(step == 0)
    def _():
        for piece in projection(x_ref, 0, 0, stash_xb):
            piece()

    @pl.when(lax.rem(step, nt) == 0)
    def _():
        s_ref[...] = jnp.zeros_like(s_ref)
        hT_ref[...] = jnp.zeros_like(hT_ref)
        cT_ref[...] = jnp.zeros_like(cT_ref)

    xp_s[pad - tail:pad, :] = cT_ref[0]
    xp_s[pad:pad + half, :] = xbn_s[...]
    for h in range(RET_HEADS):
        sb_s[h] = s_ref[0, h].astype(BF16)
    hprev = [hT_ref[0]]
    sp = LRU_C * _softplus(-lam_ref[...])
    heads = range(RET_HEADS)

    def lru_unit(c, emit):
        rows = slice(c * RET_CHUNK, (c + 1) * RET_CHUNK)
        xc = cbias_ref[...]
        for j in reversed(range(CONV_W)):
            off = pad - tail + j + c * RET_CHUNK
            xc = xc + xp_s[off:off + RET_CHUNK, :] * cw_ref[j:j + 1, :]
        emit()
        a, b_in = _lru_coeffs(xc, wgate_ref, ba_ref[...], bx_ref[...], sp)
        emit()
        hl, hprev[0] = _lru_scan(a, b_in, hprev[0])
        ymix_s[rows, RET_WIDTH:] = (hl * _gelu(z_s[rows, ZGB * w:(ZGB + 1) * w])).astype(BF16)

    def ret_unit(c, emit):
        rows = slice(c * RET_CHUNK, (c + 1) * RET_CHUNK)
        zcol = lambda blk, h: slice(blk * w + h * LANES, blk * w + (h + 1) * LANES)
        cols = lambda h: slice(h * RET_DK, (h + 1) * RET_DK)
        cosf = cos_ref[rows, :]
        sinf = sin_ref[rows, :]
        qr = [_rotary(z_s[rows, zcol(ZQ, h)], cosf, sinf) * Q_SCALE for h in heads]
        kr = [_rotary(z_s[rows, zcol(ZK, h)], cosf, sinf) for h in heads]
        vb = [z_s[rows, zcol(ZV, h)].astype(BF16) for h in heads]
        emit()
        sc = [lax.dot_general(qr[h], kr[h].astype(BF16), (((1,), (1,)), ((), ())),
                              preferred_element_type=F32) for h in heads]
        emit()
        lhs = [jnp.concatenate([sc[h] * dmat_ref[h], qr[h] * xi_ref[h]], axis=1) for h in heads]
        o = [_dot(lhs[h], jnp.concatenate([vb[h], sb_s[h]], axis=0)) for h in heads]
        u = [lax.dot_general((kr[h] * zeta_ref[h]).astype(BF16), vb[h], (((0,), (0,)), ((), ())),
                             preferred_element_type=F32) for h in heads]
        emit()
        for h in heads:
            ya = _group_norm(o[h]) * gn_ref[:, cols(h)] * _silu(z_s[rows, zcol(ZG, h)])
            ymix_s[rows, cols(h)] = ya.astype(BF16)
        emit()
        for h in heads:
            s_new = gdec_ref[h] * s_ref[0, h] + u[h]
            s_ref[0, h] = s_new
            sb_s[h] = s_new.astype(BF16)

    def out_proj(row0):
        rows = slice(row0, row0 + half)

        def piece(j):
            def run():
                cols = slice(j * pw, (j + 1) * pw)
                o_ref[rows, cols] = x_ref[rows, cols] + _dot(ymix_s[rows, :], wout_ref[j])
            return run

        return [piece(j) for j in range(D_MODEL // pw)]

    def run_half(first_chunk, pieces):
        pending = iter(pieces)

        def emit():
            for _ in range(PIECES_PER_STAGE):
                next(pending, lambda: None)()

        for c in range(first_chunk, first_chunk + half // RET_CHUNK):
            emit()
            lru_unit(c, emit)
            emit()
            ret_unit(c, emit)
        for piece in pending:
            piece()

    run_half(0, projection(x_ref, half, half, window_xb))
    run_half(half // RET_CHUNK, out_proj(0) + projection(xn_ref, 0, 0, stash_xb))
    for piece in out_proj(half):
        piece()

    hT_ref[0] = hprev[0]
    cT_ref[0] = xp_s[pad + tm - tail:pad + tm, :]

    wg_o[...] = wg_ref[...].astype(BF16)
    wu_o[...] = wu_ref[...].astype(BF16)
    wd_o[...] = wd_ref[...].astype(BF16)
    for i in range(2 * SG_PIECES):
        winc_o[i] = winc_ref[:, i * pw:(i + 1) * pw].astype(BF16)
    for j in range(D_MODEL // pw):
        woutc_o[j] = woutc_ref[:, j * pw:(j + 1) * pw].astype(BF16)


def _mixa_prompt(x, gdec, n1, win, cosf, sinf, dmat, zeta, xi, gn, cw, cbias,
                 wgate, ba, bx, lam, wout, wg, wu, wd, winc, woutc, *, batch, seq, tm):
    nt = seq // tm
    steps = batch * nt
    nlayer = wg.shape[0]
    per_layer = steps // nlayer
    ff_rows = D_MODEL // per_layer
    fd_rows = D_FF // per_layer
    c_rows = D_MODEL // steps
    band = lambda i: (i // per_layer, lax.rem(i, per_layer), 0)
    half = tm // 2
    in_w = 6 * RET_WIDTH
    est = (D_MODEL * in_w * 4 + D_MODEL * D_MODEL * 4 + 6 * tm * D_MODEL * 4
           + tm * 5 * RET_WIDTH * 4 + 4 * tm * LRU_WIDTH * 4)
    row_blk = lambda i: (i, 0)
    next_half = lambda i: (jnp.minimum(2 * (i + 1), 2 * steps - 2), 0)
    pos_blk = lambda i: (lax.rem(i, nt), 0)
    per_b4 = lambda i: (i // nt, 0, 0, 0)
    per_b3 = lambda i: (i // nt, 0, 0)
    hd = (RET_HEADS, RET_CHUNK, RET_CHUNK)
    return pl.pallas_call(
        functools.partial(_mixa_prompt_kernel, tm=tm, nt=nt),
        grid=(steps,),
        in_specs=[
            pl.BlockSpec(memory_space=pltpu.SMEM),
            pl.BlockSpec((tm, D_MODEL), row_blk),
            pl.BlockSpec((half, D_MODEL), next_half),
            _resident((1, D_MODEL)),
            _resident((in_w // PIECE_COLS, D_MODEL, PIECE_COLS)),
            pl.BlockSpec((tm, RET_DK), pos_blk),
            pl.BlockSpec((tm, RET_DK), pos_blk),
            _resident(hd),
            _resident(hd),
            _resident(hd),
            _resident((1, RET_WIDTH)),
            _resident((CONV_W, LRU_WIDTH)),
            _resident((1, LRU_WIDTH)),
            _resident((LRU_WIDTH // LANES, LANES, 2 * LANES)),
            _resident((1, LRU_WIDTH)),
            _resident((1, LRU_WIDTH)),
            _resident((1, LRU_WIDTH)),
            _resident((D_MODEL // PIECE_COLS, D_MODEL, PIECE_COLS)),
            pl.BlockSpec((1, ff_rows, D_FF), band),
            pl.BlockSpec((1, ff_rows, D_FF), band),
            pl.BlockSpec((1, fd_rows, D_MODEL), band),
            pl.BlockSpec((c_rows, 2 * SG_HALF), row_blk),
            pl.BlockSpec((c_rows, D_MODEL), row_blk),
        ],
        out_specs=[
            pl.BlockSpec((tm, D_MODEL), row_blk),
            pl.BlockSpec((1, RET_HEADS, RET_DK, RET_DK), per_b4),
            pl.BlockSpec((1, 1, LRU_WIDTH), per_b3),
            pl.BlockSpec((1, CONV_W - 1, LRU_WIDTH), per_b3),
            pl.BlockSpec((1, ff_rows, D_FF), band),
            pl.BlockSpec((1, ff_rows, D_FF), band),
            pl.BlockSpec((1, fd_rows, D_MODEL), band),
            pl.BlockSpec((2 * SG_PIECES, c_rows, PIECE_COLS), lambda i: (0, i, 0)),
            pl.BlockSpec((D_MODEL // PIECE_COLS, c_rows, PIECE_COLS), lambda i: (0, i, 0)),
        ],
        out_shape=[
            jax.ShapeDtypeStruct((batch * seq, D_MODEL), F32),
            jax.ShapeDtypeStruct((batch, RET_HEADS, RET_DK, RET_DK), F32),
            jax.ShapeDtypeStruct((batch, 1, LRU_WIDTH), F32),
            jax.ShapeDtypeStruct((batch, CONV_W - 1, LRU_WIDTH), F32),
            jax.ShapeDtypeStruct(wg.shape, BF16),
            jax.ShapeDtypeStruct(wu.shape, BF16),
            jax.ShapeDtypeStruct(wd.shape, BF16),
            jax.ShapeDtypeStruct((2 * SG_PIECES, D_MODEL, PIECE_COLS), BF16),
            jax.ShapeDtypeStruct((D_MODEL // PIECE_COLS, SG_HALF, PIECE_COLS), BF16),
        ],
        scratch_shapes=[
            pltpu.VMEM((tm, 5 * RET_WIDTH), F32),
            pltpu.VMEM((half, LRU_WIDTH), F32),
            pltpu.VMEM((tm + 2 * SUBLANES, LRU_WIDTH), F32),
            pltpu.VMEM((tm, D_MODEL), BF16),
            pltpu.VMEM((RET_HEADS, RET_DK, RET_DK), BF16),
        ],
        compiler_params=pltpu.CompilerParams(
            dimension_semantics=("arbitrary",), vmem_limit_bytes=_vmem_limit(est)),
        name="mixa_prompt",
    )(gdec, x, x, n1, win, cosf, sinf, dmat, zeta, xi, gn, cw, cbias, wgate,
      ba, bx, lam, wout, wg, wu, wd, winc, woutc)


def _mixa_sample_kernel(gdec_ref, x_ref, n1_ref, win_ref, cos_ref, sin_ref,
                        s0_ref, h0_ref, cb0_ref, gn_ref, cw_ref, cbias_ref, wgate_ref,
                        ba_ref, bx_ref, lam_ref, wout_ref,
                        o_ref, s_ref, hT_ref, cT_ref,
                        q_s, k_s, v_s, sg_s, ymix_s, *, bt):
    step = pl.program_id(0)
    w = RET_WIDTH

    @pl.when(step == 0)
    def _():
        hn = _rms(x_ref[...], n1_ref[...])
        per_blk = w // PIECE_COLS
        proj = lambda blk: jnp.concatenate(
            [_dot(hn, win_ref[blk * per_blk + j]) for j in range(per_blk)], axis=1)
        cosf = cos_ref[...]
        sinf = sin_ref[...]
        zq = proj(0)
        zk = proj(1)
        for h in range(RET_HEADS):
            cols = slice(h * RET_DK, (h + 1) * RET_DK)
            q_s[:, cols] = (_rotary(zq[:, cols], cosf, sinf) * Q_SCALE).astype(BF16)
            k_s[:, cols] = _rotary(zk[:, cols], cosf, sinf).astype(BF16)
        v_s[...] = proj(2)
        sg_s[...] = _silu(proj(3))
        xb = proj(4)
        gg = _gelu(proj(5))
        xc = cbias_ref[...]
        for j in range(CONV_W - 1):
            xc = xc + cb0_ref[j] * cw_ref[j:j + 1, :]
        xc = xc + xb * cw_ref[CONV_W - 1:CONV_W, :]
        for j in range(CONV_W - 2):
            cT_ref[j] = cb0_ref[j + 1]
        cT_ref[CONV_W - 2] = xb
        a, b_in = _lru_coeffs(xc, wgate_ref, ba_ref[...], bx_ref[...],
                              LRU_C * _softplus(-lam_ref[...]))
        hnew = a * h0_ref[...] + b_in
        hT_ref[...] = hnew
        ymix_s[:, RET_WIDTH:] = (hnew * gg).astype(BF16)

    r0 = pl.multiple_of(step * bt, bt)
    rows = pl.ds(r0, bt)
    rowi = lax.broadcasted_iota(jnp.int32, (bt, RET_DK), 0)
    for h in range(RET_HEADS):
        cols = slice(h * RET_DK, (h + 1) * RET_DK)
        qs = q_s[rows, cols]
        kb = k_s[rows, cols]
        vh = v_s[rows, cols]
        qk = jnp.sum(qs.astype(F32) * kb.astype(F32), axis=-1, keepdims=True)
        cross = jnp.zeros((bt, RET_DK), F32)
        for b in range(bt):
            s_old = s0_ref[b, h]
            cr = _dot(qs, s_old.astype(BF16))
            cross = cross + jnp.where(rowi == b, cr, 0.0)
            u = lax.dot_general(kb, jnp.where(rowi == b, vh, 0.0).astype(BF16),
                                (((0,), (0,)), ((), ())), preferred_element_type=F32)
            s_ref[b, h] = gdec_ref[h] * s_old + u
        o = qk * vh + cross * gdec_ref[h]
        ymix_s[rows, cols] = (_group_norm(o) * gn_ref[:, cols] * sg_s[rows, cols]).astype(BF16)

    @pl.when(step == pl.num_programs(0) - 1)
    def _():
        ymix = ymix_s[...]
        o_ref[...] = x_ref[...] + jnp.concatenate(
            [_dot(ymix, wout_ref[j]) for j in range(D_MODEL // PIECE_COLS)], axis=1)


def _mixa_sample(x, gdec, n1, win, cosf, sinf, s0, h0, cb0, gn, cw, cbias,
                 wgate, ba, bx, lam, wout, *, bt):
    batch = x.shape[0]
    in_w = 6 * RET_WIDTH
    est = (D_MODEL * in_w * 2 + D_MODEL * D_MODEL * 2
           + 4 * bt * RET_HEADS * RET_DK * RET_DK * 4 + 12 * batch * in_w * 4)
    return pl.pallas_call(
        functools.partial(_mixa_sample_kernel, bt=bt),
        grid=(batch // bt,),
        in_specs=[
            pl.BlockSpec(memory_space=pltpu.SMEM),
            _resident((batch, D_MODEL)),
            _resident((1, D_MODEL)),
            _resident((in_w // PIECE_COLS, D_MODEL, PIECE_COLS)),
            _resident((1, RET_DK)),
            _resident((1, RET_DK)),
            pl.BlockSpec((bt, RET_HEADS, RET_DK, RET_DK), lambda i: (i, 0, 0, 0)),
            _resident((batch, LRU_WIDTH)),
            _resident((CONV_W - 1, batch, LRU_WIDTH)),
            _resident((1, RET_WIDTH)),
            _resident((CONV_W, LRU_WIDTH)),
            _resident((1, LRU_WIDTH)),
            _resident((LRU_WIDTH // LANES, LANES, 2 * LANES)),
            _resident((1, LRU_WIDTH)),
            _resident((1, LRU_WIDTH)),
            _resident((1, LRU_WIDTH)),
            _resident((D_MODEL // PIECE_COLS, D_MODEL, PIECE_COLS)),
        ],
        out_specs=[
            pl.BlockSpec((batch, D_MODEL), lambda i: (0, 0)),
            pl.BlockSpec((bt, RET_HEADS, RET_DK, RET_DK), lambda i: (i, 0, 0, 0)),
            pl.BlockSpec((batch, LRU_WIDTH), lambda i: (0, 0)),
            pl.BlockSpec((CONV_W - 1, batch, LRU_WIDTH), lambda i: (0, 0, 0)),
        ],
        out_shape=[
            jax.ShapeDtypeStruct((batch, D_MODEL), F32),
            jax.ShapeDtypeStruct((batch, RET_HEADS, RET_DK, RET_DK), F32),
            jax.ShapeDtypeStruct((batch, LRU_WIDTH), F32),
            jax.ShapeDtypeStruct((CONV_W - 1, batch, LRU_WIDTH), F32),
        ],
        scratch_shapes=[
            pltpu.VMEM((batch, RET_WIDTH), BF16),
            pltpu.VMEM((batch, RET_WIDTH), BF16),
            pltpu.VMEM((batch, RET_WIDTH), F32),
            pltpu.VMEM((batch, RET_WIDTH), F32),
            pltpu.VMEM((batch, D_MODEL), BF16),
        ],
        compiler_params=pltpu.CompilerParams(
            dimension_semantics=("arbitrary",), vmem_limit_bytes=_vmem_limit(est)),
        name="mixa_sample",
    )(gdec, x, n1, win, cosf, sinf, s0, h0, cb0, gn, cw, cbias, wgate,
      ba, bx, lam, wout)


def _rope_tables(pos):
    half = RET_DK // 2
    inv = ROPE_BASE ** (-jnp.arange(half, dtype=F32) / half)
    ang = pos[:, None] * inv[None, :]
    cos, sin = jnp.cos(ang), jnp.sin(ang)
    return jnp.concatenate([cos, cos], axis=1), jnp.concatenate([-sin, sin], axis=1)


def _decay_tables(c):
    lg = jnp.log1p(-jnp.exp2(-5.0 - jnp.arange(RET_HEADS, dtype=F32)))
    idx = jnp.arange(c, dtype=F32)
    diff = idx[:, None] - idx[None, :]
    causal = diff >= 0
    dmat = jnp.where(causal[None], jnp.exp(jnp.where(causal, diff, 0.0)[None] * lg[:, None, None]), 0.0)
    zeta = jnp.exp((c - 1.0 - idx)[None, :] * lg[:, None])
    xi = jnp.exp((idx + 1.0)[None, :] * lg[:, None])
    gdec = jnp.exp(c * lg)
    return lg, dmat, zeta, xi, gdec


def _gate_tiles(wa, wx):
    z = jnp.zeros((LRU_BW, LRU_BW), wa.dtype)
    bd = lambda w, g: jnp.block([[w[2 * g], z], [z, w[2 * g + 1]]])
    tiles = [jnp.concatenate([bd(wa, g), bd(wx, g)], axis=1) for g in range(LRU_BLOCKS // 2)]
    return jnp.stack(tiles).astype(BF16)


def kernel(x_prompt, x_sample, state_ret, state_lru, state_conv, norm1, norm2, norm_f, w_in_a, ret_gn,
           conv_w, conv_b, lru_wa, lru_ba, lru_wx, lru_bx, lru_lambda, w_out_a, w_in_c, sg_norm_g,
           sg_norm_b, sg_ws, sg_bs, w_out_c, ffn_wg, ffn_wu, ffn_wd):
    bp, lp, _ = x_prompt.shape
    bs = x_sample.shape[0]
    row = lambda v: v.reshape(1, -1)

    slabs = lambda wmat: jnp.stack(
        [wmat[:, i * PIECE_COLS:(i + 1) * PIECE_COLS].astype(BF16)
         for i in range(wmat.shape[1] // PIECE_COLS)])
    win_a = slabs(w_in_a[0])
    wout_a = slabs(w_out_a[0])
    wgate = _gate_tiles(lru_wa[0], lru_wx[0])
    mixa_w = (row(ret_gn[0]), conv_w[0], row(conv_b[0]), wgate, row(lru_ba[0]),
              row(lru_bx[0]), row(lru_lambda[0]), wout_a)

    tm = 512
    bt = 16
    cos_p, sin_p = _rope_tables(jnp.arange(lp, dtype=F32))
    cos_s, sin_s = _rope_tables(PAST_LEN + jnp.arange(1, dtype=F32))
    _, dmat, zeta, xi, gdec_p = _decay_tables(RET_CHUNK)
    _, _, _, _, gdec_s = _decay_tables(1)
    bc = lambda t: jnp.broadcast_to(t[:, :, None], (RET_HEADS, RET_CHUNK, RET_DK))
    xp = x_prompt.reshape(bp * lp, D_MODEL)
    xs = x_sample.reshape(bs, D_MODEL)

    xp, ret_p, lru_p, conv_p, wg, wu, wd, win_c, wout_c = _mixa_prompt(
        xp, gdec_p, row(norm1[0]), win_a, cos_p, sin_p, dmat, bc(zeta), bc(xi),
        *mixa_w, ffn_wg, ffn_wu, ffn_wd, w_in_c[0], w_out_c[0], batch=bp, seq=lp, tm=tm)
    ffn = lambda xp_, xs_, layer, final: _ffn(
        xp_, xs_, row(norm2[layer]), wg, wu, wd, row(norm_f), layer=layer,
        final_norm=final, tm=tm)
    xs, ret_s, lru_s, conv_s = _mixa_sample(
        xs, gdec_s, row(norm1[0]), win_a, cos_s, sin_s, state_ret[0],
        state_lru[0], jnp.transpose(state_conv[0], (1, 0, 2)), *mixa_w, bt=bt)
    xp, xs = ffn(xp, xs, 0, False)

    bsb = jnp.broadcast_to(sg_bs[0][:, :, None], (SG_GROUPS, SG_CHUNK, SG_GW))
    xp = _mixc_prompt(xp, row(norm1[1]), win_c, row(sg_norm_g[0]), row(sg_norm_b[0]), sg_ws[0],
                      bsb, wout_c, tm=2 * tm)
    sg_scale = row(jnp.repeat(sg_ws[0][:, 0, 0], SG_GW))
    sg_bias = row(jnp.repeat(sg_bs[0][:, 0], SG_GW))
    xs, v_s = _mixc_sample(xs, row(norm1[1]), win_c, row(sg_norm_g[0]), row(sg_norm_b[0]),
                           sg_scale, sg_bias, wout_c)
    y_prompt, y_sample = ffn(xp, xs, 1, True)

    return (y_prompt.reshape(bp, lp, D_MODEL),
            y_sample.reshape(bs, 1, D_MODEL),
            ret_p[None],
            ret_s[None],
            lru_p.reshape(1, bp, LRU_WIDTH),
            lru_s[None],
            conv_p[None],
            jnp.transpose(conv_s, (1, 0, 2))[None],
            v_s.reshape(1, bs, 1, SG_HALF))
```
